```python
import jax, jax.numpy as jnp
from jax import lax
import numpy as np

D_MODEL = 1024
BATCH = 8
SEQ = 8192
DEPTH = 1

N_META = 16
CHUNK = 128
PAD = CHUNK - N_META
RET_HEADS = 4
RET_DK = D_MODEL // RET_HEADS
RET_DV = 2 * RET_DK
SB_HEADS = D_MODEL // 64
SB_DH = 64
D_FF = -(-8 * D_MODEL // (3 * 256)) * 256
ROPE_BASE = 10000.0
NORM_EPS = 1e-6
GN_EPS = 1e-5
PROJ_SPLIT = (
    RET_HEADS * RET_DK,
    RET_HEADS * RET_DK,
    RET_HEADS * RET_DV,
    RET_HEADS * RET_DV,
    SB_HEADS * SB_DH,
    SB_HEADS * SB_DH,
    SB_HEADS * SB_DH,
    D_MODEL,
    D_MODEL,
)
PROJ_WIDTH = sum(PROJ_SPLIT)

kernel_name = "hybrid_retention_stickbreaking_block"


def rmsnorm(x, g):
    xf = x.astype(jnp.float32)
    y = xf * lax.rsqrt(jnp.mean(xf * xf, axis=-1, keepdims=True) + NORM_EPS)
    return (y * g.astype(jnp.float32)).astype(x.dtype)


def rotary(x, pos):
    half = x.shape[-1] // 2
    inv = ROPE_BASE ** (-jnp.arange(half, dtype=jnp.float32) / half)
    ang = pos[:, None] * inv[None, :]
    cos, sin = jnp.cos(ang), jnp.sin(ang)
    x1, x2 = x[..., :half], x[..., half:]
    return jnp.concatenate([x1 * cos - x2 * sin, x1 * sin + x2 * cos], axis=-1)


def retention_chunkwise(q, k, v):
    B, H, Lp, dk = q.shape
    dv = v.shape[-1]
    n = Lp // CHUNK
    log_g = jnp.log1p(-(2.0 ** (-5.0 - jnp.arange(H, dtype=jnp.float32))))
    idx = jnp.arange(CHUNK, dtype=jnp.float32)
    diff = idx[:, None] - idx[None, :]
    decay = jnp.where(diff >= 0, jnp.exp(log_g[:, None, None] * jnp.maximum(diff, 0.0)), 0.0)
    zeta = jnp.exp(log_g[:, None] * (CHUNK - 1.0 - idx))[:, :, None]
    xi = jnp.exp(log_g[:, None] * (idx + 1.0))[:, :, None]
    g_chunk = jnp.exp(log_g * CHUNK)[:, None, None]

    qc = q.reshape(B, H, n, CHUNK, dk)
    kc = k.reshape(B, H, n, CHUNK, dk)
    vc = v.reshape(B, H, n, CHUNK, dv)
    scores = jnp.einsum('bhncd,bhnsd->bhncs', qc, kc) * decay[:, None]
    inner = jnp.einsum('bhncs,bhnse->bhnce', scores, vc)

    def step(state, inp):
        q_i, k_i, v_i = inp
        cross = jnp.einsum('bhcd,bhde->bhce', q_i, state) * xi
        state = g_chunk * state + jnp.einsum('bhsd,bhse->bhde', k_i, v_i * zeta)
        return state, cross

    s0 = jnp.zeros((B, H, dk, dv), jnp.float32)
    _, cross = lax.scan(step, s0, (qc.transpose(2, 0, 1, 3, 4), kc.transpose(2, 0, 1, 3, 4), vc.transpose(2, 0, 1, 3, 4)))
    out = inner + cross.transpose(1, 2, 0, 3, 4)
    return out.reshape(B, H, Lp, dv)


def head_groupnorm(y):
    mu = jnp.mean(y, axis=-1, keepdims=True)
    var = jnp.mean(jnp.square(y - mu), axis=-1, keepdims=True)
    return (y - mu) * lax.rsqrt(var + GN_EPS)


def stick_breaking(q, k, v):
    Lp, d = q.shape[2], q.shape[3]
    scale = d ** -0.5
    outs = []
    for i in range(Lp // CHUNK):
        lk = (i + 1) * CHUNK
        q_blk = q[:, :, i * CHUNK:lk]
        k_pre, v_pre = k[:, :, :lk], v[:, :, :lk]
        z = jnp.einsum('bhtd,bhsd->bhts', q_blk, k_pre) * scale
        t_pos = i * CHUNK + jnp.arange(CHUNK)
        s_pos = jnp.arange(lk)
        valid = (s_pos[None, :] < t_pos[:, None]) & (s_pos[None, :] >= PAD)
        log_not = jnp.where(valid, jax.nn.log_sigmoid(-z), 0.0)
        after = lax.cumsum(log_not, axis=3, reverse=True) - log_not
        a = jnp.where(valid, jnp.exp(jax.nn.log_sigmoid(z) + after), 0.0)
        outs.append(jnp.einsum('bhts,bhse->bhte', a, v_pre))
    return jnp.concatenate(outs, axis=2)


def hybrid_mixer(hn, w_in, w_ret_out, w_sb_out, w_out):
    B, L, _ = hn.shape
    dtype = hn.dtype
    proj = hn @ w_in
    rq, rk, rv, rg, sq, sk, sv, ga, gb = jnp.split(proj, list(np.cumsum(PROJ_SPLIT)[:-1]), axis=-1)

    def heads(t, h):
        t = t.reshape(B, L, h, -1).transpose(0, 2, 1, 3).astype(jnp.float32)
        return jnp.pad(t, ((0, 0), (0, 0), (PAD, 0), (0, 0)))

    def merge_heads(t):
        t = t[:, :, PAD:]
        return t.transpose(0, 2, 1, 3).reshape(B, L, -1)

    pos = jnp.arange(L + PAD, dtype=jnp.float32) - PAD
    q_r = rotary(heads(rq, RET_HEADS), pos) * (RET_DK ** -0.5)
    k_r = rotary(heads(rk, RET_HEADS), pos)
    y_ret = head_groupnorm(retention_chunkwise(q_r, k_r, heads(rv, RET_HEADS)))
    y_ret = (jax.nn.silu(rg.astype(jnp.float32)) * merge_heads(y_ret)).astype(dtype) @ w_ret_out
    y_sb = merge_heads(stick_breaking(heads(sq, SB_HEADS), heads(sk, SB_HEADS), heads(sv, SB_HEADS)))
    y_sb = y_sb.astype(dtype) @ w_sb_out
    merged = jax.nn.sigmoid(ga) * y_ret + jax.nn.sigmoid(gb) * y_sb
    return merged @ w_out


def swiglu(hn, w_ffn_in, w_ffn_out):
    a, b = jnp.split(hn @ w_ffn_in, 2, axis=-1)
    return (jax.nn.silu(a) * b) @ w_ffn_out


def _fwd_setup_inputs(seed: int = 0) -> dict:
    key = jax.random.key(seed)
    ks = jax.random.split(key, 12)
    f32 = jnp.float32
    nrm = lambda k, shape, fan: jax.random.normal(k, shape, f32) * (fan ** -0.5)
    gain = lambda k: 1.0 + 0.02 * jax.random.normal(k, (DEPTH, D_MODEL), f32)
    return {
        "x": jax.random.normal(ks[0], (BATCH, SEQ, D_MODEL), f32),
        "meta_tokens": jax.random.normal(ks[1], (N_META, D_MODEL), f32),
        "w_in": nrm(ks[2], (DEPTH, D_MODEL, PROJ_WIDTH), D_MODEL),
        "w_ret_out": nrm(ks[3], (DEPTH, RET_HEADS * RET_DV, D_MODEL), RET_HEADS * RET_DV),
        "w_sb_out": nrm(ks[4], (DEPTH, SB_HEADS * SB_DH, D_MODEL), SB_HEADS * SB_DH),
        "w_out": nrm(ks[5], (DEPTH, D_MODEL, D_MODEL), D_MODEL),
        "w_ffn_in": nrm(ks[6], (DEPTH, D_MODEL, 2 * D_FF), D_MODEL),
        "w_ffn_out": nrm(ks[7], (DEPTH, D_FF, D_MODEL), D_FF),
        "norm_mix_pre": gain(ks[8]),
        "norm_mix_post": gain(ks[9]),
        "norm_ffn_pre": gain(ks[10]),
        "norm_ffn_post": gain(ks[11]),
    }


def _fwd_reference(x, meta_tokens, w_in, w_ret_out, w_sb_out, w_out, w_ffn_in, w_ffn_out,
              norm_mix_pre, norm_mix_post, norm_ffn_pre, norm_ffn_post):
    B = x.shape[0]
    meta = jnp.broadcast_to(meta_tokens.astype(x.dtype)[None], (B, N_META, x.shape[-1]))
    h = jnp.concatenate([meta, x], axis=1)
    for l in range(DEPTH):
        mix = hybrid_mixer(rmsnorm(h, norm_mix_pre[l]), w_in[l], w_ret_out[l], w_sb_out[l], w_out[l])
        h = h + rmsnorm(mix, norm_mix_post[l])
        ff = swiglu(rmsnorm(h, norm_ffn_pre[l]), w_ffn_in[l], w_ffn_out[l])
        h = h + rmsnorm(ff, norm_ffn_post[l])
    return h[:, N_META:]


import jax as _jax
import jax.numpy as _jnp

TWIN_FORMAT = 'train_step'
FWD_PARAMS = ['x', 'meta_tokens', 'w_in', 'w_ret_out', 'w_sb_out', 'w_out', 'w_ffn_in', 'w_ffn_out', 'norm_mix_pre', 'norm_mix_post', 'norm_ffn_pre', 'norm_ffn_post']
TWIN_WEIGHTS = ['meta_tokens', 'w_in', 'w_ret_out', 'w_sb_out', 'w_out', 'w_ffn_in', 'w_ffn_out', 'norm_mix_pre', 'norm_mix_post', 'norm_ffn_pre', 'norm_ffn_post']
TWIN_DIFF_INPUT = 'x'
TWIN_INPUTS = ['x', 'meta_tokens', 'w_in', 'w_ret_out', 'w_sb_out', 'w_out', 'w_ffn_in', 'w_ffn_out', 'norm_mix_pre', 'norm_mix_post', 'norm_ffn_pre', 'norm_ffn_post', 'loss_target', 'm_meta_tokens', 'm_w_in', 'm_w_ret_out', 'm_w_sb_out', 'm_w_out', 'm_w_ffn_in', 'm_w_ffn_out', 'm_norm_mix_pre', 'm_norm_mix_post', 'm_norm_ffn_pre', 'm_norm_ffn_post', 'v_meta_tokens', 'v_w_in', 'v_w_ret_out', 'v_w_sb_out', 'v_w_out', 'v_w_ffn_in', 'v_w_ffn_out', 'v_norm_mix_pre', 'v_norm_mix_post', 'v_norm_ffn_pre', 'v_norm_ffn_post']
TWIN_OUTPUTS = ['loss', 'grad_x', 'grad_meta_tokens', 'grad_w_in', 'grad_w_ret_out', 'grad_w_sb_out', 'grad_w_out', 'grad_w_ffn_in', 'grad_w_ffn_out', 'grad_norm_mix_pre', 'grad_norm_mix_post', 'grad_norm_ffn_pre', 'grad_norm_ffn_post', 'delta_meta_tokens', 'delta_w_in', 'delta_w_ret_out', 'delta_w_sb_out', 'delta_w_out', 'delta_w_ffn_in', 'delta_w_ffn_out', 'delta_norm_mix_pre', 'delta_norm_mix_post', 'delta_norm_ffn_pre', 'delta_norm_ffn_post', 'new_m_meta_tokens', 'new_m_w_in', 'new_m_w_ret_out', 'new_m_w_sb_out', 'new_m_w_out', 'new_m_w_ffn_in', 'new_m_w_ffn_out', 'new_m_norm_mix_pre', 'new_m_norm_mix_post', 'new_m_norm_ffn_pre', 'new_m_norm_ffn_post', 'new_v_meta_tokens', 'new_v_w_in', 'new_v_w_ret_out', 'new_v_w_sb_out', 'new_v_w_out', 'new_v_w_ffn_in', 'new_v_w_ffn_out', 'new_v_norm_mix_pre', 'new_v_norm_mix_post', 'new_v_norm_ffn_pre', 'new_v_norm_ffn_post']
TWIN_LEAF_KINDS = {'loss': 'loss', 'grad_x': 'grad_x', 'grad_meta_tokens': 'grad_w', 'grad_w_in': 'grad_w', 'grad_w_ret_out': 'grad_w', 'grad_w_sb_out': 'grad_w', 'grad_w_out': 'grad_w', 'grad_w_ffn_in': 'grad_w', 'grad_w_ffn_out': 'grad_w', 'grad_norm_mix_pre': 'grad_w', 'grad_norm_mix_post': 'grad_w', 'grad_norm_ffn_pre': 'grad_w', 'grad_norm_ffn_post': 'grad_w', 'delta_meta_tokens': 'delta_w', 'delta_w_in': 'delta_w', 'delta_w_ret_out': 'delta_w', 'delta_w_sb_out': 'delta_w', 'delta_w_out': 'delta_w', 'delta_w_ffn_in': 'delta_w', 'delta_w_ffn_out': 'delta_w', 'delta_norm_mix_pre': 'delta_w', 'delta_norm_mix_post': 'delta_w', 'delta_norm_ffn_pre': 'delta_w', 'delta_norm_ffn_post': 'delta_w', 'new_m_meta_tokens': 'new_m', 'new_m_w_in': 'new_m', 'new_m_w_ret_out': 'new_m', 'new_m_w_sb_out': 'new_m', 'new_m_w_out': 'new_m', 'new_m_w_ffn_in': 'new_m', 'new_m_w_ffn_out': 'new_m', 'new_m_norm_mix_pre': 'new_m', 'new_m_norm_mix_post': 'new_m', 'new_m_norm_ffn_pre': 'new_m', 'new_m_norm_ffn_post': 'new_m', 'new_v_meta_tokens': 'new_v', 'new_v_w_in': 'new_v', 'new_v_w_ret_out': 'new_v', 'new_v_w_sb_out': 'new_v', 'new_v_w_out': 'new_v', 'new_v_w_ffn_in': 'new_v', 'new_v_w_ffn_out': 'new_v', 'new_v_norm_mix_pre': 'new_v', 'new_v_norm_mix_post': 'new_v', 'new_v_norm_ffn_pre': 'new_v', 'new_v_norm_ffn_post': 'new_v'}


def _forward(args):
    return _fwd_reference(*[args[k] for k in FWD_PARAMS])


def _output_shape():
    def fwd():
        inp = _fwd_setup_inputs(0)
        return _fwd_reference(*[inp[k] for k in FWD_PARAMS])
    out = _jax.eval_shape(fwd)
    return out.shape, out.dtype

N_MICROBATCH = 1
ADAM_LR = 0.001
ADAM_B1 = 0.9
ADAM_B2 = 0.999
ADAM_EPS = 1e-08
ADAM_WD = 0.01
ADAM_STEP = 10
PER_EXAMPLE_BATCH_AXIS = {'x': 0, 'loss_target': 0}
SHARED_INPUTS = []
_WEIGHT_DTYPES = {'meta_tokens': _jnp.float32, 'w_in': _jnp.float32, 'w_ret_out': _jnp.float32, 'w_sb_out': _jnp.float32, 'w_out': _jnp.float32, 'w_ffn_in': _jnp.float32, 'w_ffn_out': _jnp.float32, 'norm_mix_pre': _jnp.float32, 'norm_mix_post': _jnp.float32, 'norm_ffn_pre': _jnp.float32, 'norm_ffn_post': _jnp.float32}
MOMENT_SCALE = {'meta_tokens': 4.661970e-02, 'w_in': 3.184623e-01, 'w_ret_out': 4.410091e-01, 'w_sb_out': 5.051704e-01, 'w_out': 7.713813e-01, 'w_ffn_in': 3.837357e-01, 'w_ffn_out': 8.177595e-01, 'norm_mix_pre': 1.136365e+00, 'norm_mix_post': 6.396186e+01, 'norm_ffn_pre': 9.713218e-01, 'norm_ffn_post': 6.395034e+01}


def _to_microbatches(a, axis):
    t = _jnp.moveaxis(a, axis, 0)
    t = t.reshape((N_MICROBATCH, t.shape[0] // N_MICROBATCH) + t.shape[1:])
    return _jnp.moveaxis(t, 1, axis + 1)


def setup_inputs(seed: int = 0) -> dict:
    inp = _fwd_setup_inputs(seed)
    key = _jax.random.fold_in(_jax.random.key(seed), 7919)
    shape, _ = _output_shape()
    out = dict(inp)
    out["loss_target"] = _jax.random.normal(_jax.random.fold_in(key, 0), shape, _jnp.float32)
    for i, name in enumerate(TWIN_WEIGHTS):
        w = inp[name].astype(_jnp.float32)
        if MOMENT_SCALE is None:
            s = _jnp.sqrt(_jnp.mean(_jnp.square(w)) + 1e-30)
        else:
            s = MOMENT_SCALE[name]
        km, kv = _jax.random.split(_jax.random.fold_in(key, i + 1))
        out[name] = w
        out["m_" + name] = s * _jax.random.normal(km, w.shape, _jnp.float32)
        out["v_" + name] = (s * s) * _jax.random.uniform(kv, w.shape, _jnp.float32, 0.5, 1.5)
    if N_MICROBATCH > 1:
        for name, axis in PER_EXAMPLE_BATCH_AXIS.items():
            out[name] = _to_microbatches(out[name], axis)
    return {'x': out['x'], 'meta_tokens': out['meta_tokens'], 'w_in': out['w_in'], 'w_ret_out': out['w_ret_out'], 'w_sb_out': out['w_sb_out'], 'w_out': out['w_out'], 'w_ffn_in': out['w_ffn_in'], 'w_ffn_out': out['w_ffn_out'], 'norm_mix_pre': out['norm_mix_pre'], 'norm_mix_post': out['norm_mix_post'], 'norm_ffn_pre': out['norm_ffn_pre'], 'norm_ffn_post': out['norm_ffn_post'], 'loss_target': out['loss_target'], 'm_meta_tokens': out['m_meta_tokens'], 'm_w_in': out['m_w_in'], 'm_w_ret_out': out['m_w_ret_out'], 'm_w_sb_out': out['m_w_sb_out'], 'm_w_out': out['m_w_out'], 'm_w_ffn_in': out['m_w_ffn_in'], 'm_w_ffn_out': out['m_w_ffn_out'], 'm_norm_mix_pre': out['m_norm_mix_pre'], 'm_norm_mix_post': out['m_norm_mix_post'], 'm_norm_ffn_pre': out['m_norm_ffn_pre'], 'm_norm_ffn_post': out['m_norm_ffn_post'], 'v_meta_tokens': out['v_meta_tokens'], 'v_w_in': out['v_w_in'], 'v_w_ret_out': out['v_w_ret_out'], 'v_w_sb_out': out['v_w_sb_out'], 'v_w_out': out['v_w_out'], 'v_w_ffn_in': out['v_w_ffn_in'], 'v_w_ffn_out': out['v_w_ffn_out'], 'v_norm_mix_pre': out['v_norm_mix_pre'], 'v_norm_mix_post': out['v_norm_mix_post'], 'v_norm_ffn_pre': out['v_norm_ffn_pre'], 'v_norm_ffn_post': out['v_norm_ffn_post']}


def _loss(weights, diff, rest, loss_target):
    with _jax.named_scope("forward"):
        args = {**rest, TWIN_DIFF_INPUT: diff, **{k: w.astype(_WEIGHT_DTYPES[k]) for k, w in weights.items()}}
        y = _forward(args)
    with _jax.named_scope("loss_head"):
        err = _jnp.square(y.astype(_jnp.float32) - loss_target)
        return 0.5 * _jnp.sum(_jnp.mean(err, axis=-1)) if err.ndim else 0.5 * err


def _adamw(w, g, m, v):
    m = ADAM_B1 * m + (1.0 - ADAM_B1) * g
    v = ADAM_B2 * v + (1.0 - ADAM_B2) * _jnp.square(g)
    m_hat = m / (1.0 - ADAM_B1 ** ADAM_STEP)
    v_hat = v / (1.0 - ADAM_B2 ** ADAM_STEP)
    delta = -ADAM_LR * (m_hat / (_jnp.sqrt(v_hat) + ADAM_EPS) + ADAM_WD * w)
    return delta, m, v


def reference(x, meta_tokens, w_in, w_ret_out, w_sb_out, w_out, w_ffn_in, w_ffn_out, norm_mix_pre, norm_mix_post, norm_ffn_pre, norm_ffn_post, loss_target, m_meta_tokens, m_w_in, m_w_ret_out, m_w_sb_out, m_w_out, m_w_ffn_in, m_w_ffn_out, m_norm_mix_pre, m_norm_mix_post, m_norm_ffn_pre, m_norm_ffn_post, v_meta_tokens, v_w_in, v_w_ret_out, v_w_sb_out, v_w_out, v_w_ffn_in, v_w_ffn_out, v_norm_mix_pre, v_norm_mix_post, v_norm_ffn_pre, v_norm_ffn_post):
    given = dict(x=x, meta_tokens=meta_tokens, w_in=w_in, w_ret_out=w_ret_out, w_sb_out=w_sb_out, w_out=w_out, w_ffn_in=w_ffn_in, w_ffn_out=w_ffn_out, norm_mix_pre=norm_mix_pre, norm_mix_post=norm_mix_post, norm_ffn_pre=norm_ffn_pre, norm_ffn_post=norm_ffn_post, loss_target=loss_target, m_meta_tokens=m_meta_tokens, m_w_in=m_w_in, m_w_ret_out=m_w_ret_out, m_w_sb_out=m_w_sb_out, m_w_out=m_w_out, m_w_ffn_in=m_w_ffn_in, m_w_ffn_out=m_w_ffn_out, m_norm_mix_pre=m_norm_mix_pre, m_norm_mix_post=m_norm_mix_post, m_norm_ffn_pre=m_norm_ffn_pre, m_norm_ffn_post=m_norm_ffn_post, v_meta_tokens=v_meta_tokens, v_w_in=v_w_in, v_w_ret_out=v_w_ret_out, v_w_sb_out=v_w_sb_out, v_w_out=v_w_out, v_w_ffn_in=v_w_ffn_in, v_w_ffn_out=v_w_ffn_out, v_norm_mix_pre=v_norm_mix_pre, v_norm_mix_post=v_norm_mix_post, v_norm_ffn_pre=v_norm_ffn_pre, v_norm_ffn_post=v_norm_ffn_post)
    weights = {n: given[n] for n in TWIN_WEIGHTS}
    shared = {n: given[n] for n in SHARED_INPUTS}
    per_example = {n: given[n] for n in ['x']}
    grad_fn = _jax.value_and_grad(_loss, argnums=(0, 1))

    def one_microbatch(ex, loss_target):
        ex = dict(ex)
        diff = ex.pop(TWIN_DIFF_INPUT)
        return grad_fn(weights, diff, {**shared, **ex}, loss_target)

    if N_MICROBATCH == 1:
        loss, (grad_w, grad_x) = one_microbatch(per_example, given["loss_target"])
    else:
        def body(carry, xs):
            loss_sum, grad_sum = carry
            l_k, (gw_k, gx_k) = one_microbatch(xs[0], xs[1])
            with _jax.named_scope("update"):
                return (loss_sum + l_k, _jax.tree.map(_jnp.add, grad_sum, gw_k)), gx_k

        init = (_jnp.zeros((), _jnp.float32), _jax.tree.map(_jnp.zeros_like, weights))
        (loss, grad_w), grad_x = _jax.lax.scan(body, init, (per_example, given["loss_target"]))
    with _jax.named_scope("update"):
        delta_w, new_m, new_v = {}, {}, {}
        for n in TWIN_WEIGHTS:
            delta_w[n], new_m[n], new_v[n] = _adamw(weights[n], grad_w[n], given["m_" + n], given["v_" + n])
    return (loss, grad_x, *[grad_w[n] for n in TWIN_WEIGHTS], *[delta_w[n] for n in TWIN_WEIGHTS],
            *[new_m[n] for n in TWIN_WEIGHTS], *[new_v[n] for n in TWIN_WEIGHTS])
```

```python
import jax
import jax.numpy as jnp
from jax import lax
from jax.experimental import pallas as pl
from jax.experimental.pallas import tpu as pltpu

F32 = jnp.float32
BF16 = jnp.bfloat16

D = 1024
N_META = 16
CHUNK = 128
FRONT = 256
META0 = FRONT - N_META
RH, RDK, RDV = 4, 256, 512
SB_DH = 64
DFF = 2816
NDEV = 8
ROPE_BASE = 10000.0
NORM_EPS = 1e-6
GN_EPS = 1e-5
C_RQ, C_RK, C_RV, C_RG, C_SQ, C_SK, C_SV, C_GA, C_GB = 0, 1024, 2048, 4096, 6144, 7168, 8192, 9216, 10240
PROJ = 11264
PACK_ROWS = (("w_in_t", PROJ // NDEV), ("w_ffn_in_t", 2 * DFF // NDEV), ("w_ret_out", RH * RDV // NDEV),
             ("w_sb_out", D // NDEV), ("w_out", D // NDEV), ("w_ffn_out", DFF // NDEV))
PACK = sum(r for _, r in PACK_ROWS)
GAIN_ROWS = 8

ADAM_LR = 0.001
ADAM_B1 = 0.9
ADAM_B2 = 0.999
ADAM_EPS = 1e-08
ADAM_WD = 0.01
ADAM_STEP = 10

VMEM_LIMIT = 56 * 1024 * 1024
SB_T = 256

NT = (((1,), (1,)), ((), ()))
TN = (((0,), (0,)), ((), ()))


def _dot(a, b):
    return jnp.dot(a, b, preferred_element_type=F32)


def _dot_nt(a, b):
    return lax.dot_general(a, b, NT, preferred_element_type=F32)


def _dot_tn(a, b):
    return lax.dot_general(a, b, TN, preferred_element_type=F32)


def _tile(n, candidates):
    for t in candidates:
        if n % t == 0:
            return t
    return n


def _params(*sem):
    return pltpu.CompilerParams(dimension_semantics=sem, vmem_limit_bytes=VMEM_LIMIT)


def _rms_hat(x):
    r = lax.rsqrt(jnp.mean(x * x, axis=-1, keepdims=True) + NORM_EPS)
    return x * r, r


def _rms_bwd(xhat, r, g, dy):
    u = dy * g
    return r * (u - xhat * jnp.mean(u * xhat, axis=-1, keepdims=True))


def _gn(y):
    mu = jnp.mean(y, axis=-1, keepdims=True)
    yc = y - mu
    rs = lax.rsqrt(jnp.mean(yc * yc, axis=-1, keepdims=True) + GN_EPS)
    return yc * rs, rs


def _gn_bwd(yh, rs, d):
    return rs * (d - jnp.mean(d, axis=-1, keepdims=True) - yh * jnp.mean(d * yh, axis=-1, keepdims=True))


def _sigmoid(x):
    return 1.0 / (1.0 + jnp.exp(-x))


def mm_nt(a, b, out_dtype, name):
    m, k = a.shape
    n = b.shape[0]
    tm = _tile(m, (768, 512, 256))
    tn = _tile(n, (1024, 512, 256))

    def body(a_ref, b_ref, o_ref):
        o_ref[...] = _dot_nt(a_ref[...], b_ref[...]).astype(o_ref.dtype)

    return pl.pallas_call(
        body, name=name, grid=(m // tm, n // tn),
        in_specs=[pl.BlockSpec((tm, k), lambda i, j: (i, 0)), pl.BlockSpec((tn, k), lambda i, j: (j, 0))],
        out_specs=pl.BlockSpec((tm, tn), lambda i, j: (i, j)),
        out_shape=jax.ShapeDtypeStruct((m, n), out_dtype),
        compiler_params=_params("parallel", "arbitrary"),
    )(a, b)


def mm_nn(a, b, out_dtype, name):
    m, k = a.shape
    n = b.shape[1]
    tm = _tile(m, (768, 512, 256))
    tk = _tile(k, (1024, 512, 256))
    nk = k // tk

    def body(a_ref, b_ref, o_ref, acc_ref):
        kk = pl.program_id(1)

        @pl.when(kk == 0)
        def _():
            acc_ref[...] = jnp.zeros_like(acc_ref)

        acc_ref[...] += _dot(a_ref[...], b_ref[...])

        @pl.when(kk == nk - 1)
        def _():
            o_ref[...] = acc_ref[...].astype(o_ref.dtype)

    return pl.pallas_call(
        body, name=name, grid=(m // tm, nk),
        in_specs=[pl.BlockSpec((tm, tk), lambda i, kk: (i, kk)), pl.BlockSpec((tk, n), lambda i, kk: (kk, 0))],
        out_specs=pl.BlockSpec((tm, n), lambda i, kk: (i, 0)),
        out_shape=jax.ShapeDtypeStruct((m, n), out_dtype),
        scratch_shapes=[pltpu.VMEM((tm, n), F32)],
        compiler_params=_params("parallel", "arbitrary"),
    )(a, b)


def mm_tn(a, b, out_dtype, name):
    m, ka = a.shape
    n = b.shape[1]
    ta = _tile(ka, (1024, 512, 256))
    tl = _tile(m, (768, 512, 256))
    nl = m // tl

    def body(a_ref, b_ref, o_ref, acc_ref):
        ll = pl.program_id(1)

        @pl.when(ll == 0)
        def _():
            acc_ref[...] = jnp.zeros_like(acc_ref)

        acc_ref[...] += _dot_tn(a_ref[...], b_ref[...])

        @pl.when(ll == nl - 1)
        def _():
            o_ref[...] = acc_ref[...].astype(o_ref.dtype)

    return pl.pallas_call(
        body, name=name, grid=(ka // ta, nl),
        in_specs=[pl.BlockSpec((tl, ta), lambda i, ll: (ll, i)), pl.BlockSpec((tl, n), lambda i, ll: (ll, 0))],
        out_specs=pl.BlockSpec((ta, n), lambda i, ll: (i, 0)),
        out_shape=jax.ShapeDtypeStruct((ka, n), out_dtype),
        scratch_shapes=[pltpu.VMEM((ta, n), F32)],
        compiler_params=_params("parallel", "arbitrary"),
    )(a, b)


TM = 256


def _rb(arr, width=None, col_block=0):
    w = arr.shape[1] if width is None else width
    return pl.BlockSpec((TM, w), lambda i: (i, col_block))


def _whole(arr):
    return pl.BlockSpec(arr.shape, lambda i: (0,) * arr.ndim)


def _rows_call(body, name, lq, ins, in_specs, out_shapes, out_specs):
    return pl.pallas_call(
        body, name=name, grid=(lq // TM,), in_specs=in_specs, out_specs=out_specs, out_shape=out_shapes,
        compiler_params=_params("arbitrary"),
    )(*ins)


def rms_fwd(h, g, name):
    lq = h.shape[0]

    def body(h_ref, g_ref, o_ref):
        xhat, _ = _rms_hat(h_ref[...])
        o_ref[...] = (xhat * g_ref[...]).astype(BF16)

    return _rows_call(body, name, lq, (h, g), [_rb(h), _whole(g)],
                      jax.ShapeDtypeStruct((lq, D), BF16), _rb(h))


def rotary_fwd(proj, cos, sin):
    lq = proj.shape[0]
    half = RDK // 2

    def body(p_ref, c_ref, s_ref, q_ref, k_ref):
        c, s = c_ref[...], s_ref[...]
        for col, o_ref, scale in ((C_RQ, q_ref, RDK ** -0.5), (C_RK, k_ref, 1.0)):
            for h in range(RH):
                x1 = p_ref[:, col + h * RDK: col + h * RDK + half].astype(F32)
                x2 = p_ref[:, col + h * RDK + half: col + (h + 1) * RDK].astype(F32)
                o_ref[:, h * RDK: h * RDK + half] = ((x1 * c - x2 * s) * scale).astype(BF16)
                o_ref[:, h * RDK + half: (h + 1) * RDK] = ((x1 * s + x2 * c) * scale).astype(BF16)

    out = jax.ShapeDtypeStruct((lq, RH * RDK), BF16)
    return _rows_call(body, "rotary_fwd", lq, (proj, cos, sin),
                      [_rb(proj, 2 * RH * RDK, 0), _rb(cos), _rb(sin)],
                      (out, out), (pl.BlockSpec((TM, RH * RDK), lambda i: (i, 0)),) * 2)


def rotary_bwd(dq, dk, cos, sin):
    lq = dq.shape[0]
    half = RDK // 2

    def body(dq_ref, dk_ref, c_ref, s_ref, o_ref):
        c, s = c_ref[...], s_ref[...]
        for col, d_ref, scale in ((C_RQ, dq_ref, RDK ** -0.5), (C_RK, dk_ref, 1.0)):
            for h in range(RH):
                d1 = d_ref[:, h * RDK: h * RDK + half]
                d2 = d_ref[:, h * RDK + half: (h + 1) * RDK]
                o_ref[:, col + h * RDK: col + h * RDK + half] = ((d1 * c + d2 * s) * scale).astype(BF16)
                o_ref[:, col + h * RDK + half: col + (h + 1) * RDK] = ((d2 * c - d1 * s) * scale).astype(BF16)

    return _rows_call(body, "rotary_bwd", lq, (dq, dk, cos, sin), [_rb(dq), _rb(dk), _rb(cos), _rb(sin)],
                      jax.ShapeDtypeStruct((lq, 2 * RH * RDK), BF16),
                      pl.BlockSpec((TM, 2 * RH * RDK), lambda i: (i, 0)))


def ret_gate_fwd(proj, o_ret):
    lq = proj.shape[0]

    def body(p_ref, y_ref, o_ref):
        for h in range(RH):
            sl = slice(h * RDV, (h + 1) * RDV)
            yh, _ = _gn(y_ref[:, sl])
            rg = p_ref[:, sl].astype(F32)
            o_ref[:, sl] = (rg * _sigmoid(rg) * yh).astype(BF16)

    return _rows_call(body, "ret_gate_fwd", lq, (proj, o_ret),
                      [_rb(proj, RH * RDV, C_RG // (RH * RDV)), _rb(o_ret)],
                      jax.ShapeDtypeStruct((lq, RH * RDV), BF16), _rb(o_ret))


def ret_gate_bwd(proj, o_ret, d_gr):
    lq = proj.shape[0]

    def body(p_ref, y_ref, d_ref, drg_ref, dy_ref):
        for h in range(RH):
            sl = slice(h * RDV, (h + 1) * RDV)
            yh, rs = _gn(y_ref[:, sl])
            rg = p_ref[:, sl].astype(F32)
            sg = _sigmoid(rg)
            d = d_ref[:, sl].astype(F32)
            drg_ref[:, sl] = (d * yh * sg * (1.0 + rg * (1.0 - sg))).astype(BF16)
            dy_ref[:, sl] = _gn_bwd(yh, rs, d * rg * sg).astype(BF16)

    out = jax.ShapeDtypeStruct((lq, RH * RDV), BF16)
    return _rows_call(body, "ret_gate_bwd", lq, (proj, o_ret, d_gr),
                      [_rb(proj, RH * RDV, C_RG // (RH * RDV)), _rb(o_ret), _rb(d_gr)],
                      (out, out), (_rb(o_ret), _rb(o_ret)))


def merge_fwd(proj, y_ret, y_sb):
    lq = proj.shape[0]

    def body(ga_ref, gb_ref, yr_ref, ys_ref, o_ref):
        o_ref[...] = (_sigmoid(ga_ref[...].astype(F32)) * yr_ref[...]
                      + _sigmoid(gb_ref[...].astype(F32)) * ys_ref[...]).astype(BF16)

    return _rows_call(body, "merge_fwd", lq, (proj, proj, y_ret, y_sb),
                      [_rb(proj, D, C_GA // D), _rb(proj, D, C_GB // D), _rb(y_ret), _rb(y_sb)],
                      jax.ShapeDtypeStruct((lq, D), BF16), _rb(y_ret))


def merge_bwd(proj, y_ret, y_sb, d_merged):
    lq = proj.shape[0]

    def body(ga_ref, gb_ref, yr_ref, ys_ref, d_ref, dyr_ref, dys_ref, dg_ref):
        d = d_ref[...].astype(F32)
        sa = _sigmoid(ga_ref[...].astype(F32))
        sb = _sigmoid(gb_ref[...].astype(F32))
        dyr_ref[...] = (d * sa).astype(BF16)
        dys_ref[...] = (d * sb).astype(BF16)
        dg_ref[:, :D] = (d * yr_ref[...] * sa * (1.0 - sa)).astype(BF16)
        dg_ref[:, D:] = (d * ys_ref[...] * sb * (1.0 - sb)).astype(BF16)

    o1 = jax.ShapeDtypeStruct((lq, D), BF16)
    return _rows_call(body, "merge_bwd", lq, (proj, proj, y_ret, y_sb, d_merged),
                      [_rb(proj, D, C_GA // D), _rb(proj, D, C_GB // D), _rb(y_ret), _rb(y_sb), _rb(d_merged)],
                      (o1, o1, jax.ShapeDtypeStruct((lq, 2 * D), BF16)),
                      (_rb(y_ret), _rb(y_ret), pl.BlockSpec((TM, 2 * D), lambda i: (i, 0))))


def post_mix_fwd(hp, mix, g_post, g_pre):
    lq = hp.shape[0]

    def body(h_ref, m_ref, g2_ref, g3_ref, h1_ref, hn_ref):
        mhat, _ = _rms_hat(m_ref[...])
        h1 = h_ref[...] + mhat * g2_ref[...]
        h1_ref[...] = h1
        hhat, _ = _rms_hat(h1)
        hn_ref[...] = (hhat * g3_ref[...]).astype(BF16)

    return _rows_call(body, "post_mix_fwd", lq, (hp, mix, g_post, g_pre),
                      [_rb(hp), _rb(mix), _whole(g_post), _whole(g_pre)],
                      (jax.ShapeDtypeStruct((lq, D), F32), jax.ShapeDtypeStruct((lq, D), BF16)),
                      (_rb(hp), _rb(hp)))


def swiglu_fwd(ab):
    lq = ab.shape[0]

    def body(a_ref, b_ref, o_ref):
        a = a_ref[...].astype(F32)
        o_ref[...] = (a * _sigmoid(a) * b_ref[...].astype(F32)).astype(BF16)

    return _rows_call(body, "swiglu_fwd", lq, (ab, ab), [_rb(ab, DFF, 0), _rb(ab, DFF, 1)],
                      jax.ShapeDtypeStruct((lq, DFF), BF16), pl.BlockSpec((TM, DFF), lambda i: (i, 0)))


def swiglu_bwd(ab, d_act):
    lq = ab.shape[0]

    def body(a_ref, b_ref, d_ref, o_ref):
        a = a_ref[...].astype(F32)
        b = b_ref[...].astype(F32)
        d = d_ref[...].astype(F32)
        sg = _sigmoid(a)
        o_ref[:, :DFF] = (d * b * sg * (1.0 + a * (1.0 - sg))).astype(BF16)
        o_ref[:, DFF:] = (d * a * sg).astype(BF16)

    return _rows_call(body, "swiglu_bwd", lq, (ab, ab, d_act), [_rb(ab, DFF, 0), _rb(ab, DFF, 1), _rb(d_act)],
                      jax.ShapeDtypeStruct((lq, 2 * DFF), BF16), pl.BlockSpec((TM, 2 * DFF), lambda i: (i, 0)))


def loss_head(h1, ff, g_post, target):
    lq = h1.shape[0]
    front_blocks = FRONT // TM

    def body(h_ref, f_ref, g_ref, t_ref, loss_ref, dh_ref, df_ref, dg_ref):
        i = pl.program_id(0)

        @pl.when(i == 0)
        def _():
            loss_ref[...] = jnp.zeros_like(loss_ref)
            dg_ref[...] = jnp.zeros_like(dg_ref)

        g = g_ref[...]
        fhat, r = _rms_hat(f_ref[...])
        is_x = (i >= front_blocks).astype(F32)
        diff = (h_ref[...] + fhat * g - t_ref[...]) * is_x
        loss_ref[...] += 0.5 * jnp.sum(diff * diff) / D
        dy = diff / D
        dh_ref[...] = dy
        df_ref[...] = _rms_bwd(fhat, r, g, dy).astype(BF16)
        dg_ref[...] += jnp.sum(dy * fhat, axis=0, keepdims=True)

    return _rows_call(
        body, "loss_head", lq, (h1, ff, g_post, target),
        [_rb(h1), _rb(ff), _whole(g_post),
         pl.BlockSpec((TM, D), lambda i: (jnp.maximum(i - front_blocks, 0), 0))],
        (jax.ShapeDtypeStruct((8, 128), F32), jax.ShapeDtypeStruct((lq, D), F32),
         jax.ShapeDtypeStruct((lq, D), BF16), jax.ShapeDtypeStruct((GAIN_ROWS, D), F32)),
        (pl.BlockSpec((8, 128), lambda i: (0, 0)), _rb(h1), _rb(h1), pl.BlockSpec((GAIN_ROWS, D), lambda i: (0, 0))))


def post_mix_bwd(h1, d_hn2, g_pre, d_h2, mix, g_post):
    lq = h1.shape[0]

    def body(h_ref, dn_ref, g3_ref, dh2_ref, m_ref, g2_ref, dh1_ref, dm_ref, dg3_ref, dg2_ref):
        i = pl.program_id(0)

        @pl.when(i == 0)
        def _():
            dg3_ref[...] = jnp.zeros_like(dg3_ref)
            dg2_ref[...] = jnp.zeros_like(dg2_ref)

        hhat, r = _rms_hat(h_ref[...])
        dn = dn_ref[...]
        d_h1 = dh2_ref[...] + _rms_bwd(hhat, r, g3_ref[...], dn)
        dh1_ref[...] = d_h1
        dg3_ref[...] += jnp.sum(dn * hhat, axis=0, keepdims=True)
        mhat, rm = _rms_hat(m_ref[...])
        dm_ref[...] = _rms_bwd(mhat, rm, g2_ref[...], d_h1).astype(BF16)
        dg2_ref[...] += jnp.sum(d_h1 * mhat, axis=0, keepdims=True)

    vec = jax.ShapeDtypeStruct((GAIN_ROWS, D), F32)
    vspec = pl.BlockSpec((GAIN_ROWS, D), lambda i: (0, 0))
    return _rows_call(body, "post_mix_bwd", lq, (h1, d_hn2, g_pre, d_h2, mix, g_post),
                      [_rb(h1), _rb(d_hn2), _whole(g_pre), _rb(d_h2), _rb(mix), _whole(g_post)],
                      (jax.ShapeDtypeStruct((lq, D), F32), jax.ShapeDtypeStruct((lq, D), BF16), vec, vec),
                      (_rb(h1), _rb(h1), vspec, vspec))


def pre_mix_bwd(hp, d_hn1, g_pre, d_h1):
    lq = hp.shape[0]

    def body(h_ref, dn_ref, g_ref, dh1_ref, dhp_ref, dg_ref):
        i = pl.program_id(0)

        @pl.when(i == 0)
        def _():
            dg_ref[...] = jnp.zeros_like(dg_ref)

        hhat, r = _rms_hat(h_ref[...])
        dn = dn_ref[...]
        dhp_ref[...] = dh1_ref[...] + _rms_bwd(hhat, r, g_ref[...], dn)
        dg_ref[...] += jnp.sum(dn * hhat, axis=0, keepdims=True)

    return _rows_call(body, "pre_mix_bwd", lq, (hp, d_hn1, g_pre, d_h1),
                      [_rb(hp), _rb(d_hn1), _whole(g_pre), _rb(d_h1)],
                      (jax.ShapeDtypeStruct((lq, D), F32), jax.ShapeDtypeStruct((GAIN_ROWS, D), F32)),
                      (_rb(hp), pl.BlockSpec((GAIN_ROWS, D), lambda i: (0, 0))))


def _retention_tables():
    log_g = jnp.log1p(-(2.0 ** (-5.0 - jnp.arange(RH, dtype=F32))))
    idx = jnp.arange(CHUNK, dtype=F32)
    diff = idx[:, None] - idx[None, :]
    decay = jnp.where(diff >= 0, jnp.exp(log_g[:, None, None] * jnp.maximum(diff, 0.0)), 0.0)
    zeta = jnp.exp(log_g[:, None] * (CHUNK - 1.0 - idx))
    xi = jnp.exp(log_g[:, None] * (idx + 1.0))
    g_chunk = jnp.broadcast_to(jnp.exp(log_g * CHUNK)[:, None], (RH, CHUNK))
    coef = jnp.stack([xi, zeta, g_chunk] + [jnp.zeros_like(xi)] * 125, axis=-1)
    return decay, coef


def retention_fwd(qr, kr, proj, decay, coef):
    lq = qr.shape[0]
    n = lq // CHUNK

    def body(q_ref, k_ref, v_ref, dec_ref, cf_ref, o_ref, st_ref, state):
        @pl.when(pl.program_id(1) == 0)
        def _():
            state[...] = jnp.zeros_like(state)

        q, k, v = q_ref[...], k_ref[...], v_ref[...]
        xi, zeta, gch = cf_ref[:, 0:1], cf_ref[:, 1:2], cf_ref[0:1, 2:3]
        st = state[...]
        stb = st.astype(BF16)
        st_ref[...] = stb
        s = _dot_nt(q, k) * dec_ref[...]
        o_ref[...] = _dot(s.astype(BF16), v) + _dot(q, stb) * xi
        vz = (v.astype(F32) * zeta).astype(BF16)
        state[...] = gch * st + _dot_tn(k, vz)

    return pl.pallas_call(
        body, name="retention_fwd", grid=(RH, n),
        in_specs=[pl.BlockSpec((CHUNK, RDK), lambda h, c: (c, h)), pl.BlockSpec((CHUNK, RDK), lambda h, c: (c, h)),
                  pl.BlockSpec((CHUNK, RDV), lambda h, c: (c, C_RV // RDV + h)),
                  pl.BlockSpec((None, CHUNK, CHUNK), lambda h, c: (h, 0, 0)),
                  pl.BlockSpec((None, CHUNK, 128), lambda h, c: (h, 0, 0))],
        out_specs=(pl.BlockSpec((CHUNK, RDV), lambda h, c: (c, h)),
                   pl.BlockSpec((None, None, RDK, RDV), lambda h, c: (c, h, 0, 0))),
        out_shape=(jax.ShapeDtypeStruct((lq, RH * RDV), F32), jax.ShapeDtypeStruct((n, RH, RDK, RDV), BF16)),
        scratch_shapes=[pltpu.VMEM((RDK, RDV), F32)],
        compiler_params=_params("parallel", "arbitrary"),
    )(qr, kr, proj, decay, coef)


def retention_bwd(qr, kr, proj, d_o, states, decay, coef):
    lq = qr.shape[0]
    n = lq // CHUNK

    def body(q_ref, k_ref, v_ref, do_ref, st_ref, dec_ref, cf_ref, dq_ref, dk_ref, dv_ref, dstate):
        @pl.when(pl.program_id(1) == 0)
        def _():
            dstate[...] = jnp.zeros_like(dstate)

        q, k, v, dob = q_ref[...], k_ref[...], v_ref[...], do_ref[...]
        xi, zeta, gch = cf_ref[:, 0:1], cf_ref[:, 1:2], cf_ref[0:1, 2:3]
        dec = dec_ref[...]
        dsn = dstate[...]
        dsnb = dsn.astype(BF16)
        dox = (dob.astype(F32) * xi).astype(BF16)
        sb = (_dot_nt(q, k) * dec).astype(BF16)
        dsb = (_dot_nt(dob, v) * dec).astype(BF16)
        vz = (v.astype(F32) * zeta).astype(BF16)
        dq_ref[...] = _dot(dsb, k) + _dot_nt(dox, st_ref[...])
        dk_ref[...] = _dot_tn(dsb, q) + _dot_nt(vz, dsnb)
        dv_ref[...] = (_dot_tn(sb, dob) + _dot(k, dsnb) * zeta).astype(BF16)
        dstate[...] = gch * dsn + _dot_tn(q, dox)

    rev = lambda c: n - 1 - c
    return pl.pallas_call(
        body, name="retention_bwd", grid=(RH, n),
        in_specs=[pl.BlockSpec((CHUNK, RDK), lambda h, c: (rev(c), h)),
                  pl.BlockSpec((CHUNK, RDK), lambda h, c: (rev(c), h)),
                  pl.BlockSpec((CHUNK, RDV), lambda h, c: (rev(c), C_RV // RDV + h)),
                  pl.BlockSpec((CHUNK, RDV), lambda h, c: (rev(c), h)),
                  pl.BlockSpec((None, None, RDK, RDV), lambda h, c: (rev(c), h, 0, 0)),
                  pl.BlockSpec((None, CHUNK, CHUNK), lambda h, c: (h, 0, 0)),
                  pl.BlockSpec((None, CHUNK, 128), lambda h, c: (h, 0, 0))],
        out_specs=(pl.BlockSpec((CHUNK, RDK), lambda h, c: (rev(c), h)),
                   pl.BlockSpec((CHUNK, RDK), lambda h, c: (rev(c), h)),
                   pl.BlockSpec((CHUNK, RDV), lambda h, c: (rev(c), h))),
        out_shape=(jax.ShapeDtypeStruct((lq, RH * RDK), F32), jax.ShapeDtypeStruct((lq, RH * RDK), F32),
                   jax.ShapeDtypeStruct((lq, RH * RDV), BF16)),
        scratch_shapes=[pltpu.VMEM((RDK, RDV), F32)],
        compiler_params=_params("parallel", "arbitrary"),
    )(qr, kr, proj, d_o, states, decay, coef)


def _sb_masks_and_u():
    lane = lax.broadcasted_iota(jnp.int32, (1, 2 * SB_DH), 1)
    lo = lane < SB_DH
    row = lax.broadcasted_iota(jnp.int32, (SB_T, SB_T), 0)
    col = lax.broadcasted_iota(jnp.int32, (SB_T, SB_T), 1)
    u = (row > col).astype(BF16)
    return lo, u, row, col


def _sb_rows(j):
    start = j * SB_T
    return pl.ds(start if isinstance(j, int) else pl.multiple_of(start, SB_T), SB_T)


def _sb_logs(z):
    ls = jnp.minimum(z, 0.0) - jnp.log(1.0 + jnp.exp(-jnp.abs(z)))
    return ls, ls - z


def sb_fwd(proj):
    lq = proj.shape[0]
    t = SB_T

    def body(q_ref, k_ref, v_ref, o_ref, of_ref, acc_ref):
        i = pl.program_id(1)
        lo, u, row, col = _sb_masks_and_u()
        qs = (q_ref[...].astype(F32) * SB_DH ** -0.5).astype(BF16)
        zero = jnp.zeros_like(qs)
        qh = (jnp.where(lo, qs, zero), jnp.where(lo, zero, qs))
        acc_ref[...] = jnp.zeros_like(acc_ref)

        def block(j, run, masked):
            rows = _sb_rows(j)
            ks, vs = k_ref[rows, :], v_ref[rows, :]
            if masked:
                valid = (col + j * t < row + i * t) & (col + j * t >= META0)
            out = []
            for hh in range(2):
                ls, ln = _sb_logs(_dot_nt(qh[hh], ks))
                if masked:
                    ln = jnp.where(valid, ln, 0.0)
                a = jnp.exp(ls + _dot(ln.astype(BF16), u) + run[hh])
                if masked:
                    a = jnp.where(valid, a, 0.0)
                acc_ref[hh] += _dot(a.astype(BF16), vs)
                out.append(run[hh] + jnp.sum(ln, axis=1, keepdims=True))
            return tuple(out)

        zeros = jnp.zeros((t, 1), F32)
        run = block(i, (zeros, zeros), True)
        run = lax.fori_loop(0, jnp.maximum(i - 1, 0), lambda jj, r: block(i - 1 - jj, r, False), run)

        @pl.when(i > 0)
        def _():
            block(0, run, True)

        o = jnp.where(lo, acc_ref[0], acc_ref[1])
        o_ref[...] = o.astype(BF16)
        of_ref[...] = o

    blk = pl.BlockSpec((t, 128), lambda p, i: (i, p))
    return pl.pallas_call(
        body, name="sb_fwd", grid=(D // 128, lq // t),
        in_specs=[pl.BlockSpec((t, 128), lambda p, i: (i, C_SQ // 128 + p)),
                  pl.BlockSpec((lq, 128), lambda p, i: (0, C_SK // 128 + p)),
                  pl.BlockSpec((lq, 128), lambda p, i: (0, C_SV // 128 + p))],
        out_specs=(blk, blk),
        out_shape=(jax.ShapeDtypeStruct((lq, D), BF16), jax.ShapeDtypeStruct((lq, D), F32)),
        scratch_shapes=[pltpu.VMEM((2, t, 128), F32)],
        compiler_params=_params("parallel", "arbitrary"),
    )(proj, proj, proj)


def sb_bwd(proj, o, d_o):
    lq = proj.shape[0]
    t = SB_T

    def body(q_ref, k_ref, v_ref, o_ref, do_ref, dq_ref, dk_ref, dv_ref, acc_ref):
        i = pl.program_id(1)

        @pl.when(i == 0)
        def _():
            dk_ref[...] = jnp.zeros_like(dk_ref)
            dv_ref[...] = jnp.zeros_like(dv_ref)

        lo, u, row, col = _sb_masks_and_u()
        qs = (q_ref[...].astype(F32) * SB_DH ** -0.5).astype(BF16)
        do = do_ref[...]
        zero = jnp.zeros_like(qs)
        qh = (jnp.where(lo, qs, zero), jnp.where(lo, zero, qs))
        doh = (jnp.where(lo, do, zero), jnp.where(lo, zero, do))
        prod = o_ref[...] * do.astype(F32)
        dsum = (jnp.sum(jnp.where(lo, prod, 0.0), axis=1, keepdims=True),
                jnp.sum(jnp.where(lo, 0.0, prod), axis=1, keepdims=True))
        acc_ref[...] = jnp.zeros_like(acc_ref)

        def block(j, run, masked):
            rows = _sb_rows(j)
            ks, vs = k_ref[rows, :], v_ref[rows, :]
            if masked:
                valid = (col + j * t < row + i * t) & (col + j * t >= META0)
            out = []
            for hh in range(2):
                run_ln, run_e = run[2 * hh], run[2 * hh + 1]
                ls, ln = _sb_logs(_dot_nt(qh[hh], ks))
                if masked:
                    ln = jnp.where(valid, ln, 0.0)
                a = jnp.exp(ls + _dot(ln.astype(BF16), u) + run_ln)
                if masked:
                    a = jnp.where(valid, a, 0.0)
                ab = a.astype(BF16)
                e = ab.astype(F32) * _dot_nt(doh[hh], vs)
                e_hi = e.astype(BF16)
                e_lo = (e - e_hi.astype(F32)).astype(BF16)
                dz = e - jnp.exp(ls) * (dsum[hh] - run_e - (_dot(e_hi, u) + _dot(e_lo, u)))
                if masked:
                    dz = jnp.where(valid, dz, 0.0)
                dzb = dz.astype(BF16)
                acc_ref[hh] += _dot(dzb, ks)
                dk_ref[rows, :] += _dot_tn(dzb, qh[hh])
                dv_ref[rows, :] += _dot_tn(ab, doh[hh])
                out += [run_ln + jnp.sum(ln, axis=1, keepdims=True), run_e + jnp.sum(e, axis=1, keepdims=True)]
            return tuple(out)

        zeros = jnp.zeros((t, 1), F32)
        run = block(i, (zeros,) * 4, True)
        run = lax.fori_loop(0, jnp.maximum(i - 1, 0), lambda jj, r: block(i - 1 - jj, r, False), run)

        @pl.when(i > 0)
        def _():
            block(0, run, True)

        dq_ref[...] = (jnp.where(lo, acc_ref[0], acc_ref[1]) * SB_DH ** -0.5).astype(BF16)

    blk = pl.BlockSpec((t, 128), lambda p, i: (i, p))
    col_blk = pl.BlockSpec((lq, 128), lambda p, i: (0, p))
    return pl.pallas_call(
        body, name="sb_bwd", grid=(D // 128, lq // t),
        in_specs=[pl.BlockSpec((t, 128), lambda p, i: (i, C_SQ // 128 + p)),
                  pl.BlockSpec((lq, 128), lambda p, i: (0, C_SK // 128 + p)),
                  pl.BlockSpec((lq, 128), lambda p, i: (0, C_SV // 128 + p)), blk, blk],
        out_specs=(blk, col_blk, col_blk),
        out_shape=(jax.ShapeDtypeStruct((lq, D), BF16), jax.ShapeDtypeStruct((lq, D), F32),
                   jax.ShapeDtypeStruct((lq, D), F32)),
        scratch_shapes=[pltpu.VMEM((2, t, 128), F32)],
        compiler_params=_params("parallel", "arbitrary"),
    )(proj, proj, proj, o, d_o)


def _device_index(px, py, pc):
    return 4 * px + 2 * py + pc


def all_gather(blocks):
    nb = len(blocks)

    def body(*refs):
        x_refs, out_refs = refs[:nb], refs[nb:2 * nb]
        send_sems, recv_sems, local_sems = refs[2 * nb:]
        x, y, c = lax.axis_index("x"), lax.axis_index("y"), lax.axis_index("c")
        me, sibling = (x, y, c), (x, y, 1 - c)
        chips = [(1 - x, y), (x, 1 - y), (1 - x, 1 - y)]

        def copy(b, k, block, to, src=None):
            slot = out_refs[b].at[_device_index(*block)]
            return pltpu.make_async_remote_copy(
                src_ref=slot if src is None else src, dst_ref=slot,
                send_sem=send_sems.at[b, k], recv_sem=recv_sems.at[b, k],
                device_id=to, device_id_type=pl.DeviceIdType.MESH)

        mine = [pltpu.make_async_copy(x_refs[b], out_refs[b].at[_device_index(*me)], local_sems.at[b])
                for b in range(nb)]
        for cp in mine:
            cp.start()
        first = []
        for b in range(nb):
            first.append(copy(b, 0, me, sibling, src=x_refs[b]))
            first += [copy(b, 1 + j, me, (*chip, c), src=x_refs[b]) for j, chip in enumerate(chips)]
        for cp in first:
            cp.start()
        passed = []
        for j, chip in enumerate(chips):
            for b in range(nb):
                copy(b, 1 + j, (*chip, c), me).wait_recv()
                cp = copy(b, 4 + j, (*chip, c), sibling)
                cp.start()
                passed.append(cp)
        for b in range(nb):
            copy(b, 0, sibling, me).wait_recv()
            for j, chip in enumerate(chips):
                copy(b, 4 + j, (*chip, 1 - c), me).wait_recv()
        for cp in first + passed:
            cp.wait_send()
        for cp in mine:
            cp.wait()

    any_spec = pl.BlockSpec(memory_space=pl.ANY)
    return pl.pallas_call(
        body, name="all_gather",
        in_specs=[any_spec] * nb, out_specs=[any_spec] * nb,
        out_shape=[jax.ShapeDtypeStruct((NDEV,) + b.shape, b.dtype) for b in blocks],
        scratch_shapes=[pltpu.SemaphoreType.DMA((nb, 7)), pltpu.SemaphoreType.DMA((nb, 7)),
                        pltpu.SemaphoreType.DMA((nb,))],
    )(*blocks)


def exchange(parts, small):
    def body(g_ref, s_ref, land_ref, sland_ref, send_sems, recv_sems, local_sems):
        x, y, c = lax.axis_index("x"), lax.axis_index("y"), lax.axis_index("c")
        me = _device_index(x, y, c)
        own = [pltpu.make_async_copy(g_ref.at[me], land_ref.at[me], local_sems.at[0]),
               pltpu.make_async_copy(s_ref, sland_ref.at[me], local_sems.at[1])]
        for cp in own:
            cp.start()
        sent = []
        for k in range(1, NDEV):
            px = 1 - x if k & 4 else x
            py = 1 - y if k & 2 else y
            pc = 1 - c if k & 1 else c
            peer = _device_index(px, py, pc)
            sent.append((peer, pltpu.make_async_remote_copy(
                src_ref=g_ref.at[peer], dst_ref=land_ref.at[me],
                send_sem=send_sems.at[0, k - 1], recv_sem=recv_sems.at[0, k - 1],
                device_id=(px, py, pc), device_id_type=pl.DeviceIdType.MESH)))
            sent.append((peer, pltpu.make_async_remote_copy(
                src_ref=s_ref, dst_ref=sland_ref.at[me],
                send_sem=send_sems.at[1, k - 1], recv_sem=recv_sems.at[1, k - 1],
                device_id=(px, py, pc), device_id_type=pl.DeviceIdType.MESH)))
        for _, cp in sent:
            cp.start()
        for n, (peer, cp) in enumerate(sent):
            k, which = n // 2, n % 2
            if which == 0:
                pltpu.make_async_remote_copy(
                    src_ref=g_ref.at[peer], dst_ref=land_ref.at[peer],
                    send_sem=send_sems.at[0, k], recv_sem=recv_sems.at[0, k],
                    device_id=(x, y, c), device_id_type=pl.DeviceIdType.MESH).wait_recv()
            else:
                pltpu.make_async_remote_copy(
                    src_ref=s_ref, dst_ref=sland_ref.at[peer],
                    send_sem=send_sems.at[1, k], recv_sem=recv_sems.at[1, k],
                    device_id=(x, y, c), device_id_type=pl.DeviceIdType.MESH).wait_recv()
        for _, cp in sent:
            cp.wait_send()
        for cp in own:
            cp.wait()

    any_spec = pl.BlockSpec(memory_space=pl.ANY)
    return pl.pallas_call(
        body, name="exchange",
        in_specs=[any_spec, any_spec], out_specs=[any_spec, any_spec],
        out_shape=[jax.ShapeDtypeStruct(parts.shape, parts.dtype),
                   jax.ShapeDtypeStruct((NDEV,) + small.shape, small.dtype)],
        scratch_shapes=[pltpu.SemaphoreType.DMA((2, NDEV - 1)), pltpu.SemaphoreType.DMA((2, NDEV - 1)),
                        pltpu.SemaphoreType.DMA((2,))],
    )(parts, small)


def sum_slots(landed, name):
    _, r, c = landed.shape
    tr = _tile(r, (496, 248, 8))

    def body(l_ref, o_ref):
        acc = l_ref[0].astype(F32)
        for p in range(1, NDEV):
            acc = acc + l_ref[p].astype(F32)
        o_ref[...] = acc

    return pl.pallas_call(
        body, name=name, grid=(r // tr,),
        in_specs=[pl.BlockSpec((NDEV, tr, c), lambda i: (0, i, 0))],
        out_specs=pl.BlockSpec((tr, c), lambda i: (i, 0)),
        out_shape=jax.ShapeDtypeStruct((r, c), F32),
        compiler_params=_params("parallel"),
    )(landed)


def adamw(w, g, m, v, name):
    r, c = w.shape
    tr = _tile(r, (256, 128))

    def body(w_ref, g_ref, m_ref, v_ref, d_ref, nm_ref, nv_ref):
        g_ = g_ref[...]
        m_ = ADAM_B1 * m_ref[...] + (1.0 - ADAM_B1) * g_
        v_ = ADAM_B2 * v_ref[...] + (1.0 - ADAM_B2) * jnp.square(g_)
        m_hat = m_ / (1.0 - ADAM_B1 ** ADAM_STEP)
        v_hat = v_ / (1.0 - ADAM_B2 ** ADAM_STEP)
        d_ref[...] = -ADAM_LR * (m_hat / (jnp.sqrt(v_hat) + ADAM_EPS) + ADAM_WD * w_ref[...])
        nm_ref[...] = m_
        nv_ref[...] = v_

    spec = pl.BlockSpec((tr, c), lambda i: (i, 0))
    out = jax.ShapeDtypeStruct((r, c), F32)
    return pl.pallas_call(
        body, name=name, grid=(r // tr,), in_specs=[spec] * 4, out_specs=(spec,) * 3, out_shape=(out,) * 3,
        compiler_params=_params("parallel"),
    )(w, g, m, v)


def kernel(x, meta_tokens, w_in, w_ret_out, w_sb_out, w_out, w_ffn_in, w_ffn_out, norm_mix_pre, norm_mix_post, norm_ffn_pre, norm_ffn_post, loss_target, m_meta_tokens, m_w_in, m_w_ret_out, m_w_sb_out, m_w_out, m_w_ffn_in, m_w_ffn_out, m_norm_mix_pre, m_norm_mix_post, m_norm_ffn_pre, m_norm_ffn_post, v_meta_tokens, v_w_in, v_w_ret_out, v_w_sb_out, v_w_out, v_w_ffn_in, v_w_ffn_out, v_norm_mix_pre, v_norm_mix_post, v_norm_ffn_pre, v_norm_ffn_post):
    seq = x.shape[1]
    lq = seq + FRONT
    me = _device_index(lax.axis_index("x"), lax.axis_index("y"), lax.axis_index("c"))

    shards = {"w_in_t": w_in[0].T, "w_ffn_in_t": w_ffn_in[0].T, "w_ret_out": w_ret_out[0],
              "w_sb_out": w_sb_out[0], "w_out": w_out[0], "w_ffn_out": w_ffn_out[0]}
    pack = jnp.concatenate([shards[n].astype(BF16) for n, _ in PACK_ROWS], axis=0)
    gathered, meta_all = all_gather([pack, meta_tokens])
    full, off = {}, 0
    for n, r in PACK_ROWS:
        full[n] = gathered[:, off:off + r, :].reshape(NDEV * r, D)
        off += r
    meta_full = meta_all.transpose(1, 0, 2).reshape(N_META, D)

    pos = jnp.arange(lq, dtype=F32) - META0
    half = RDK // 2
    ang = pos[:, None] * (ROPE_BASE ** (-jnp.arange(half, dtype=F32) / half))[None, :]
    cos, sin = jnp.cos(ang), jnp.sin(ang)
    decay, coef = _retention_tables()

    hp = jnp.concatenate([jnp.zeros((META0, D), F32), meta_full, x[0]], axis=0)
    hn1 = rms_fwd(hp, norm_mix_pre, "rms_mix_pre")
    proj = mm_nt(hn1, full["w_in_t"], BF16, "proj")
    qr, kr = rotary_fwd(proj, cos, sin)
    o_ret, states = retention_fwd(qr, kr, proj, decay, coef)
    gr = ret_gate_fwd(proj, o_ret)
    o_sb, o_sb_f32 = sb_fwd(proj)
    y_ret = mm_nn(gr, full["w_ret_out"], F32, "y_ret")
    y_sb = mm_nn(o_sb, full["w_sb_out"], F32, "y_sb")
    merged = merge_fwd(proj, y_ret, y_sb)
    mix = mm_nn(merged, full["w_out"], F32, "mix")
    h1, hn2 = post_mix_fwd(hp, mix, norm_mix_post, norm_ffn_pre)
    ab = mm_nt(hn2, full["w_ffn_in_t"], BF16, "ffn_in")
    act = swiglu_fwd(ab)
    ff = mm_nn(act, full["w_ffn_out"], F32, "ffn_out")
    loss_blk, d_h2, d_ff, dg_ffn_post = loss_head(h1, ff, norm_ffn_post, loss_target[0])
    loss = lax.psum(loss_blk[0, 0], ("x", "y", "c"))

    grads = {}
    d_act = mm_nt(d_ff, full["w_ffn_out"], BF16, "d_act")
    grads["w_ffn_out"] = mm_tn(act, d_ff, BF16, "dw_ffn_out")
    d_ab = swiglu_bwd(ab, d_act)
    d_hn2 = mm_nn(d_ab, full["w_ffn_in_t"], F32, "d_hn2")
    grads["w_ffn_in_t"] = mm_tn(d_ab, hn2, BF16, "dw_ffn_in")
    d_h1, d_mix, dg_ffn_pre, dg_mix_post = post_mix_bwd(h1, d_hn2, norm_ffn_pre, d_h2, mix, norm_mix_post)
    d_merged = mm_nt(d_mix, full["w_out"], BF16, "d_merged")
    grads["w_out"] = mm_tn(merged, d_mix, BF16, "dw_out")
    d_y_ret, d_y_sb, d_gates = merge_bwd(proj, y_ret, y_sb, d_merged)
    d_gr = mm_nt(d_y_ret, full["w_ret_out"], BF16, "d_gr")
    grads["w_ret_out"] = mm_tn(gr, d_y_ret, BF16, "dw_ret_out")
    d_o_sb = mm_nt(d_y_sb, full["w_sb_out"], BF16, "d_o_sb")
    grads["w_sb_out"] = mm_tn(o_sb, d_y_sb, BF16, "dw_sb_out")
    d_rg, d_o_ret = ret_gate_bwd(proj, o_ret, d_gr)
    d_qr, d_kr, d_rv = retention_bwd(qr, kr, proj, d_o_ret, states, decay, coef)
    d_rqk = rotary_bwd(d_qr, d_kr, cos, sin)
    d_sq, d_sk, d_sv = sb_bwd(proj, o_sb_f32, d_o_sb)
    d_proj = jnp.concatenate([d_rqk, d_rv, d_rg, d_sq, d_sk.astype(BF16), d_sv.astype(BF16), d_gates], axis=1)
    d_hn1 = mm_nn(d_proj, full["w_in_t"], F32, "d_hn1")
    grads["w_in_t"] = mm_tn(d_proj, hn1, BF16, "dw_in")
    d_hp, dg_mix_pre = pre_mix_bwd(hp, d_hn1, norm_mix_pre, d_h1)
    grad_x = d_hp[FRONT:][None]

    parts = jnp.concatenate([grads[n].reshape(NDEV, r, D) for n, r in PACK_ROWS], axis=1)
    small = jnp.concatenate([dg_mix_pre, dg_mix_post, dg_ffn_pre, dg_ffn_post, d_hp[META0:FRONT]], axis=0)
    landed, small_landed = exchange(parts, small)
    gsum = sum_slots(landed, "sum_grads")
    ssum = sum_slots(small_landed, "sum_small")
    g, off = {}, 0
    for n, r in PACK_ROWS:
        g[n] = gsum[off:off + r]
        off += r
    gain = lambda k: ssum[k * GAIN_ROWS:k * GAIN_ROWS + 1]
    g_w = {"meta_tokens": lax.dynamic_slice(ssum[4 * GAIN_ROWS:], (0, me * (D // NDEV)), (N_META, D // NDEV)),
           "w_in": g["w_in_t"].T[None], "w_ret_out": g["w_ret_out"][None], "w_sb_out": g["w_sb_out"][None],
           "w_out": g["w_out"][None], "w_ffn_in": g["w_ffn_in_t"].T[None], "w_ffn_out": g["w_ffn_out"][None],
           "norm_mix_pre": gain(0), "norm_mix_post": gain(1), "norm_ffn_pre": gain(2), "norm_ffn_post": gain(3)}

    names = ["meta_tokens", "w_in", "w_ret_out", "w_sb_out", "w_out", "w_ffn_in", "w_ffn_out",
             "norm_mix_pre", "norm_mix_post", "norm_ffn_pre", "norm_ffn_post"]
    w_of = dict(zip(names, (meta_tokens, w_in, w_ret_out, w_sb_out, w_out, w_ffn_in, w_ffn_out,
                            norm_mix_pre, norm_mix_post, norm_ffn_pre, norm_ffn_post)))
    m_of = dict(zip(names, (m_meta_tokens, m_w_in, m_w_ret_out, m_w_sb_out, m_w_out, m_w_ffn_in, m_w_ffn_out,
                            m_norm_mix_pre, m_norm_mix_post, m_norm_ffn_pre, m_norm_ffn_post)))
    v_of = dict(zip(names, (v_meta_tokens, v_w_in, v_w_ret_out, v_w_sb_out, v_w_out, v_w_ffn_in, v_w_ffn_out,
                            v_norm_mix_pre, v_norm_mix_post, v_norm_ffn_pre, v_norm_ffn_post)))
    delta, new_m, new_v = {}, {}, {}
    for n in names:
        shape = w_of[n].shape
        two_d = (shape[-2], shape[-1])
        d_, m_, v_ = adamw(w_of[n].reshape(two_d), g_w[n].reshape(two_d), m_of[n].reshape(two_d),
                           v_of[n].reshape(two_d), "adamw_" + n)
        delta[n], new_m[n], new_v[n] = d_.reshape(shape), m_.reshape(shape), v_.reshape(shape)

    return (loss, grad_x, *[g_w[n] for n in names], *[delta[n] for n in names],
            *[new_m[n] for n in names], *[new_v[n] for n in names])
```

```python
import jax
import jax.numpy as jnp
from jax import lax
from jax.experimental import pallas as pl
from jax.experimental.pallas import tpu as pltpu

F32 = jnp.float32
BF16 = jnp.bfloat16

D = 1024
N_META = 16
CHUNK = 128
FRONT = 256
META0 = FRONT - N_META
RH, RDK, RDV = 4, 256, 512
SB_DH = 64
DFF = 2816
NDEV = 8
ROPE_BASE = 10000.0
NORM_EPS = 1e-6
GN_EPS = 1e-5
C_RQ, C_RK, C_RV, C_RG, C_SQ, C_SK, C_SV, C_GA, C_GB = 0, 1024, 2048, 4096, 6144, 7168, 8192, 9216, 10240
PROJ = 11264
PACK_ROWS = (("w_in_t", PROJ // NDEV), ("w_ffn_in_t", 2 * DFF // NDEV), ("w_ret_out", RH * RDV // NDEV),
             ("w_sb_out", D // NDEV), ("w_out", D // NDEV), ("w_ffn_out", DFF // NDEV))
PACK = sum(r for _, r in PACK_ROWS)
GAIN_ROWS = 8

ADAM_LR = 0.001
ADAM_B1 = 0.9
ADAM_B2 = 0.999
ADAM_EPS = 1e-08
ADAM_WD = 0.01
ADAM_STEP = 10

VMEM_LIMIT = 56 * 1024 * 1024
SB_T = 256

NT = (((1,), (1,)), ((), ()))
TN = (((0,), (0,)), ((), ()))


def _dot(a, b):
    return jnp.dot(a, b, preferred_element_type=F32)


def _dot_nt(a, b):
    return lax.dot_general(a, b, NT, preferred_element_type=F32)


def _dot_tn(a, b):
    return lax.dot_general(a, b, TN, preferred_element_type=F32)


WIDE_TILES = (1024, 1408, 512, 256)


def _tile(n, candidates):
    for t in candidates:
        if n % t == 0:
            return t
    return n


def _params(*sem):
    return pltpu.CompilerParams(dimension_semantics=sem, vmem_limit_bytes=VMEM_LIMIT)


def _rms_hat(x):
    r = lax.rsqrt(jnp.mean(x * x, axis=-1, keepdims=True) + NORM_EPS)
    return x * r, r


def _rms_bwd(xhat, r, g, dy):
    u = dy * g
    return r * (u - xhat * jnp.mean(u * xhat, axis=-1, keepdims=True))


def _gn(y):
    mu = jnp.mean(y, axis=-1, keepdims=True)
    yc = y - mu
    rs = lax.rsqrt(jnp.mean(yc * yc, axis=-1, keepdims=True) + GN_EPS)
    return yc * rs, rs


def _gn_bwd(yh, rs, d):
    return rs * (d - jnp.mean(d, axis=-1, keepdims=True) - yh * jnp.mean(d * yh, axis=-1, keepdims=True))


def _sigmoid(x):
    return 1.0 / (1.0 + jnp.exp(-x))


def mm_nt(a, b, out_dtype, name):
    m, k = a.shape
    n = b.shape[0]
    tm = _tile(m, (768, 512, 256))
    tn = _tile(n, WIDE_TILES)

    def body(a_ref, b_ref, o_ref):
        o_ref[...] = _dot_nt(a_ref[...], b_ref[...]).astype(o_ref.dtype)

    return pl.pallas_call(
        body, name=name, grid=(m // tm, n // tn),
        in_specs=[pl.BlockSpec((tm, k), lambda i, j: (i, 0)), pl.BlockSpec((tn, k), lambda i, j: (j, 0))],
        out_specs=pl.BlockSpec((tm, tn), lambda i, j: (i, j)),
        out_shape=jax.ShapeDtypeStruct((m, n), out_dtype),
        compiler_params=_params("parallel", "arbitrary"),
    )(a, b)


def mm_nn(a, b, out_dtype, name):
    m, k = a.shape
    n = b.shape[1]
    tm = _tile(m, (768, 512, 256))
    tk = _tile(k, WIDE_TILES)
    nk = k // tk

    def body(a_ref, b_ref, o_ref, acc_ref):
        kk = pl.program_id(1)

        @pl.when(kk == 0)
        def _():
            acc_ref[...] = jnp.zeros_like(acc_ref)

        acc_ref[...] += _dot(a_ref[...], b_ref[...])

        @pl.when(kk == nk - 1)
        def _():
            o_ref[...] = acc_ref[...].astype(o_ref.dtype)

    return pl.pallas_call(
        body, name=name, grid=(m // tm, nk),
        in_specs=[pl.BlockSpec((tm, tk), lambda i, kk: (i, kk)), pl.BlockSpec((tk, n), lambda i, kk: (kk, 0))],
        out_specs=pl.BlockSpec((tm, n), lambda i, kk: (i, 0)),
        out_shape=jax.ShapeDtypeStruct((m, n), out_dtype),
        scratch_shapes=[pltpu.VMEM((tm, n), F32)],
        compiler_params=_params("parallel", "arbitrary"),
    )(a, b)


def mm_tn(a, b, out_dtype, name):
    m, ka = a.shape
    n = b.shape[1]
    ta = _tile(ka, WIDE_TILES)
    tl = _tile(m, (768, 512, 256))
    nl = m // tl

    def body(a_ref, b_ref, o_ref, acc_ref):
        ll = pl.program_id(1)

        @pl.when(ll == 0)
        def _():
            acc_ref[...] = jnp.zeros_like(acc_ref)

        acc_ref[...] += _dot_tn(a_ref[...], b_ref[...])

        @pl.when(ll == nl - 1)
        def _():
            o_ref[...] = acc_ref[...].astype(o_ref.dtype)

    return pl.pallas_call(
        body, name=name, grid=(ka // ta, nl),
        in_specs=[pl.BlockSpec((tl, ta), lambda i, ll: (ll, i)), pl.BlockSpec((tl, n), lambda i, ll: (ll, 0))],
        out_specs=pl.BlockSpec((ta, n), lambda i, ll: (i, 0)),
        out_shape=jax.ShapeDtypeStruct((ka, n), out_dtype),
        scratch_shapes=[pltpu.VMEM((ta, n), F32)],
        compiler_params=_params("parallel", "arbitrary"),
    )(a, b)


TM = 256


def _rb(arr, width=None, col_block=0):
    w = arr.shape[1] if width is None else width
    return pl.BlockSpec((TM, w), lambda i: (i, col_block))


def _whole(arr):
    return pl.BlockSpec(arr.shape, lambda i: (0,) * arr.ndim)


def _rows_call(body, name, lq, ins, in_specs, out_shapes, out_specs):
    return pl.pallas_call(
        body, name=name, grid=(lq // TM,), in_specs=in_specs, out_specs=out_specs, out_shape=out_shapes,
        compiler_params=_params("arbitrary"),
    )(*ins)


def rms_fwd(h, g, name):
    lq = h.shape[0]

    def body(h_ref, g_ref, o_ref):
        xhat, _ = _rms_hat(h_ref[...])
        o_ref[...] = (xhat * g_ref[...]).astype(BF16)

    return _rows_call(body, name, lq, (h, g), [_rb(h), _whole(g)],
                      jax.ShapeDtypeStruct((lq, D), BF16), _rb(h))


def rotary_fwd(proj, cos, sin):
    lq = proj.shape[0]
    half = RDK // 2

    def body(p_ref, c_ref, s_ref, q_ref, k_ref):
        c, s = c_ref[...], s_ref[...]
        for col, o_ref, scale in ((C_RQ, q_ref, RDK ** -0.5), (C_RK, k_ref, 1.0)):
            for h in range(RH):
                x1 = p_ref[:, col + h * RDK: col + h * RDK + half].astype(F32)
                x2 = p_ref[:, col + h * RDK + half: col + (h + 1) * RDK].astype(F32)
                o_ref[:, h * RDK: h * RDK + half] = ((x1 * c - x2 * s) * scale).astype(BF16)
                o_ref[:, h * RDK + half: (h + 1) * RDK] = ((x1 * s + x2 * c) * scale).astype(BF16)

    out = jax.ShapeDtypeStruct((lq, RH * RDK), BF16)
    return _rows_call(body, "rotary_fwd", lq, (proj, cos, sin),
                      [_rb(proj, 2 * RH * RDK, 0), _rb(cos), _rb(sin)],
                      (out, out), (pl.BlockSpec((TM, RH * RDK), lambda i: (i, 0)),) * 2)


def rotary_bwd(dq, dk, cos, sin):
    lq = dq.shape[0]
    half = RDK // 2

    def body(dq_ref, dk_ref, c_ref, s_ref, o_ref):
        c, s = c_ref[...], s_ref[...]
        for col, d_ref, scale in ((C_RQ, dq_ref, RDK ** -0.5), (C_RK, dk_ref, 1.0)):
            for h in range(RH):
                d1 = d_ref[:, h * RDK: h * RDK + half]
                d2 = d_ref[:, h * RDK + half: (h + 1) * RDK]
                o_ref[:, col + h * RDK: col + h * RDK + half] = ((d1 * c + d2 * s) * scale).astype(BF16)
                o_ref[:, col + h * RDK + half: col + (h + 1) * RDK] = ((d2 * c - d1 * s) * scale).astype(BF16)

    return _rows_call(body, "rotary_bwd", lq, (dq, dk, cos, sin), [_rb(dq), _rb(dk), _rb(cos), _rb(sin)],
                      jax.ShapeDtypeStruct((lq, 2 * RH * RDK), BF16),
                      pl.BlockSpec((TM, 2 * RH * RDK), lambda i: (i, 0)))


def ret_gate_fwd(proj, o_ret):
    lq = proj.shape[0]

    def body(p_ref, y_ref, o_ref):
        for h in range(RH):
            sl = slice(h * RDV, (h + 1) * RDV)
            yh, _ = _gn(y_ref[:, sl])
            rg = p_ref[:, sl].astype(F32)
            o_ref[:, sl] = (rg * _sigmoid(rg) * yh).astype(BF16)

    return _rows_call(body, "ret_gate_fwd", lq, (proj, o_ret),
                      [_rb(proj, RH * RDV, C_RG // (RH * RDV)), _rb(o_ret)],
                      jax.ShapeDtypeStruct((lq, RH * RDV), BF16), _rb(o_ret))


def ret_gate_bwd(proj, o_ret, d_gr):
    lq = proj.shape[0]

    def body(p_ref, y_ref, d_ref, drg_ref, dy_ref):
        for h in range(RH):
            sl = slice(h * RDV, (h + 1) * RDV)
            yh, rs = _gn(y_ref[:, sl])
            rg = p_ref[:, sl].astype(F32)
            sg = _sigmoid(rg)
            d = d_ref[:, sl].astype(F32)
            drg_ref[:, sl] = (d * yh * sg * (1.0 + rg * (1.0 - sg))).astype(BF16)
            dy_ref[:, sl] = _gn_bwd(yh, rs, d * rg * sg).astype(BF16)

    out = jax.ShapeDtypeStruct((lq, RH * RDV), BF16)
    return _rows_call(body, "ret_gate_bwd", lq, (proj, o_ret, d_gr),
                      [_rb(proj, RH * RDV, C_RG // (RH * RDV)), _rb(o_ret), _rb(d_gr)],
                      (out, out), (_rb(o_ret), _rb(o_ret)))


def merge_fwd(proj, y_ret, y_sb):
    lq = proj.shape[0]

    def body(ga_ref, gb_ref, yr_ref, ys_ref, o_ref):
        o_ref[...] = (_sigmoid(ga_ref[...].astype(F32)) * yr_ref[...]
                      + _sigmoid(gb_ref[...].astype(F32)) * ys_ref[...]).astype(BF16)

    return _rows_call(body, "merge_fwd", lq, (proj, proj, y_ret, y_sb),
                      [_rb(proj, D, C_GA // D), _rb(proj, D, C_GB // D), _rb(y_ret), _rb(y_sb)],
                      jax.ShapeDtypeStruct((lq, D), BF16), _rb(y_ret))


def merge_bwd(proj, y_ret, y_sb, d_merged):
    lq = proj.shape[0]

    def body(ga_ref, gb_ref, yr_ref, ys_ref, d_ref, dyr_ref, dys_ref, dg_ref):
        d = d_ref[...].astype(F32)
        sa = _sigmoid(ga_ref[...].astype(F32))
        sb = _sigmoid(gb_ref[...].astype(F32))
        dyr_ref[...] = (d * sa).astype(BF16)
        dys_ref[...] = (d * sb).astype(BF16)
        dg_ref[:, :D] = (d * yr_ref[...] * sa * (1.0 - sa)).astype(BF16)
        dg_ref[:, D:] = (d * ys_ref[...] * sb * (1.0 - sb)).astype(BF16)

    o1 = jax.ShapeDtypeStruct((lq, D), BF16)
    return _rows_call(body, "merge_bwd", lq, (proj, proj, y_ret, y_sb, d_merged),
                      [_rb(proj, D, C_GA // D), _rb(proj, D, C_GB // D), _rb(y_ret), _rb(y_sb), _rb(d_merged)],
                      (o1, o1, jax.ShapeDtypeStruct((lq, 2 * D), BF16)),
                      (_rb(y_ret), _rb(y_ret), pl.BlockSpec((TM, 2 * D), lambda i: (i, 0))))


def post_mix_fwd(hp, mix, g_post, g_pre):
    lq = hp.shape[0]

    def body(h_ref, m_ref, g2_ref, g3_ref, h1_ref, hn_ref):
        mhat, _ = _rms_hat(m_ref[...])
        h1 = h_ref[...] + mhat * g2_ref[...]
        h1_ref[...] = h1
        hhat, _ = _rms_hat(h1)
        hn_ref[...] = (hhat * g3_ref[...]).astype(BF16)

    return _rows_call(body, "post_mix_fwd", lq, (hp, mix, g_post, g_pre),
                      [_rb(hp), _rb(mix), _whole(g_post), _whole(g_pre)],
                      (jax.ShapeDtypeStruct((lq, D), F32), jax.ShapeDtypeStruct((lq, D), BF16)),
                      (_rb(hp), _rb(hp)))


def swiglu_fwd(ab):
    lq = ab.shape[0]

    def body(a_ref, b_ref, o_ref):
        a = a_ref[...].astype(F32)
        o_ref[...] = (a * _sigmoid(a) * b_ref[...].astype(F32)).astype(BF16)

    return _rows_call(body, "swiglu_fwd", lq, (ab, ab), [_rb(ab, DFF, 0), _rb(ab, DFF, 1)],
                      jax.ShapeDtypeStruct((lq, DFF), BF16), pl.BlockSpec((TM, DFF), lambda i: (i, 0)))


def swiglu_bwd(ab, d_act):
    lq = ab.shape[0]

    def body(a_ref, b_ref, d_ref, o_ref):
        a = a_ref[...].astype(F32)
        b = b_ref[...].astype(F32)
        d = d_ref[...].astype(F32)
        sg = _sigmoid(a)
        o_ref[:, :DFF] = (d * b * sg * (1.0 + a * (1.0 - sg))).astype(BF16)
        o_ref[:, DFF:] = (d * a * sg).astype(BF16)

    return _rows_call(body, "swiglu_bwd", lq, (ab, ab, d_act), [_rb(ab, DFF, 0), _rb(ab, DFF, 1), _rb(d_act)],
                      jax.ShapeDtypeStruct((lq, 2 * DFF), BF16), pl.BlockSpec((TM, 2 * DFF), lambda i: (i, 0)))


def loss_head(h1, ff, g_post, target):
    lq = h1.shape[0]
    front_blocks = FRONT // TM

    def body(h_ref, f_ref, g_ref, t_ref, loss_ref, dh_ref, df_ref, dg_ref):
        i = pl.program_id(0)

        @pl.when(i == 0)
        def _():
            loss_ref[...] = jnp.zeros_like(loss_ref)
            dg_ref[...] = jnp.zeros_like(dg_ref)

        g = g_ref[...]
        fhat, r = _rms_hat(f_ref[...])
        is_x = (i >= front_blocks).astype(F32)
        diff = (h_ref[...] + fhat * g - t_ref[...]) * is_x
        loss_ref[...] += 0.5 * jnp.sum(diff * diff) / D
        dy = diff / D
        dh_ref[...] = dy
        df_ref[...] = _rms_bwd(fhat, r, g, dy).astype(BF16)
        dg_ref[...] += jnp.sum(dy * fhat, axis=0, keepdims=True)

    return _rows_call(
        body, "loss_head", lq, (h1, ff, g_post, target),
        [_rb(h1), _rb(ff), _whole(g_post),
         pl.BlockSpec((TM, D), lambda i: (jnp.maximum(i - front_blocks, 0), 0))],
        (jax.ShapeDtypeStruct((8, 128), F32), jax.ShapeDtypeStruct((lq, D), F32),
         jax.ShapeDtypeStruct((lq, D), BF16), jax.ShapeDtypeStruct((GAIN_ROWS, D), F32)),
        (pl.BlockSpec((8, 128), lambda i: (0, 0)), _rb(h1), _rb(h1), pl.BlockSpec((GAIN_ROWS, D), lambda i: (0, 0))))


def post_mix_bwd(h1, d_hn2, g_pre, d_h2, mix, g_post):
    lq = h1.shape[0]

    def body(h_ref, dn_ref, g3_ref, dh2_ref, m_ref, g2_ref, dh1_ref, dm_ref, dg3_ref, dg2_ref):
        i = pl.program_id(0)

        @pl.when(i == 0)
        def _():
            dg3_ref[...] = jnp.zeros_like(dg3_ref)
            dg2_ref[...] = jnp.zeros_like(dg2_ref)

        hhat, r = _rms_hat(h_ref[...])
        dn = dn_ref[...]
        d_h1 = dh2_ref[...] + _rms_bwd(hhat, r, g3_ref[...], dn)
        dh1_ref[...] = d_h1
        dg3_ref[...] += jnp.sum(dn * hhat, axis=0, keepdims=True)
        mhat, rm = _rms_hat(m_ref[...])
        dm_ref[...] = _rms_bwd(mhat, rm, g2_ref[...], d_h1).astype(BF16)
        dg2_ref[...] += jnp.sum(d_h1 * mhat, axis=0, keepdims=True)

    vec = jax.ShapeDtypeStruct((GAIN_ROWS, D), F32)
    vspec = pl.BlockSpec((GAIN_ROWS, D), lambda i: (0, 0))
    return _rows_call(body, "post_mix_bwd", lq, (h1, d_hn2, g_pre, d_h2, mix, g_post),
                      [_rb(h1), _rb(d_hn2), _whole(g_pre), _rb(d_h2), _rb(mix), _whole(g_post)],
                      (jax.ShapeDtypeStruct((lq, D), F32), jax.ShapeDtypeStruct((lq, D), BF16), vec, vec),
                      (_rb(h1), _rb(h1), vspec, vspec))


def pre_mix_bwd(hp, d_hn1, g_pre, d_h1):
    lq = hp.shape[0]

    def body(h_ref, dn_ref, g_ref, dh1_ref, dhp_ref, dg_ref):
        i = pl.program_id(0)

        @pl.when(i == 0)
        def _():
            dg_ref[...] = jnp.zeros_like(dg_ref)

        hhat, r = _rms_hat(h_ref[...])
        dn = dn_ref[...]
        dhp_ref[...] = dh1_ref[...] + _rms_bwd(hhat, r, g_ref[...], dn)
        dg_ref[...] += jnp.sum(dn * hhat, axis=0, keepdims=True)

    return _rows_call(body, "pre_mix_bwd", lq, (hp, d_hn1, g_pre, d_h1),
                      [_rb(hp), _rb(d_hn1), _whole(g_pre), _rb(d_h1)],
                      (jax.ShapeDtypeStruct((lq, D), F32), jax.ShapeDtypeStruct((GAIN_ROWS, D), F32)),
                      (_rb(hp), pl.BlockSpec((GAIN_ROWS, D), lambda i: (0, 0))))


def _retention_tables():
    log_g = jnp.log1p(-(2.0 ** (-5.0 - jnp.arange(RH, dtype=F32))))
    idx = jnp.arange(CHUNK, dtype=F32)
    diff = idx[:, None] - idx[None, :]
    decay = jnp.where(diff >= 0, jnp.exp(log_g[:, None, None] * jnp.maximum(diff, 0.0)), 0.0)
    zeta = jnp.exp(log_g[:, None] * (CHUNK - 1.0 - idx))
    xi = jnp.exp(log_g[:, None] * (idx + 1.0))
    g_chunk = jnp.broadcast_to(jnp.exp(log_g * CHUNK)[:, None], (RH, CHUNK))
    coef = jnp.stack([xi, zeta, g_chunk] + [jnp.zeros_like(xi)] * 125, axis=-1)
    return decay, coef


def retention_fwd(qr, kr, proj, decay, coef):
    lq = qr.shape[0]
    n = lq // CHUNK

    def body(q_ref, k_ref, v_ref, dec_ref, cf_ref, o_ref, st_ref, state):
        @pl.when(pl.program_id(1) == 0)
        def _():
            state[...] = jnp.zeros_like(state)

        q, k, v = q_ref[...], k_ref[...], v_ref[...]
        xi, zeta, gch = cf_ref[:, 0:1], cf_ref[:, 1:2], cf_ref[0:1, 2:3]
        st = state[...]
        stb = st.astype(BF16)
        st_ref[...] = stb
        s = _dot_nt(q, k) * dec_ref[...]
        o_ref[...] = _dot(s.astype(BF16), v) + _dot(q, stb) * xi
        vz = (v.astype(F32) * zeta).astype(BF16)
        state[...] = gch * st + _dot_tn(k, vz)

    return pl.pallas_call(
        body, name="retention_fwd", grid=(RH, n),
        in_specs=[pl.BlockSpec((CHUNK, RDK), lambda h, c: (c, h)), pl.BlockSpec((CHUNK, RDK), lambda h, c: (c, h)),
                  pl.BlockSpec((CHUNK, RDV), lambda h, c: (c, C_RV // RDV + h)),
                  pl.BlockSpec((None, CHUNK, CHUNK), lambda h, c: (h, 0, 0)),
                  pl.BlockSpec((None, CHUNK, 128), lambda h, c: (h, 0, 0))],
        out_specs=(pl.BlockSpec((CHUNK, RDV), lambda h, c: (c, h)),
                   pl.BlockSpec((None, None, RDK, RDV), lambda h, c: (c, h, 0, 0))),
        out_shape=(jax.ShapeDtypeStruct((lq, RH * RDV), F32), jax.ShapeDtypeStruct((n, RH, RDK, RDV), BF16)),
        scratch_shapes=[pltpu.VMEM((RDK, RDV), F32)],
        compiler_params=_params("parallel", "arbitrary"),
    )(qr, kr, proj, decay, coef)


def retention_bwd(qr, kr, proj, d_o, states, decay, coef):
    lq = qr.shape[0]
    n = lq // CHUNK

    def body(q_ref, k_ref, v_ref, do_ref, st_ref, dec_ref, cf_ref, dq_ref, dk_ref, dv_ref, dstate):
        @pl.when(pl.program_id(1) == 0)
        def _():
            dstate[...] = jnp.zeros_like(dstate)

        q, k, v, dob = q_ref[...], k_ref[...], v_ref[...], do_ref[...]
        xi, zeta, gch = cf_ref[:, 0:1], cf_ref[:, 1:2], cf_ref[0:1, 2:3]
        dec = dec_ref[...]
        dsn = dstate[...]
        dsnb = dsn.astype(BF16)
        dox = (dob.astype(F32) * xi).astype(BF16)
        sb = (_dot_nt(q, k) * dec).astype(BF16)
        dsb = (_dot_nt(dob, v) * dec).astype(BF16)
        vz = (v.astype(F32) * zeta).astype(BF16)
        dq_ref[...] = _dot(dsb, k) + _dot_nt(dox, st_ref[...])
        dk_ref[...] = _dot_tn(dsb, q) + _dot_nt(vz, dsnb)
        dv_ref[...] = (_dot_tn(sb, dob) + _dot(k, dsnb) * zeta).astype(BF16)
        dstate[...] = gch * dsn + _dot_tn(q, dox)

    rev = lambda c: n - 1 - c
    return pl.pallas_call(
        body, name="retention_bwd", grid=(RH, n),
        in_specs=[pl.BlockSpec((CHUNK, RDK), lambda h, c: (rev(c), h)),
                  pl.BlockSpec((CHUNK, RDK), lambda h, c: (rev(c), h)),
                  pl.BlockSpec((CHUNK, RDV), lambda h, c: (rev(c), C_RV // RDV + h)),
                  pl.BlockSpec((CHUNK, RDV), lambda h, c: (rev(c), h)),
                  pl.BlockSpec((None, None, RDK, RDV), lambda h, c: (rev(c), h, 0, 0)),
                  pl.BlockSpec((None, CHUNK, CHUNK), lambda h, c: (h, 0, 0)),
                  pl.BlockSpec((None, CHUNK, 128), lambda h, c: (h, 0, 0))],
        out_specs=(pl.BlockSpec((CHUNK, RDK), lambda h, c: (rev(c), h)),
                   pl.BlockSpec((CHUNK, RDK), lambda h, c: (rev(c), h)),
                   pl.BlockSpec((CHUNK, RDV), lambda h, c: (rev(c), h))),
        out_shape=(jax.ShapeDtypeStruct((lq, RH * RDK), F32), jax.ShapeDtypeStruct((lq, RH * RDK), F32),
                   jax.ShapeDtypeStruct((lq, RH * RDV), BF16)),
        scratch_shapes=[pltpu.VMEM((RDK, RDV), F32)],
        compiler_params=_params("parallel", "arbitrary"),
    )(qr, kr, proj, d_o, states, decay, coef)


def _sb_masks_and_u():
    lane = lax.broadcasted_iota(jnp.int32, (1, 2 * SB_DH), 1)
    lo = lane < SB_DH
    row = lax.broadcasted_iota(jnp.int32, (SB_T, SB_T), 0)
    col = lax.broadcasted_iota(jnp.int32, (SB_T, SB_T), 1)
    u = (row > col).astype(BF16)
    return lo, u, row, col


def _sb_rows(j):
    start = j * SB_T
    return pl.ds(start if isinstance(j, int) else pl.multiple_of(start, SB_T), SB_T)


SB_DEAD = -104.0


def _sb_alive(logs):
    m = logs[0]
    for l in logs[1:]:
        m = jnp.maximum(m, l)
    return (jnp.max(m) > SB_DEAD).astype(jnp.int32)


def _sb_walk(i, run, logs_of, step):
    def cond(c):
        return (c[0] >= 1) & (c[1] > 0)

    def body(c):
        r = step(c[0], c[2])
        return c[0] - 1, _sb_alive(logs_of(r)), r

    return lax.while_loop(cond, body, (i - 1, _sb_alive(logs_of(run)), run))


def _sb_logs(z):
    ls = jnp.minimum(z, 0.0) - jnp.log(1.0 + jnp.exp(-jnp.abs(z)))
    return ls, ls - z


def sb_fwd(proj):
    lq = proj.shape[0]
    t = SB_T

    def body(q_ref, k_ref, v_ref, o_ref, of_ref, acc_ref):
        i = pl.program_id(1)
        lo, u, row, col = _sb_masks_and_u()
        qs = (q_ref[...].astype(F32) * SB_DH ** -0.5).astype(BF16)
        zero = jnp.zeros_like(qs)
        qh = (jnp.where(lo, qs, zero), jnp.where(lo, zero, qs))
        acc_ref[...] = jnp.zeros_like(acc_ref)

        def block(j, run, masked):
            rows = _sb_rows(j)
            ks, vs = k_ref[rows, :], v_ref[rows, :]
            if masked:
                valid = (col + j * t < row + i * t) & (col + j * t >= META0)
            out = []
            for hh in range(2):
                ls, ln = _sb_logs(_dot_nt(qh[hh], ks))
                if masked:
                    ln = jnp.where(valid, ln, 0.0)
                a = jnp.exp(ls + _dot(ln.astype(BF16), u) + run[hh])
                if masked:
                    a = jnp.where(valid, a, 0.0)
                acc_ref[hh] += _dot(a.astype(BF16), vs)
                out.append(run[hh] + jnp.sum(ln, axis=1, keepdims=True))
            return tuple(out)

        zeros = jnp.zeros((t, 1), F32)
        run = block(i, (zeros, zeros), True)
        _, go, run = _sb_walk(i, run, lambda r: r, lambda j, r: block(j, r, False))

        @pl.when((i > 0) & (go > 0))
        def _():
            block(0, run, True)

        o = jnp.where(lo, acc_ref[0], acc_ref[1])
        o_ref[...] = o.astype(BF16)
        of_ref[...] = o

    blk = pl.BlockSpec((t, 128), lambda p, i: (i, p))
    return pl.pallas_call(
        body, name="sb_fwd", grid=(D // 128, lq // t),
        in_specs=[pl.BlockSpec((t, 128), lambda p, i: (i, C_SQ // 128 + p)),
                  pl.BlockSpec((lq, 128), lambda p, i: (0, C_SK // 128 + p)),
                  pl.BlockSpec((lq, 128), lambda p, i: (0, C_SV // 128 + p))],
        out_specs=(blk, blk),
        out_shape=(jax.ShapeDtypeStruct((lq, D), BF16), jax.ShapeDtypeStruct((lq, D), F32)),
        scratch_shapes=[pltpu.VMEM((2, t, 128), F32)],
        compiler_params=_params("parallel", "arbitrary"),
    )(proj, proj, proj)


def sb_bwd(proj, o, d_o):
    lq = proj.shape[0]
    t = SB_T

    def body(q_ref, k_ref, v_ref, o_ref, do_ref, dq_ref, dk_ref, dv_ref, acc_ref):
        i = pl.program_id(1)

        @pl.when(i == 0)
        def _():
            dk_ref[...] = jnp.zeros_like(dk_ref)
            dv_ref[...] = jnp.zeros_like(dv_ref)

        lo, u, row, col = _sb_masks_and_u()
        qs = (q_ref[...].astype(F32) * SB_DH ** -0.5).astype(BF16)
        do = do_ref[...]
        zero = jnp.zeros_like(qs)
        qh = (jnp.where(lo, qs, zero), jnp.where(lo, zero, qs))
        doh = (jnp.where(lo, do, zero), jnp.where(lo, zero, do))
        prod = o_ref[...] * do.astype(F32)
        dsum = (jnp.sum(jnp.where(lo, prod, 0.0), axis=1, keepdims=True),
                jnp.sum(jnp.where(lo, 0.0, prod), axis=1, keepdims=True))
        acc_ref[...] = jnp.zeros_like(acc_ref)

        def block(j, run, masked):
            rows = _sb_rows(j)
            ks, vs = k_ref[rows, :], v_ref[rows, :]
            if masked:
                valid = (col + j * t < row + i * t) & (col + j * t >= META0)
            out = []
            for hh in range(2):
                run_ln, run_e = run[2 * hh], run[2 * hh + 1]
                ls, ln = _sb_logs(_dot_nt(qh[hh], ks))
                if masked:
                    ln = jnp.where(valid, ln, 0.0)
                a = jnp.exp(ls + _dot(ln.astype(BF16), u) + run_ln)
                if masked:
                    a = jnp.where(valid, a, 0.0)
                ab = a.astype(BF16)
                e = ab.astype(F32) * _dot_nt(doh[hh], vs)
                e_hi = e.astype(BF16)
                e_lo = (e - e_hi.astype(F32)).astype(BF16)
                dz = e - jnp.exp(ls) * (dsum[hh] - run_e - (_dot(e_hi, u) + _dot(e_lo, u)))
                if masked:
                    dz = jnp.where(valid, dz, 0.0)
                dzb = dz.astype(BF16)
                acc_ref[hh] += _dot(dzb, ks)
                dk_ref[rows, :] += _dot_tn(dzb, qh[hh])
                dv_ref[rows, :] += _dot_tn(ab, doh[hh])
                out += [run_ln + jnp.sum(ln, axis=1, keepdims=True), run_e + jnp.sum(e, axis=1, keepdims=True)]
            return tuple(out)

        zeros = jnp.zeros((t, 1), F32)
        run = block(i, (zeros,) * 4, True)
        _, go, run = _sb_walk(i, run, lambda r: (r[0], r[2]), lambda j, r: block(j, r, False))

        @pl.when((i > 0) & (go > 0))
        def _():
            block(0, run, True)

        dq_ref[...] = (jnp.where(lo, acc_ref[0], acc_ref[1]) * SB_DH ** -0.5).astype(BF16)

    blk = pl.BlockSpec((t, 128), lambda p, i: (i, p))
    col_blk = pl.BlockSpec((lq, 128), lambda p, i: (0, p))
    return pl.pallas_call(
        body, name="sb_bwd", grid=(D // 128, lq // t),
        in_specs=[pl.BlockSpec((t, 128), lambda p, i: (i, C_SQ // 128 + p)),
                  pl.BlockSpec((lq, 128), lambda p, i: (0, C_SK // 128 + p)),
                  pl.BlockSpec((lq, 128), lambda p, i: (0, C_SV // 128 + p)), blk, blk],
        out_specs=(blk, col_blk, col_blk),
        out_shape=(jax.ShapeDtypeStruct((lq, D), BF16), jax.ShapeDtypeStruct((lq, D), F32),
                   jax.ShapeDtypeStruct((lq, D), F32)),
        scratch_shapes=[pltpu.VMEM((2, t, 128), F32)],
        compiler_params=_params("parallel", "arbitrary"),
    )(proj, proj, proj, o, d_o)


def _device_index(px, py, pc):
    return 4 * px + 2 * py + pc


def all_gather(blocks):
    nb = len(blocks)

    def body(*refs):
        x_refs, out_refs = refs[:nb], refs[nb:2 * nb]
        send_sems, recv_sems, local_sems = refs[2 * nb:]
        x, y, c = lax.axis_index("x"), lax.axis_index("y"), lax.axis_index("c")
        me, sibling = (x, y, c), (x, y, 1 - c)
        chips = [(1 - x, y), (x, 1 - y), (1 - x, 1 - y)]

        def copy(b, k, block, to, src=None):
            slot = out_refs[b].at[_device_index(*block)]
            return pltpu.make_async_remote_copy(
                src_ref=slot if src is None else src, dst_ref=slot,
                send_sem=send_sems.at[b, k], recv_sem=recv_sems.at[b, k],
                device_id=to, device_id_type=pl.DeviceIdType.MESH)

        mine = [pltpu.make_async_copy(x_refs[b], out_refs[b].at[_device_index(*me)], local_sems.at[b])
                for b in range(nb)]
        for cp in mine:
            cp.start()
        first = []
        for b in range(nb):
            first.append(copy(b, 0, me, sibling, src=x_refs[b]))
            first += [copy(b, 1 + j, me, (*chip, c), src=x_refs[b]) for j, chip in enumerate(chips)]
        for cp in first:
            cp.start()
        passed = []
        for j, chip in enumerate(chips):
            for b in range(nb):
                copy(b, 1 + j, (*chip, c), me).wait_recv()
                cp = copy(b, 4 + j, (*chip, c), sibling)
                cp.start()
                passed.append(cp)
        for b in range(nb):
            copy(b, 0, sibling, me).wait_recv()
            for j, chip in enumerate(chips):
                copy(b, 4 + j, (*chip, 1 - c), me).wait_recv()
        for cp in first + passed:
            cp.wait_send()
        for cp in mine:
            cp.wait()

    any_spec = pl.BlockSpec(memory_space=pl.ANY)
    return pl.pallas_call(
        body, name="all_gather",
        in_specs=[any_spec] * nb, out_specs=[any_spec] * nb,
        out_shape=[jax.ShapeDtypeStruct((NDEV,) + b.shape, b.dtype) for b in blocks],
        scratch_shapes=[pltpu.SemaphoreType.DMA((nb, 7)), pltpu.SemaphoreType.DMA((nb, 7)),
                        pltpu.SemaphoreType.DMA((nb,))],
    )(*blocks)


def exchange(parts, small):
    def body(g_ref, s_ref, land_ref, sland_ref, send_sems, recv_sems, local_sems):
        x, y, c = lax.axis_index("x"), lax.axis_index("y"), lax.axis_index("c")
        me = _device_index(x, y, c)
        own = [pltpu.make_async_copy(g_ref.at[me], land_ref.at[me], local_sems.at[0]),
               pltpu.make_async_copy(s_ref, sland_ref.at[me], local_sems.at[1])]
        for cp in own:
            cp.start()
        sent = []
        for k in range(1, NDEV):
            px = 1 - x if k & 4 else x
            py = 1 - y if k & 2 else y
            pc = 1 - c if k & 1 else c
            peer = _device_index(px, py, pc)
            sent.append((peer, pltpu.make_async_remote_copy(
                src_ref=g_ref.at[peer], dst_ref=land_ref.at[me],
                send_sem=send_sems.at[0, k - 1], recv_sem=recv_sems.at[0, k - 1],
                device_id=(px, py, pc), device_id_type=pl.DeviceIdType.MESH)))
            sent.append((peer, pltpu.make_async_remote_copy(
                src_ref=s_ref, dst_ref=sland_ref.at[me],
                send_sem=send_sems.at[1, k - 1], recv_sem=recv_sems.at[1, k - 1],
                device_id=(px, py, pc), device_id_type=pl.DeviceIdType.MESH)))
        for _, cp in sent:
            cp.start()
        for n, (peer, cp) in enumerate(sent):
            k, which = n // 2, n % 2
            if which == 0:
                pltpu.make_async_remote_copy(
                    src_ref=g_ref.at[peer], dst_ref=land_ref.at[peer],
                    send_sem=send_sems.at[0, k], recv_sem=recv_sems.at[0, k],
                    device_id=(x, y, c), device_id_type=pl.DeviceIdType.MESH).wait_recv()
            else:
                pltpu.make_async_remote_copy(
                    src_ref=s_ref, dst_ref=sland_ref.at[peer],
                    send_sem=send_sems.at[1, k], recv_sem=recv_sems.at[1, k],
                    device_id=(x, y, c), device_id_type=pl.DeviceIdType.MESH).wait_recv()
        for _, cp in sent:
            cp.wait_send()
        for cp in own:
            cp.wait()

    any_spec = pl.BlockSpec(memory_space=pl.ANY)
    return pl.pallas_call(
        body, name="exchange",
        in_specs=[any_spec, any_spec], out_specs=[any_spec, any_spec],
        out_shape=[jax.ShapeDtypeStruct(parts.shape, parts.dtype),
                   jax.ShapeDtypeStruct((NDEV,) + small.shape, small.dtype)],
        scratch_shapes=[pltpu.SemaphoreType.DMA((2, NDEV - 1)), pltpu.SemaphoreType.DMA((2, NDEV - 1)),
                        pltpu.SemaphoreType.DMA((2,))],
    )(parts, small)


def sum_slots(landed, name):
    _, r, c = landed.shape
    tr = _tile(r, (496, 248, 8))

    def body(l_ref, o_ref):
        acc = l_ref[0].astype(F32)
        for p in range(1, NDEV):
            acc = acc + l_ref[p].astype(F32)
        o_ref[...] = acc

    return pl.pallas_call(
        body, name=name, grid=(r // tr,),
        in_specs=[pl.BlockSpec((NDEV, tr, c), lambda i: (0, i, 0))],
        out_specs=pl.BlockSpec((tr, c), lambda i: (i, 0)),
        out_shape=jax.ShapeDtypeStruct((r, c), F32),
        compiler_params=_params("parallel"),
    )(landed)


def adamw(w, g, m, v, name):
    r, c = w.shape
    tr = _tile(r, (256, 128))

    def body(w_ref, g_ref, m_ref, v_ref, d_ref, nm_ref, nv_ref):
        g_ = g_ref[...]
        m_ = ADAM_B1 * m_ref[...] + (1.0 - ADAM_B1) * g_
        v_ = ADAM_B2 * v_ref[...] + (1.0 - ADAM_B2) * jnp.square(g_)
        m_hat = m_ / (1.0 - ADAM_B1 ** ADAM_STEP)
        v_hat = v_ / (1.0 - ADAM_B2 ** ADAM_STEP)
        d_ref[...] = -ADAM_LR * (m_hat / (jnp.sqrt(v_hat) + ADAM_EPS) + ADAM_WD * w_ref[...])
        nm_ref[...] = m_
        nv_ref[...] = v_

    spec = pl.BlockSpec((tr, c), lambda i: (i, 0))
    out = jax.ShapeDtypeStruct((r, c), F32)
    return pl.pallas_call(
        body, name=name, grid=(r // tr,), in_specs=[spec] * 4, out_specs=(spec,) * 3, out_shape=(out,) * 3,
        compiler_params=_params("parallel"),
    )(w, g, m, v)


def kernel(x, meta_tokens, w_in, w_ret_out, w_sb_out, w_out, w_ffn_in, w_ffn_out, norm_mix_pre, norm_mix_post, norm_ffn_pre, norm_ffn_post, loss_target, m_meta_tokens, m_w_in, m_w_ret_out, m_w_sb_out, m_w_out, m_w_ffn_in, m_w_ffn_out, m_norm_mix_pre, m_norm_mix_post, m_norm_ffn_pre, m_norm_ffn_post, v_meta_tokens, v_w_in, v_w_ret_out, v_w_sb_out, v_w_out, v_w_ffn_in, v_w_ffn_out, v_norm_mix_pre, v_norm_mix_post, v_norm_ffn_pre, v_norm_ffn_post):
    seq = x.shape[1]
    lq = seq + FRONT
    me = _device_index(lax.axis_index("x"), lax.axis_index("y"), lax.axis_index("c"))

    shards = {"w_in_t": w_in[0].T, "w_ffn_in_t": w_ffn_in[0].T, "w_ret_out": w_ret_out[0],
              "w_sb_out": w_sb_out[0], "w_out": w_out[0], "w_ffn_out": w_ffn_out[0]}
    pack = jnp.concatenate([shards[n].astype(BF16) for n, _ in PACK_ROWS], axis=0)
    gathered, meta_all = all_gather([pack, meta_tokens])
    full, off = {}, 0
    for n, r in PACK_ROWS:
        full[n] = gathered[:, off:off + r, :].reshape(NDEV * r, D)
        off += r
    meta_full = meta_all.transpose(1, 0, 2).reshape(N_META, D)

    pos = jnp.arange(lq, dtype=F32) - META0
    half = RDK // 2
    ang = pos[:, None] * (ROPE_BASE ** (-jnp.arange(half, dtype=F32) / half))[None, :]
    cos, sin = jnp.cos(ang), jnp.sin(ang)
    decay, coef = _retention_tables()

    hp = jnp.concatenate([jnp.zeros((META0, D), F32), meta_full, x[0]], axis=0)
    hn1 = rms_fwd(hp, norm_mix_pre, "rms_mix_pre")
    proj = mm_nt(hn1, full["w_in_t"], BF16, "proj")
    qr, kr = rotary_fwd(proj, cos, sin)
    o_ret, states = retention_fwd(qr, kr, proj, decay, coef)
    gr = ret_gate_fwd(proj, o_ret)
    o_sb, o_sb_f32 = sb_fwd(proj)
    y_ret = mm_nn(gr, full["w_ret_out"], F32, "y_ret")
    y_sb = mm_nn(o_sb, full["w_sb_out"], F32, "y_sb")
    merged = merge_fwd(proj, y_ret, y_sb)
    mix = mm_nn(merged, full["w_out"], F32, "mix")
    h1, hn2 = post_mix_fwd(hp, mix, norm_mix_post, norm_ffn_pre)
    ab = mm_nt(hn2, full["w_ffn_in_t"], BF16, "ffn_in")
    act = swiglu_fwd(ab)
    ff = mm_nn(act, full["w_ffn_out"], F32, "ffn_out")
    loss_blk, d_h2, d_ff, dg_ffn_post = loss_head(h1, ff, norm_ffn_post, loss_target[0])
    loss = lax.psum(loss_blk[0, 0], ("x", "y", "c"))

    grads = {}
    d_act = mm_nt(d_ff, full["w_ffn_out"], BF16, "d_act")
    grads["w_ffn_out"] = mm_tn(act, d_ff, BF16, "dw_ffn_out")
    d_ab = swiglu_bwd(ab, d_act)
    d_hn2 = mm_nn(d_ab, full["w_ffn_in_t"], F32, "d_hn2")
    grads["w_ffn_in_t"] = mm_tn(d_ab, hn2, BF16, "dw_ffn_in")
    d_h1, d_mix, dg_ffn_pre, dg_mix_post = post_mix_bwd(h1, d_hn2, norm_ffn_pre, d_h2, mix, norm_mix_post)
    d_merged = mm_nt(d_mix, full["w_out"], BF16, "d_merged")
    grads["w_out"] = mm_tn(merged, d_mix, BF16, "dw_out")
    d_y_ret, d_y_sb, d_gates = merge_bwd(proj, y_ret, y_sb, d_merged)
    d_gr = mm_nt(d_y_ret, full["w_ret_out"], BF16, "d_gr")
    grads["w_ret_out"] = mm_tn(gr, d_y_ret, BF16, "dw_ret_out")
    d_o_sb = mm_nt(d_y_sb, full["w_sb_out"], BF16, "d_o_sb")
    grads["w_sb_out"] = mm_tn(o_sb, d_y_sb, BF16, "dw_sb_out")
    d_rg, d_o_ret = ret_gate_bwd(proj, o_ret, d_gr)
    d_qr, d_kr, d_rv = retention_bwd(qr, kr, proj, d_o_ret, states, decay, coef)
    d_rqk = rotary_bwd(d_qr, d_kr, cos, sin)
    d_sq, d_sk, d_sv = sb_bwd(proj, o_sb_f32, d_o_sb)
    d_proj = jnp.concatenate([d_rqk, d_rv, d_rg, d_sq, d_sk.astype(BF16), d_sv.astype(BF16), d_gates], axis=1)
    d_hn1 = mm_nn(d_proj, full["w_in_t"], F32, "d_hn1")
    grads["w_in_t"] = mm_tn(d_proj, hn1, BF16, "dw_in")
    d_hp, dg_mix_pre = pre_mix_bwd(hp, d_hn1, norm_mix_pre, d_h1)
    grad_x = d_hp[FRONT:][None]

    parts = jnp.concatenate([grads[n].reshape(NDEV, r, D) for n, r in PACK_ROWS], axis=1)
    small = jnp.concatenate([dg_mix_pre, dg_mix_post, dg_ffn_pre, dg_ffn_post, d_hp[META0:FRONT]], axis=0)
    landed, small_landed = exchange(parts, small)
    gsum = sum_slots(landed, "sum_grads")
    ssum = sum_slots(small_landed, "sum_small")
    g, off = {}, 0
    for n, r in PACK_ROWS:
        g[n] = gsum[off:off + r]
        off += r
    gain = lambda k: ssum[k * GAIN_ROWS:k * GAIN_ROWS + 1]
    g_w = {"meta_tokens": lax.dynamic_slice(ssum[4 * GAIN_ROWS:], (0, me * (D // NDEV)), (N_META, D // NDEV)),
           "w_in": g["w_in_t"].T[None], "w_ret_out": g["w_ret_out"][None], "w_sb_out": g["w_sb_out"][None],
           "w_out": g["w_out"][None], "w_ffn_in": g["w_ffn_in_t"].T[None], "w_ffn_out": g["w_ffn_out"][None],
           "norm_mix_pre": gain(0), "norm_mix_post": gain(1), "norm_ffn_pre": gain(2), "norm_ffn_post": gain(3)}

    names = ["meta_tokens", "w_in", "w_ret_out", "w_sb_out", "w_out", "w_ffn_in", "w_ffn_out",
             "norm_mix_pre", "norm_mix_post", "norm_ffn_pre", "norm_ffn_post"]
    w_of = dict(zip(names, (meta_tokens, w_in, w_ret_out, w_sb_out, w_out, w_ffn_in, w_ffn_out,
                            norm_mix_pre, norm_mix_post, norm_ffn_pre, norm_ffn_post)))
    m_of = dict(zip(names, (m_meta_tokens, m_w_in, m_w_ret_out, m_w_sb_out, m_w_out, m_w_ffn_in, m_w_ffn_out,
                            m_norm_mix_pre, m_norm_mix_post, m_norm_ffn_pre, m_norm_ffn_post)))
    v_of = dict(zip(names, (v_meta_tokens, v_w_in, v_w_ret_out, v_w_sb_out, v_w_out, v_w_ffn_in, v_w_ffn_out,
                            v_norm_mix_pre, v_norm_mix_post, v_norm_ffn_pre, v_norm_ffn_post)))
    delta, new_m, new_v = {}, {}, {}
    for n in names:
        shape = w_of[n].shape
        two_d = (shape[-2], shape[-1])
        d_, m_, v_ = adamw(w_of[n].reshape(two_d), g_w[n].reshape(two_d), m_of[n].reshape(two_d),
                           v_of[n].reshape(two_d), "adamw_" + n)
        delta[n], new_m[n], new_v[n] = d_.reshape(shape), m_.reshape(shape), v_.reshape(shape)

    return (loss, grad_x, *[g_w[n] for n in names], *[delta[n] for n in names],
            *[new_m[n] for n in names], *[new_v[n] for n in names])
```

```python
import jax
import jax.numpy as jnp
from jax import lax
from jax.experimental import pallas as pl
from jax.experimental.pallas import tpu as pltpu

F32 = jnp.float32
BF16 = jnp.bfloat16

D = 1024
N_META = 16
CHUNK = 128
FRONT = 256
META0 = FRONT - N_META
RH, RDK, RDV = 4, 256, 512
SB_DH = 64
DFF = 2816
NDEV = 8
ROPE_BASE = 10000.0
NORM_EPS = 1e-6
GN_EPS = 1e-5
C_RQ, C_RK, C_RV, C_RG, C_SQ, C_SK, C_SV, C_GA, C_GB = 0, 1024, 2048, 4096, 6144, 7168, 8192, 9216, 10240
PROJ = 11264
PACK_ROWS = (("w_in_t", PROJ // NDEV), ("w_ffn_in_t", 2 * DFF // NDEV), ("w_ret_out", RH * RDV // NDEV),
             ("w_sb_out", D // NDEV), ("w_out", D // NDEV), ("w_ffn_out", DFF // NDEV))
PACK = sum(r for _, r in PACK_ROWS)
GAIN_ROWS = 8

ADAM_LR = 0.001
ADAM_B1 = 0.9
ADAM_B2 = 0.999
ADAM_EPS = 1e-08
ADAM_WD = 0.01
ADAM_STEP = 10

VMEM_LIMIT = 56 * 1024 * 1024
SB_T = 256

NT = (((1,), (1,)), ((), ()))
TN = (((0,), (0,)), ((), ()))


def _dot(a, b):
    return jnp.dot(a, b, preferred_element_type=F32)


def _dot_nt(a, b):
    return lax.dot_general(a, b, NT, preferred_element_type=F32)


def _dot_tn(a, b):
    return lax.dot_general(a, b, TN, preferred_element_type=F32)


WIDE_TILES = (1024, 1408, 512, 256)


def _tile(n, candidates):
    for t in candidates:
        if n % t == 0:
            return t
    return n


def _params(*sem):
    return pltpu.CompilerParams(dimension_semantics=sem, vmem_limit_bytes=VMEM_LIMIT)


def _rms_hat(x):
    r = lax.rsqrt(jnp.mean(x * x, axis=-1, keepdims=True) + NORM_EPS)
    return x * r, r


def _rms_bwd(xhat, r, g, dy):
    u = dy * g
    return r * (u - xhat * jnp.mean(u * xhat, axis=-1, keepdims=True))


def _gn(y):
    mu = jnp.mean(y, axis=-1, keepdims=True)
    yc = y - mu
    rs = lax.rsqrt(jnp.mean(yc * yc, axis=-1, keepdims=True) + GN_EPS)
    return yc * rs, rs


def _gn_bwd(yh, rs, d):
    return rs * (d - jnp.mean(d, axis=-1, keepdims=True) - yh * jnp.mean(d * yh, axis=-1, keepdims=True))


def _sigmoid(x):
    return 1.0 / (1.0 + jnp.exp(-x))


def _device_index(px, py, pc):
    return 4 * px + 2 * py + pc


def _direct_exchange(src_for, land_ref, send_sems, recv_sems, local_sem):
    x, y, c = lax.axis_index("x"), lax.axis_index("y"), lax.axis_index("c")
    me = _device_index(x, y, c)
    peers = []
    for k in range(1, NDEV):
        pos = (1 - x if k & 4 else x, 1 - y if k & 2 else y, 1 - c if k & 1 else c)
        peers.append((k - 1, pos, _device_index(*pos)))

    def local():
        return pltpu.make_async_copy(src_for(me), land_ref.at[me], local_sem)

    def remote(k, pos, idx):
        return pltpu.make_async_remote_copy(
            src_ref=src_for(idx), dst_ref=land_ref.at[me], send_sem=send_sems.at[k], recv_sem=recv_sems.at[k],
            device_id=pos, device_id_type=pl.DeviceIdType.MESH)

    def arrival(k, idx):
        return pltpu.make_async_remote_copy(
            src_ref=src_for(idx), dst_ref=land_ref.at[idx], send_sem=send_sems.at[k], recv_sem=recv_sems.at[k],
            device_id=(x, y, c), device_id_type=pl.DeviceIdType.MESH)

    def start():
        local().start()
        for p in peers:
            remote(*p).start()

    def wait():
        for k, _, idx in peers:
            arrival(k, idx).wait_recv()
        for p in peers:
            remote(*p).wait_send()
        local().wait()

    return start, wait


def _mm_call(compute, name, grid, arrays, in_specs, out_shape, out_spec, acc_shape, exch):
    steps = grid[0] * grid[1]

    def body(*refs):
        refs = list(refs)
        a_ref, b_ref = refs[:2]
        src_ref = refs.pop(2) if exch else None
        o_ref = refs[2]
        land_ref = refs.pop(3) if exch else None
        acc_ref = refs[3] if acc_shape else None
        if exch:
            send_sems, recv_sems, local_sem = refs[-3:]
            src_for = (lambda d: src_ref) if exch[1] == "gather" else (lambda d: src_ref.at[d])
            start, wait = _direct_exchange(src_for, land_ref, send_sems, recv_sems, local_sem)
            step = pl.program_id(0) * grid[1] + pl.program_id(1)
            pl.when(step == 0)(start)
        compute(a_ref, b_ref, o_ref, acc_ref)
        if exch:
            pl.when(step == steps - 1)(wait)

    out_shapes, out_specs, scratch = [out_shape], [out_spec], []
    if acc_shape:
        scratch.append(pltpu.VMEM(acc_shape, F32))
    if exch:
        src = exch[0]
        any_spec = pl.BlockSpec(memory_space=pl.ANY)
        arrays, in_specs = list(arrays) + [src], list(in_specs) + [any_spec]
        land = src.shape if exch[1] == "scatter" else (NDEV,) + src.shape
        out_shapes.append(jax.ShapeDtypeStruct(land, src.dtype))
        out_specs.append(any_spec)
        scratch += [pltpu.SemaphoreType.DMA((NDEV - 1,)), pltpu.SemaphoreType.DMA((NDEV - 1,)),
                    pltpu.SemaphoreType.DMA(())]
    out = pl.pallas_call(
        body, name=name, grid=grid, in_specs=in_specs, out_specs=out_specs, out_shape=out_shapes,
        scratch_shapes=scratch,
        compiler_params=_params(*(("arbitrary", "arbitrary") if exch else ("parallel", "arbitrary"))),
    )(*arrays)
    return tuple(out) if exch else out[0]


def mm_nt(a, b, out_dtype, name, exch=None):
    m, k = a.shape
    n = b.shape[0]
    tm = _tile(m, (768, 512, 256))
    tn = _tile(n, WIDE_TILES)

    def compute(a_ref, b_ref, o_ref, acc_ref):
        o_ref[...] = _dot_nt(a_ref[...], b_ref[...]).astype(o_ref.dtype)

    return _mm_call(
        compute, name, (m // tm, n // tn), (a, b),
        [pl.BlockSpec((tm, k), lambda i, j: (i, 0)), pl.BlockSpec((tn, k), lambda i, j: (j, 0))],
        jax.ShapeDtypeStruct((m, n), out_dtype), pl.BlockSpec((tm, tn), lambda i, j: (i, j)), None, exch)


def _accumulate(dot, steps):
    def compute(a_ref, b_ref, o_ref, acc_ref):
        kk = pl.program_id(1)

        @pl.when(kk == 0)
        def _():
            acc_ref[...] = jnp.zeros_like(acc_ref)

        acc_ref[...] += dot(a_ref[...], b_ref[...])

        @pl.when(kk == steps - 1)
        def _():
            o_ref[...] = acc_ref[...].astype(o_ref.dtype)

    return compute


def mm_nn(a, b, out_dtype, name, exch=None):
    m, k = a.shape
    n = b.shape[1]
    tm = _tile(m, (768, 512, 256))
    tk = _tile(k, WIDE_TILES)
    return _mm_call(
        _accumulate(_dot, k // tk), name, (m // tm, k // tk), (a, b),
        [pl.BlockSpec((tm, tk), lambda i, kk: (i, kk)), pl.BlockSpec((tk, n), lambda i, kk: (kk, 0))],
        jax.ShapeDtypeStruct((m, n), out_dtype), pl.BlockSpec((tm, n), lambda i, kk: (i, 0)), (tm, n), exch)


def mm_tn(a, b, out_dtype, name, exch=None):
    m, ka = a.shape
    n = b.shape[1]
    ta = _tile(ka, WIDE_TILES)
    tl = _tile(m, (768, 512, 256))
    return _mm_call(
        _accumulate(_dot_tn, m // tl), name, (ka // ta, m // tl), (a, b),
        [pl.BlockSpec((tl, ta), lambda i, ll: (ll, i)), pl.BlockSpec((tl, n), lambda i, ll: (ll, 0))],
        jax.ShapeDtypeStruct((ka, n), out_dtype), pl.BlockSpec((ta, n), lambda i, ll: (i, 0)), (ta, n), exch)


TM = 256


def _rb(arr, width=None, col_block=0):
    w = arr.shape[1] if width is None else width
    return pl.BlockSpec((TM, w), lambda i: (i, col_block))


def _whole(arr):
    return pl.BlockSpec(arr.shape, lambda i: (0,) * arr.ndim)


def _rows_call(body, name, lq, ins, in_specs, out_shapes, out_specs):
    return pl.pallas_call(
        body, name=name, grid=(lq // TM,), in_specs=in_specs, out_specs=out_specs, out_shape=out_shapes,
        compiler_params=_params("arbitrary"),
    )(*ins)


def rms_fwd(h, g, name):
    lq = h.shape[0]

    def body(h_ref, g_ref, o_ref):
        xhat, _ = _rms_hat(h_ref[...])
        o_ref[...] = (xhat * g_ref[...]).astype(BF16)

    return _rows_call(body, name, lq, (h, g), [_rb(h), _whole(g)],
                      jax.ShapeDtypeStruct((lq, D), BF16), _rb(h))


def rotary_fwd(proj, cos, sin):
    lq = proj.shape[0]
    half = RDK // 2

    def body(p_ref, c_ref, s_ref, q_ref, k_ref):
        c, s = c_ref[...], s_ref[...]
        for col, o_ref, scale in ((C_RQ, q_ref, RDK ** -0.5), (C_RK, k_ref, 1.0)):
            for h in range(RH):
                x1 = p_ref[:, col + h * RDK: col + h * RDK + half].astype(F32)
                x2 = p_ref[:, col + h * RDK + half: col + (h + 1) * RDK].astype(F32)
                o_ref[:, h * RDK: h * RDK + half] = ((x1 * c - x2 * s) * scale).astype(BF16)
                o_ref[:, h * RDK + half: (h + 1) * RDK] = ((x1 * s + x2 * c) * scale).astype(BF16)

    out = jax.ShapeDtypeStruct((lq, RH * RDK), BF16)
    return _rows_call(body, "rotary_fwd", lq, (proj, cos, sin),
                      [_rb(proj, 2 * RH * RDK, 0), _rb(cos), _rb(sin)],
                      (out, out), (pl.BlockSpec((TM, RH * RDK), lambda i: (i, 0)),) * 2)


def rotary_bwd(dq, dk, cos, sin):
    lq = dq.shape[0]
    half = RDK // 2

    def body(dq_ref, dk_ref, c_ref, s_ref, o_ref):
        c, s = c_ref[...], s_ref[...]
        for col, d_ref, scale in ((C_RQ, dq_ref, RDK ** -0.5), (C_RK, dk_ref, 1.0)):
            for h in range(RH):
                d1 = d_ref[:, h * RDK: h * RDK + half]
                d2 = d_ref[:, h * RDK + half: (h + 1) * RDK]
                o_ref[:, col + h * RDK: col + h * RDK + half] = ((d1 * c + d2 * s) * scale).astype(BF16)
                o_ref[:, col + h * RDK + half: col + (h + 1) * RDK] = ((d2 * c - d1 * s) * scale).astype(BF16)

    return _rows_call(body, "rotary_bwd", lq, (dq, dk, cos, sin), [_rb(dq), _rb(dk), _rb(cos), _rb(sin)],
                      jax.ShapeDtypeStruct((lq, 2 * RH * RDK), BF16),
                      pl.BlockSpec((TM, 2 * RH * RDK), lambda i: (i, 0)))


def ret_gate_fwd(proj, o_ret):
    lq = proj.shape[0]

    def body(p_ref, y_ref, o_ref):
        for h in range(RH):
            sl = slice(h * RDV, (h + 1) * RDV)
            yh, _ = _gn(y_ref[:, sl])
            rg = p_ref[:, sl].astype(F32)
            o_ref[:, sl] = (rg * _sigmoid(rg) * yh).astype(BF16)

    return _rows_call(body, "ret_gate_fwd", lq, (proj, o_ret),
                      [_rb(proj, RH * RDV, C_RG // (RH * RDV)), _rb(o_ret)],
                      jax.ShapeDtypeStruct((lq, RH * RDV), BF16), _rb(o_ret))


def ret_gate_bwd(proj, o_ret, d_gr):
    lq = proj.shape[0]

    def body(p_ref, y_ref, d_ref, drg_ref, dy_ref):
        for h in range(RH):
            sl = slice(h * RDV, (h + 1) * RDV)
            yh, rs = _gn(y_ref[:, sl])
            rg = p_ref[:, sl].astype(F32)
            sg = _sigmoid(rg)
            d = d_ref[:, sl].astype(F32)
            drg_ref[:, sl] = (d * yh * sg * (1.0 + rg * (1.0 - sg))).astype(BF16)
            dy_ref[:, sl] = _gn_bwd(yh, rs, d * rg * sg).astype(BF16)

    out = jax.ShapeDtypeStruct((lq, RH * RDV), BF16)
    return _rows_call(body, "ret_gate_bwd", lq, (proj, o_ret, d_gr),
                      [_rb(proj, RH * RDV, C_RG // (RH * RDV)), _rb(o_ret), _rb(d_gr)],
                      (out, out), (_rb(o_ret), _rb(o_ret)))


def merge_fwd(proj, y_ret, y_sb):
    lq = proj.shape[0]

    def body(ga_ref, gb_ref, yr_ref, ys_ref, o_ref):
        o_ref[...] = (_sigmoid(ga_ref[...].astype(F32)) * yr_ref[...]
                      + _sigmoid(gb_ref[...].astype(F32)) * ys_ref[...]).astype(BF16)

    return _rows_call(body, "merge_fwd", lq, (proj, proj, y_ret, y_sb),
                      [_rb(proj, D, C_GA // D), _rb(proj, D, C_GB // D), _rb(y_ret), _rb(y_sb)],
                      jax.ShapeDtypeStruct((lq, D), BF16), _rb(y_ret))


def merge_bwd(proj, y_ret, y_sb, d_merged):
    lq = proj.shape[0]

    def body(ga_ref, gb_ref, yr_ref, ys_ref, d_ref, dyr_ref, dys_ref, dg_ref):
        d = d_ref[...].astype(F32)
        sa = _sigmoid(ga_ref[...].astype(F32))
        sb = _sigmoid(gb_ref[...].astype(F32))
        dyr_ref[...] = (d * sa).astype(BF16)
        dys_ref[...] = (d * sb).astype(BF16)
        dg_ref[:, :D] = (d * yr_ref[...] * sa * (1.0 - sa)).astype(BF16)
        dg_ref[:, D:] = (d * ys_ref[...] * sb * (1.0 - sb)).astype(BF16)

    o1 = jax.ShapeDtypeStruct((lq, D), BF16)
    return _rows_call(body, "merge_bwd", lq, (proj, proj, y_ret, y_sb, d_merged),
                      [_rb(proj, D, C_GA // D), _rb(proj, D, C_GB // D), _rb(y_ret), _rb(y_sb), _rb(d_merged)],
                      (o1, o1, jax.ShapeDtypeStruct((lq, 2 * D), BF16)),
                      (_rb(y_ret), _rb(y_ret), pl.BlockSpec((TM, 2 * D), lambda i: (i, 0))))


def post_mix_fwd(hp, mix, g_post, g_pre):
    lq = hp.shape[0]

    def body(h_ref, m_ref, g2_ref, g3_ref, h1_ref, hn_ref):
        mhat, _ = _rms_hat(m_ref[...])
        h1 = h_ref[...] + mhat * g2_ref[...]
        h1_ref[...] = h1
        hhat, _ = _rms_hat(h1)
        hn_ref[...] = (hhat * g3_ref[...]).astype(BF16)

    return _rows_call(body, "post_mix_fwd", lq, (hp, mix, g_post, g_pre),
                      [_rb(hp), _rb(mix), _whole(g_post), _whole(g_pre)],
                      (jax.ShapeDtypeStruct((lq, D), F32), jax.ShapeDtypeStruct((lq, D), BF16)),
                      (_rb(hp), _rb(hp)))


def swiglu_fwd(ab):
    lq = ab.shape[0]

    def body(a_ref, b_ref, o_ref):
        a = a_ref[...].astype(F32)
        o_ref[...] = (a * _sigmoid(a) * b_ref[...].astype(F32)).astype(BF16)

    return _rows_call(body, "swiglu_fwd", lq, (ab, ab), [_rb(ab, DFF, 0), _rb(ab, DFF, 1)],
                      jax.ShapeDtypeStruct((lq, DFF), BF16), pl.BlockSpec((TM, DFF), lambda i: (i, 0)))


def swiglu_bwd(ab, d_act):
    lq = ab.shape[0]

    def body(a_ref, b_ref, d_ref, o_ref):
        a = a_ref[...].astype(F32)
        b = b_ref[...].astype(F32)
        d = d_ref[...].astype(F32)
        sg = _sigmoid(a)
        o_ref[:, :DFF] = (d * b * sg * (1.0 + a * (1.0 - sg))).astype(BF16)
        o_ref[:, DFF:] = (d * a * sg).astype(BF16)

    return _rows_call(body, "swiglu_bwd", lq, (ab, ab, d_act), [_rb(ab, DFF, 0), _rb(ab, DFF, 1), _rb(d_act)],
                      jax.ShapeDtypeStruct((lq, 2 * DFF), BF16), pl.BlockSpec((TM, 2 * DFF), lambda i: (i, 0)))


def loss_head(h1, ff, g_post, target):
    lq = h1.shape[0]
    front_blocks = FRONT // TM

    def body(h_ref, f_ref, g_ref, t_ref, loss_ref, dh_ref, df_ref, dg_ref):
        i = pl.program_id(0)

        @pl.when(i == 0)
        def _():
            loss_ref[...] = jnp.zeros_like(loss_ref)
            dg_ref[...] = jnp.zeros_like(dg_ref)

        g = g_ref[...]
        fhat, r = _rms_hat(f_ref[...])
        is_x = (i >= front_blocks).astype(F32)
        diff = (h_ref[...] + fhat * g - t_ref[...]) * is_x
        loss_ref[...] += 0.5 * jnp.sum(diff * diff) / D
        dy = diff / D
        dh_ref[...] = dy
        df_ref[...] = _rms_bwd(fhat, r, g, dy).astype(BF16)
        dg_ref[...] += jnp.sum(dy * fhat, axis=0, keepdims=True)

    return _rows_call(
        body, "loss_head", lq, (h1, ff, g_post, target),
        [_rb(h1), _rb(ff), _whole(g_post),
         pl.BlockSpec((TM, D), lambda i: (jnp.maximum(i - front_blocks, 0), 0))],
        (jax.ShapeDtypeStruct((8, 128), F32), jax.ShapeDtypeStruct((lq, D), F32),
         jax.ShapeDtypeStruct((lq, D), BF16), jax.ShapeDtypeStruct((GAIN_ROWS, D), F32)),
        (pl.BlockSpec((8, 128), lambda i: (0, 0)), _rb(h1), _rb(h1), pl.BlockSpec((GAIN_ROWS, D), lambda i: (0, 0))))


def post_mix_bwd(h1, d_hn2, g_pre, d_h2, mix, g_post):
    lq = h1.shape[0]

    def body(h_ref, dn_ref, g3_ref, dh2_ref, m_ref, g2_ref, dh1_ref, dm_ref, dg3_ref, dg2_ref):
        i = pl.program_id(0)

        @pl.when(i == 0)
        def _():
            dg3_ref[...] = jnp.zeros_like(dg3_ref)
            dg2_ref[...] = jnp.zeros_like(dg2_ref)

        hhat, r = _rms_hat(h_ref[...])
        dn = dn_ref[...]
        d_h1 = dh2_ref[...] + _rms_bwd(hhat, r, g3_ref[...], dn)
        dh1_ref[...] = d_h1
        dg3_ref[...] += jnp.sum(dn * hhat, axis=0, keepdims=True)
        mhat, rm = _rms_hat(m_ref[...])
        dm_ref[...] = _rms_bwd(mhat, rm, g2_ref[...], d_h1).astype(BF16)
        dg2_ref[...] += jnp.sum(d_h1 * mhat, axis=0, keepdims=True)

    vec = jax.ShapeDtypeStruct((GAIN_ROWS, D), F32)
    vspec = pl.BlockSpec((GAIN_ROWS, D), lambda i: (0, 0))
    return _rows_call(body, "post_mix_bwd", lq, (h1, d_hn2, g_pre, d_h2, mix, g_post),
                      [_rb(h1), _rb(d_hn2), _whole(g_pre), _rb(d_h2), _rb(mix), _whole(g_post)],
                      (jax.ShapeDtypeStruct((lq, D), F32), jax.ShapeDtypeStruct((lq, D), BF16), vec, vec),
                      (_rb(h1), _rb(h1), vspec, vspec))


def pre_mix_bwd(hp, d_hn1, g_pre, d_h1):
    lq = hp.shape[0]

    def body(h_ref, dn_ref, g_ref, dh1_ref, dhp_ref, dg_ref):
        i = pl.program_id(0)

        @pl.when(i == 0)
        def _():
            dg_ref[...] = jnp.zeros_like(dg_ref)

        hhat, r = _rms_hat(h_ref[...])
        dn = dn_ref[...]
        dhp_ref[...] = dh1_ref[...] + _rms_bwd(hhat, r, g_ref[...], dn)
        dg_ref[...] += jnp.sum(dn * hhat, axis=0, keepdims=True)

    return _rows_call(body, "pre_mix_bwd", lq, (hp, d_hn1, g_pre, d_h1),
                      [_rb(hp), _rb(d_hn1), _whole(g_pre), _rb(d_h1)],
                      (jax.ShapeDtypeStruct((lq, D), F32), jax.ShapeDtypeStruct((GAIN_ROWS, D), F32)),
                      (_rb(hp), pl.BlockSpec((GAIN_ROWS, D), lambda i: (0, 0))))


def _retention_tables():
    log_g = jnp.log1p(-(2.0 ** (-5.0 - jnp.arange(RH, dtype=F32))))
    idx = jnp.arange(CHUNK, dtype=F32)
    diff = idx[:, None] - idx[None, :]
    decay = jnp.where(diff >= 0, jnp.exp(log_g[:, None, None] * jnp.maximum(diff, 0.0)), 0.0)
    zeta = jnp.exp(log_g[:, None] * (CHUNK - 1.0 - idx))
    xi = jnp.exp(log_g[:, None] * (idx + 1.0))
    g_chunk = jnp.broadcast_to(jnp.exp(log_g * CHUNK)[:, None], (RH, CHUNK))
    coef = jnp.stack([xi, zeta, g_chunk] + [jnp.zeros_like(xi)] * 125, axis=-1)
    return decay, coef


def retention_fwd(qr, kr, proj, decay, coef):
    lq = qr.shape[0]
    n = lq // CHUNK

    def body(q_ref, k_ref, v_ref, dec_ref, cf_ref, o_ref, st_ref, state):
        @pl.when(pl.program_id(0) == 0)
        def _():
            state[...] = jnp.zeros_like(state)

        for h in range(RH):
            qk, vv = slice(h * RDK, (h + 1) * RDK), slice(h * RDV, (h + 1) * RDV)
            q, k, v = q_ref[:, qk], k_ref[:, qk], v_ref[:, vv]
            xi, zeta, gch = cf_ref[h, :, 0:1], cf_ref[h, :, 1:2], cf_ref[h, 0:1, 2:3]
            st = state[h]
            stb = st.astype(BF16)
            st_ref[h] = stb
            s = _dot_nt(q, k) * dec_ref[h]
            o_ref[:, vv] = _dot(s.astype(BF16), v) + _dot(q, stb) * xi
            vz = (v.astype(F32) * zeta).astype(BF16)
            state[h] = gch * st + _dot_tn(k, vz)

    return pl.pallas_call(
        body, name="retention_fwd", grid=(n,),
        in_specs=[pl.BlockSpec((CHUNK, RH * RDK), lambda c: (c, 0)), pl.BlockSpec((CHUNK, RH * RDK), lambda c: (c, 0)),
                  pl.BlockSpec((CHUNK, RH * RDV), lambda c: (c, C_RV // (RH * RDV))),
                  pl.BlockSpec((RH, CHUNK, CHUNK), lambda c: (0, 0, 0)),
                  pl.BlockSpec((RH, CHUNK, 128), lambda c: (0, 0, 0))],
        out_specs=(pl.BlockSpec((CHUNK, RH * RDV), lambda c: (c, 0)),
                   pl.BlockSpec((None, RH, RDK, RDV), lambda c: (c, 0, 0, 0))),
        out_shape=(jax.ShapeDtypeStruct((lq, RH * RDV), F32), jax.ShapeDtypeStruct((n, RH, RDK, RDV), BF16)),
        scratch_shapes=[pltpu.VMEM((RH, RDK, RDV), F32)],
        compiler_params=_params("arbitrary"),
    )(qr, kr, proj, decay, coef)


def retention_bwd(qr, kr, proj, d_o, states, decay, coef):
    lq = qr.shape[0]
    n = lq // CHUNK

    def body(q_ref, k_ref, v_ref, do_ref, st_ref, dec_ref, cf_ref, dq_ref, dk_ref, dv_ref, dstate):
        @pl.when(pl.program_id(0) == 0)
        def _():
            dstate[...] = jnp.zeros_like(dstate)

        for h in range(RH):
            qk, vv = slice(h * RDK, (h + 1) * RDK), slice(h * RDV, (h + 1) * RDV)
            q, k, v, dob = q_ref[:, qk], k_ref[:, qk], v_ref[:, vv], do_ref[:, vv]
            xi, zeta, gch = cf_ref[h, :, 0:1], cf_ref[h, :, 1:2], cf_ref[h, 0:1, 2:3]
            dec = dec_ref[h]
            dsn = dstate[h]
            dsnb = dsn.astype(BF16)
            dox = (dob.astype(F32) * xi).astype(BF16)
            sb = (_dot_nt(q, k) * dec).astype(BF16)
            dsb = (_dot_nt(dob, v) * dec).astype(BF16)
            vz = (v.astype(F32) * zeta).astype(BF16)
            dq_ref[:, qk] = _dot(dsb, k) + _dot_nt(dox, st_ref[h])
            dk_ref[:, qk] = _dot_tn(dsb, q) + _dot_nt(vz, dsnb)
            dv_ref[:, vv] = (_dot_tn(sb, dob) + _dot(k, dsnb) * zeta).astype(BF16)
            dstate[h] = gch * dsn + _dot_tn(q, dox)

    rev = lambda c: n - 1 - c
    qk_spec = pl.BlockSpec((CHUNK, RH * RDK), lambda c: (rev(c), 0))
    v_spec = pl.BlockSpec((CHUNK, RH * RDV), lambda c: (rev(c), 0))
    return pl.pallas_call(
        body, name="retention_bwd", grid=(n,),
        in_specs=[qk_spec, qk_spec,
                  pl.BlockSpec((CHUNK, RH * RDV), lambda c: (rev(c), C_RV // (RH * RDV))), v_spec,
                  pl.BlockSpec((None, RH, RDK, RDV), lambda c: (rev(c), 0, 0, 0)),
                  pl.BlockSpec((RH, CHUNK, CHUNK), lambda c: (0, 0, 0)),
                  pl.BlockSpec((RH, CHUNK, 128), lambda c: (0, 0, 0))],
        out_specs=(qk_spec, qk_spec, v_spec),
        out_shape=(jax.ShapeDtypeStruct((lq, RH * RDK), F32), jax.ShapeDtypeStruct((lq, RH * RDK), F32),
                   jax.ShapeDtypeStruct((lq, RH * RDV), BF16)),
        scratch_shapes=[pltpu.VMEM((RH, RDK, RDV), F32)],
        compiler_params=_params("arbitrary"),
    )(qr, kr, proj, d_o, states, decay, coef)


def _sb_masks_and_u():
    lane = lax.broadcasted_iota(jnp.int32, (1, 2 * SB_DH), 1)
    lo = lane < SB_DH
    row = lax.broadcasted_iota(jnp.int32, (SB_T, SB_T), 0)
    col = lax.broadcasted_iota(jnp.int32, (SB_T, SB_T), 1)
    u = (row > col).astype(BF16)
    return lo, u, row, col


def _sb_rows(j):
    start = j * SB_T
    return pl.ds(start if isinstance(j, int) else pl.multiple_of(start, SB_T), SB_T)


SB_DEAD = -104.0


def _sb_alive(logs):
    m = logs[0]
    for l in logs[1:]:
        m = jnp.maximum(m, l)
    return (jnp.max(m) > SB_DEAD).astype(jnp.int32)


def _sb_walk(i, run, logs_of, step):
    def cond(c):
        return (c[0] >= 1) & (c[1] > 0)

    def body(c):
        r = step(c[0], c[2])
        return c[0] - 1, _sb_alive(logs_of(r)), r

    return lax.while_loop(cond, body, (i - 1, _sb_alive(logs_of(run)), run))


def _sb_logs(z):
    ls = jnp.minimum(z, 0.0) - jnp.log(1.0 + jnp.exp(-jnp.abs(z)))
    return ls, ls - z


def sb_fwd(proj):
    lq = proj.shape[0]
    t = SB_T

    def body(q_ref, k_ref, v_ref, o_ref, of_ref, acc_ref):
        i = pl.program_id(1)
        lo, u, row, col = _sb_masks_and_u()
        qs = (q_ref[...].astype(F32) * SB_DH ** -0.5).astype(BF16)
        zero = jnp.zeros_like(qs)
        qh = (jnp.where(lo, qs, zero), jnp.where(lo, zero, qs))
        acc_ref[...] = jnp.zeros_like(acc_ref)

        def block(j, run, masked):
            rows = _sb_rows(j)
            ks, vs = k_ref[rows, :], v_ref[rows, :]
            if masked:
                valid = (col + j * t < row + i * t) & (col + j * t >= META0)
            out = []
            for hh in range(2):
                ls, ln = _sb_logs(_dot_nt(qh[hh], ks))
                if masked:
                    ln = jnp.where(valid, ln, 0.0)
                a = jnp.exp(ls + _dot(ln.astype(BF16), u) + run[hh])
                if masked:
                    a = jnp.where(valid, a, 0.0)
                acc_ref[hh] += _dot(a.astype(BF16), vs)
                out.append(run[hh] + jnp.sum(ln, axis=1, keepdims=True))
            return tuple(out)

        zeros = jnp.zeros((t, 1), F32)
        run = block(i, (zeros, zeros), True)
        _, go, run = _sb_walk(i, run, lambda r: r, lambda j, r: block(j, r, False))

        @pl.when((i > 0) & (go > 0))
        def _():
            block(0, run, True)

        o = jnp.where(lo, acc_ref[0], acc_ref[1])
        o_ref[...] = o.astype(BF16)
        of_ref[...] = o

    blk = pl.BlockSpec((t, 128), lambda p, i: (i, p))
    return pl.pallas_call(
        body, name="sb_fwd", grid=(D // 128, lq // t),
        in_specs=[pl.BlockSpec((t, 128), lambda p, i: (i, C_SQ // 128 + p)),
                  pl.BlockSpec((lq, 128), lambda p, i: (0, C_SK // 128 + p)),
                  pl.BlockSpec((lq, 128), lambda p, i: (0, C_SV // 128 + p))],
        out_specs=(blk, blk),
        out_shape=(jax.ShapeDtypeStruct((lq, D), BF16), jax.ShapeDtypeStruct((lq, D), F32)),
        scratch_shapes=[pltpu.VMEM((2, t, 128), F32)],
        compiler_params=_params("parallel", "arbitrary"),
    )(proj, proj, proj)


def sb_bwd(proj, o, d_o):
    lq = proj.shape[0]
    t = SB_T

    def body(q_ref, k_ref, v_ref, o_ref, do_ref, dq_ref, dk_ref, dv_ref, acc_ref):
        i = pl.program_id(1)

        @pl.when(i == 0)
        def _():
            dk_ref[...] = jnp.zeros_like(dk_ref)
            dv_ref[...] = jnp.zeros_like(dv_ref)

        lo, u, row, col = _sb_masks_and_u()
        qs = (q_ref[...].astype(F32) * SB_DH ** -0.5).astype(BF16)
        do = do_ref[...]
        zero = jnp.zeros_like(qs)
        qh = (jnp.where(lo, qs, zero), jnp.where(lo, zero, qs))
        doh = (jnp.where(lo, do, zero), jnp.where(lo, zero, do))
        prod = o_ref[...] * do.astype(F32)
        dsum = (jnp.sum(jnp.where(lo, prod, 0.0), axis=1, keepdims=True),
                jnp.sum(jnp.where(lo, 0.0, prod), axis=1, keepdims=True))
        acc_ref[...] = jnp.zeros_like(acc_ref)

        def block(j, run, masked):
            rows = _sb_rows(j)
            ks, vs = k_ref[rows, :], v_ref[rows, :]
            if masked:
                valid = (col + j * t < row + i * t) & (col + j * t >= META0)
            out = []
            for hh in range(2):
                run_ln, run_e = run[2 * hh], run[2 * hh + 1]
                ls, ln = _sb_logs(_dot_nt(qh[hh], ks))
                if masked:
                    ln = jnp.where(valid, ln, 0.0)
                a = jnp.exp(ls + _dot(ln.astype(BF16), u) + run_ln)
                if masked:
                    a = jnp.where(valid, a, 0.0)
                ab = a.astype(BF16)
                e = ab.astype(F32) * _dot_nt(doh[hh], vs)
                e_hi = e.astype(BF16)
                e_lo = (e - e_hi.astype(F32)).astype(BF16)
                dz = e - jnp.exp(ls) * (dsum[hh] - run_e - (_dot(e_hi, u) + _dot(e_lo, u)))
                if masked:
                    dz = jnp.where(valid, dz, 0.0)
                dzb = dz.astype(BF16)
                acc_ref[hh] += _dot(dzb, ks)
                dk_ref[rows, :] += _dot_tn(dzb, qh[hh])
                dv_ref[rows, :] += _dot_tn(ab, doh[hh])
                out += [run_ln + jnp.sum(ln, axis=1, keepdims=True), run_e + jnp.sum(e, axis=1, keepdims=True)]
            return tuple(out)

        zeros = jnp.zeros((t, 1), F32)
        run = block(i, (zeros,) * 4, True)
        _, go, run = _sb_walk(i, run, lambda r: (r[0], r[2]), lambda j, r: block(j, r, False))

        @pl.when((i > 0) & (go > 0))
        def _():
            block(0, run, True)

        dq_ref[...] = (jnp.where(lo, acc_ref[0], acc_ref[1]) * SB_DH ** -0.5).astype(BF16)

    blk = pl.BlockSpec((t, 128), lambda p, i: (i, p))
    col_blk = pl.BlockSpec((lq, 128), lambda p, i: (0, p))
    return pl.pallas_call(
        body, name="sb_bwd", grid=(D // 128, lq // t),
        in_specs=[pl.BlockSpec((t, 128), lambda p, i: (i, C_SQ // 128 + p)),
                  pl.BlockSpec((lq, 128), lambda p, i: (0, C_SK // 128 + p)),
                  pl.BlockSpec((lq, 128), lambda p, i: (0, C_SV // 128 + p)), blk, blk],
        out_specs=(blk, col_blk, col_blk),
        out_shape=(jax.ShapeDtypeStruct((lq, D), BF16), jax.ShapeDtypeStruct((lq, D), F32),
                   jax.ShapeDtypeStruct((lq, D), F32)),
        scratch_shapes=[pltpu.VMEM((2, t, 128), F32)],
        compiler_params=_params("parallel", "arbitrary"),
    )(proj, proj, proj, o, d_o)


def all_gather(blocks):
    nb = len(blocks)

    def body(*refs):
        x_refs, out_refs = refs[:nb], refs[nb:2 * nb]
        send_sems, recv_sems, local_sems = refs[2 * nb:]
        x, y, c = lax.axis_index("x"), lax.axis_index("y"), lax.axis_index("c")
        me, sibling = (x, y, c), (x, y, 1 - c)
        chips = [(1 - x, y), (x, 1 - y), (1 - x, 1 - y)]

        def copy(b, k, block, to, src=None):
            slot = out_refs[b].at[_device_index(*block)]
            return pltpu.make_async_remote_copy(
                src_ref=slot if src is None else src, dst_ref=slot,
                send_sem=send_sems.at[b, k], recv_sem=recv_sems.at[b, k],
                device_id=to, device_id_type=pl.DeviceIdType.MESH)

        mine = [pltpu.make_async_copy(x_refs[b], out_refs[b].at[_device_index(*me)], local_sems.at[b])
                for b in range(nb)]
        for cp in mine:
            cp.start()
        first = []
        for b in range(nb):
            first.append(copy(b, 0, me, sibling, src=x_refs[b]))
            first += [copy(b, 1 + j, me, (*chip, c), src=x_refs[b]) for j, chip in enumerate(chips)]
        for cp in first:
            cp.start()
        passed = []
        for j, chip in enumerate(chips):
            for b in range(nb):
                copy(b, 1 + j, (*chip, c), me).wait_recv()
                cp = copy(b, 4 + j, (*chip, c), sibling)
                cp.start()
                passed.append(cp)
        for b in range(nb):
            copy(b, 0, sibling, me).wait_recv()
            for j, chip in enumerate(chips):
                copy(b, 4 + j, (*chip, 1 - c), me).wait_recv()
        for cp in first + passed:
            cp.wait_send()
        for cp in mine:
            cp.wait()

    any_spec = pl.BlockSpec(memory_space=pl.ANY)
    return pl.pallas_call(
        body, name="all_gather",
        in_specs=[any_spec] * nb, out_specs=[any_spec] * nb,
        out_shape=[jax.ShapeDtypeStruct((NDEV,) + b.shape, b.dtype) for b in blocks],
        scratch_shapes=[pltpu.SemaphoreType.DMA((nb, 7)), pltpu.SemaphoreType.DMA((nb, 7)),
                        pltpu.SemaphoreType.DMA((nb,))],
    )(*blocks)


def gather_small(small):
    def body(s_ref, land_ref, send_sems, recv_sems, local_sem):
        start, wait = _direct_exchange(lambda d: s_ref, land_ref, send_sems, recv_sems, local_sem)
        start()
        wait()

    any_spec = pl.BlockSpec(memory_space=pl.ANY)
    return pl.pallas_call(
        body, name="gather_small", in_specs=[any_spec], out_specs=any_spec,
        out_shape=jax.ShapeDtypeStruct((NDEV,) + small.shape, small.dtype),
        scratch_shapes=[pltpu.SemaphoreType.DMA((NDEV - 1,)), pltpu.SemaphoreType.DMA((NDEV - 1,)),
                        pltpu.SemaphoreType.DMA(())],
    )(small)


def sum_slots(landed, name):
    _, r, c = landed.shape
    tr = _tile(r, (352, 224, 8))

    def body(l_ref, o_ref):
        acc = l_ref[0].astype(F32)
        for p in range(1, NDEV):
            acc = acc + l_ref[p].astype(F32)
        o_ref[...] = acc

    return pl.pallas_call(
        body, name=name, grid=(r // tr,),
        in_specs=[pl.BlockSpec((NDEV, tr, c), lambda i: (0, i, 0))],
        out_specs=pl.BlockSpec((tr, c), lambda i: (i, 0)),
        out_shape=jax.ShapeDtypeStruct((r, c), F32),
        compiler_params=_params("parallel"),
    )(landed)


def adamw(w, g, m, v, name):
    r, c = w.shape
    tr = _tile(r, (256, 128))

    def body(w_ref, g_ref, m_ref, v_ref, d_ref, nm_ref, nv_ref):
        g_ = g_ref[...]
        m_ = ADAM_B1 * m_ref[...] + (1.0 - ADAM_B1) * g_
        v_ = ADAM_B2 * v_ref[...] + (1.0 - ADAM_B2) * jnp.square(g_)
        m_hat = m_ / (1.0 - ADAM_B1 ** ADAM_STEP)
        v_hat = v_ / (1.0 - ADAM_B2 ** ADAM_STEP)
        d_ref[...] = -ADAM_LR * (m_hat / (jnp.sqrt(v_hat) + ADAM_EPS) + ADAM_WD * w_ref[...])
        nm_ref[...] = m_
        nv_ref[...] = v_

    spec = pl.BlockSpec((tr, c), lambda i: (i, 0))
    out = jax.ShapeDtypeStruct((r, c), F32)
    return pl.pallas_call(
        body, name=name, grid=(r // tr,), in_specs=[spec] * 4, out_specs=(spec,) * 3, out_shape=(out,) * 3,
        compiler_params=_params("parallel"),
    )(w, g, m, v)


def kernel(x, meta_tokens, w_in, w_ret_out, w_sb_out, w_out, w_ffn_in, w_ffn_out, norm_mix_pre, norm_mix_post, norm_ffn_pre, norm_ffn_post, loss_target, m_meta_tokens, m_w_in, m_w_ret_out, m_w_sb_out, m_w_out, m_w_ffn_in, m_w_ffn_out, m_norm_mix_pre, m_norm_mix_post, m_norm_ffn_pre, m_norm_ffn_post, v_meta_tokens, v_w_in, v_w_ret_out, v_w_sb_out, v_w_out, v_w_ffn_in, v_w_ffn_out, v_norm_mix_pre, v_norm_mix_post, v_norm_ffn_pre, v_norm_ffn_post):
    seq = x.shape[1]
    lq = seq + FRONT
    me = _device_index(lax.axis_index("x"), lax.axis_index("y"), lax.axis_index("c"))

    shards = {"w_in_t": w_in[0].T, "w_ffn_in_t": w_ffn_in[0].T, "w_ret_out": w_ret_out[0],
              "w_sb_out": w_sb_out[0], "w_out": w_out[0], "w_ffn_out": w_ffn_out[0]}
    rest_rows = PACK_ROWS[1:]
    pack_rest = jnp.concatenate([shards[n].astype(BF16) for n, _ in rest_rows], axis=0)
    gathered_in, meta_all = all_gather([shards["w_in_t"].astype(BF16), meta_tokens])
    full = {"w_in_t": gathered_in.reshape(PROJ, D)}
    meta_full = meta_all.transpose(1, 0, 2).reshape(N_META, D)

    pos = jnp.arange(lq, dtype=F32) - META0
    half = RDK // 2
    ang = pos[:, None] * (ROPE_BASE ** (-jnp.arange(half, dtype=F32) / half))[None, :]
    cos, sin = jnp.cos(ang), jnp.sin(ang)
    decay, coef = _retention_tables()

    hp = jnp.concatenate([jnp.zeros((META0, D), F32), meta_full, x[0]], axis=0)
    hn1 = rms_fwd(hp, norm_mix_pre, "rms_mix_pre")
    proj, gathered_rest = mm_nt(hn1, full["w_in_t"], BF16, "proj", exch=(pack_rest, "gather"))
    off = 0
    for n, r in rest_rows:
        full[n] = gathered_rest[:, off:off + r, :].reshape(NDEV * r, D)
        off += r
    qr, kr = rotary_fwd(proj, cos, sin)
    o_ret, states = retention_fwd(qr, kr, proj, decay, coef)
    gr = ret_gate_fwd(proj, o_ret)
    o_sb, o_sb_f32 = sb_fwd(proj)
    y_ret = mm_nn(gr, full["w_ret_out"], F32, "y_ret")
    y_sb = mm_nn(o_sb, full["w_sb_out"], F32, "y_sb")
    merged = merge_fwd(proj, y_ret, y_sb)
    mix = mm_nn(merged, full["w_out"], F32, "mix")
    h1, hn2 = post_mix_fwd(hp, mix, norm_mix_post, norm_ffn_pre)
    ab = mm_nt(hn2, full["w_ffn_in_t"], BF16, "ffn_in")
    act = swiglu_fwd(ab)
    ff = mm_nn(act, full["w_ffn_out"], F32, "ffn_out")
    loss_blk, d_h2, d_ff, dg_ffn_post = loss_head(h1, ff, norm_ffn_post, loss_target[0])
    loss = lax.psum(loss_blk[0, 0], ("x", "y", "c"))

    grads = {}
    d_act = mm_nt(d_ff, full["w_ffn_out"], BF16, "d_act")
    grads["w_ffn_out"] = mm_tn(act, d_ff, BF16, "dw_ffn_out")
    d_ab = swiglu_bwd(ab, d_act)
    d_hn2 = mm_nn(d_ab, full["w_ffn_in_t"], F32, "d_hn2")
    grads["w_ffn_in_t"] = mm_tn(d_ab, hn2, BF16, "dw_ffn_in")
    d_h1, d_mix, dg_ffn_pre, dg_mix_post = post_mix_bwd(h1, d_hn2, norm_ffn_pre, d_h2, mix, norm_mix_post)
    d_merged = mm_nt(d_mix, full["w_out"], BF16, "d_merged")
    grads["w_out"] = mm_tn(merged, d_mix, BF16, "dw_out")
    d_y_ret, d_y_sb, d_gates = merge_bwd(proj, y_ret, y_sb, d_merged)
    d_gr = mm_nt(d_y_ret, full["w_ret_out"], BF16, "d_gr")
    grads["w_ret_out"] = mm_tn(gr, d_y_ret, BF16, "dw_ret_out")
    d_o_sb = mm_nt(d_y_sb, full["w_sb_out"], BF16, "d_o_sb")
    grads["w_sb_out"] = mm_tn(o_sb, d_y_sb, BF16, "dw_sb_out")
    d_rg, d_o_ret = ret_gate_bwd(proj, o_ret, d_gr)
    d_qr, d_kr, d_rv = retention_bwd(qr, kr, proj, d_o_ret, states, decay, coef)
    d_rqk = rotary_bwd(d_qr, d_kr, cos, sin)
    d_sq, d_sk, d_sv = sb_bwd(proj, o_sb_f32, d_o_sb)
    d_proj = jnp.concatenate([d_rqk, d_rv, d_rg, d_sq, d_sk.astype(BF16), d_sv.astype(BF16), d_gates], axis=1)
    parts_rest = jnp.concatenate([grads[n].reshape(NDEV, r, D) for n, r in rest_rows], axis=1)
    dw_in_t, landed_rest = mm_tn(d_proj, hn1, BF16, "dw_in", exch=(parts_rest, "scatter"))
    d_hn1, landed_in = mm_nn(d_proj, full["w_in_t"], F32, "d_hn1",
                             exch=(dw_in_t.reshape(NDEV, PROJ // NDEV, D), "scatter"))
    d_hp, dg_mix_pre = pre_mix_bwd(hp, d_hn1, norm_mix_pre, d_h1)
    grad_x = d_hp[FRONT:][None]
    small = jnp.concatenate([dg_mix_pre, dg_mix_post, dg_ffn_pre, dg_ffn_post, d_hp[META0:FRONT]], axis=0)
    ssum = sum_slots(gather_small(small), "sum_small")
    g = {"w_in_t": sum_slots(landed_in, "sum_grads_in")}
    gsum = sum_slots(landed_rest, "sum_grads_rest")
    off = 0
    for n, r in rest_rows:
        g[n] = gsum[off:off + r]
        off += r
    gain = lambda k: ssum[k * GAIN_ROWS:k * GAIN_ROWS + 1]
    g_w = {"meta_tokens": lax.dynamic_slice(ssum[4 * GAIN_ROWS:], (0, me * (D // NDEV)), (N_META, D // NDEV)),
           "w_in": g["w_in_t"].T[None], "w_ret_out": g["w_ret_out"][None], "w_sb_out": g["w_sb_out"][None],
           "w_out": g["w_out"][None], "w_ffn_in": g["w_ffn_in_t"].T[None], "w_ffn_out": g["w_ffn_out"][None],
           "norm_mix_pre": gain(0), "norm_mix_post": gain(1), "norm_ffn_pre": gain(2), "norm_ffn_post": gain(3)}

    names = ["meta_tokens", "w_in", "w_ret_out", "w_sb_out", "w_out", "w_ffn_in", "w_ffn_out",
             "norm_mix_pre", "norm_mix_post", "norm_ffn_pre", "norm_ffn_post"]
    w_of = dict(zip(names, (meta_tokens, w_in, w_ret_out, w_sb_out, w_out, w_ffn_in, w_ffn_out,
                            norm_mix_pre, norm_mix_post, norm_ffn_pre, norm_ffn_post)))
    m_of = dict(zip(names, (m_meta_tokens, m_w_in, m_w_ret_out, m_w_sb_out, m_w_out, m_w_ffn_in, m_w_ffn_out,
                            m_norm_mix_pre, m_norm_mix_post, m_norm_ffn_pre, m_norm_ffn_post)))
    v_of = dict(zip(names, (v_meta_tokens, v_w_in, v_w_ret_out, v_w_sb_out, v_w_out, v_w_ffn_in, v_w_ffn_out,
                            v_norm_mix_pre, v_norm_mix_post, v_norm_ffn_pre, v_norm_ffn_post)))
    delta, new_m, new_v = {}, {}, {}
    for n in names:
        shape = w_of[n].shape
        two_d = (shape[-2], shape[-1])
        d_, m_, v_ = adamw(w_of[n].reshape(two_d), g_w[n].reshape(two_d), m_of[n].reshape(two_d),
                           v_of[n].reshape(two_d), "adamw_" + n)
        delta[n], new_m[n], new_v[n] = d_.reshape(shape), m_.reshape(shape), v_.reshape(shape)

    return (loss, grad_x, *[g_w[n] for n in names], *[delta[n] for n in names],
            *[new_m[n] for n in names], *[new_v[n] for n in names])
```

```python
import jax
import jax.numpy as jnp
from jax import lax
from jax.experimental import pallas as pl
from jax.experimental.pallas import tpu as pltpu

F32 = jnp.float32
BF16 = jnp.bfloat16

D = 1024
N_META = 16
CHUNK = 128
FRONT = 256
META0 = FRONT - N_META
RH, RDK, RDV = 4, 256, 512
SB_DH = 64
DFF = 2816
NDEV = 8
ROPE_BASE = 10000.0
NORM_EPS = 1e-6
GN_EPS = 1e-5
C_RQ, C_RK, C_RV, C_RG, C_SQ, C_SK, C_SV, C_GA, C_GB = 0, 1024, 2048, 4096, 6144, 7168, 8192, 9216, 10240
PROJ = 11264
PACK_ROWS = (("w_in_t", PROJ // NDEV), ("w_ffn_in_t", 2 * DFF // NDEV), ("w_ret_out", RH * RDV // NDEV),
             ("w_sb_out", D // NDEV), ("w_out", D // NDEV), ("w_ffn_out", DFF // NDEV))
PACK = sum(r for _, r in PACK_ROWS)
GAIN_ROWS = 8

ADAM_LR = 0.001
ADAM_B1 = 0.9
ADAM_B2 = 0.999
ADAM_EPS = 1e-08
ADAM_WD = 0.01
ADAM_STEP = 10

VMEM_LIMIT = 56 * 1024 * 1024
SB_T = 256

NT = (((1,), (1,)), ((), ()))
TN = (((0,), (0,)), ((), ()))


def _dot(a, b):
    return jnp.dot(a, b, preferred_element_type=F32)


def _dot_nt(a, b):
    return lax.dot_general(a, b, NT, preferred_element_type=F32)


def _dot_tn(a, b):
    return lax.dot_general(a, b, TN, preferred_element_type=F32)


WIDE_TILES = (1024, 1408, 512, 256)


def _tile(n, candidates):
    for t in candidates:
        if n % t == 0:
            return t
    return n


def _params(*sem):
    return pltpu.CompilerParams(dimension_semantics=sem, vmem_limit_bytes=VMEM_LIMIT)


def _rms_hat(x):
    r = lax.rsqrt(jnp.mean(x * x, axis=-1, keepdims=True) + NORM_EPS)
    return x * r, r


def _rms_bwd(xhat, r, g, dy):
    u = dy * g
    return r * (u - xhat * jnp.mean(u * xhat, axis=-1, keepdims=True))


def _gn(y):
    mu = jnp.mean(y, axis=-1, keepdims=True)
    yc = y - mu
    rs = lax.rsqrt(jnp.mean(yc * yc, axis=-1, keepdims=True) + GN_EPS)
    return yc * rs, rs


def _gn_bwd(yh, rs, d):
    return rs * (d - jnp.mean(d, axis=-1, keepdims=True) - yh * jnp.mean(d * yh, axis=-1, keepdims=True))


def _sigmoid(x):
    return 1.0 / (1.0 + jnp.exp(-x))


def _device_index(px, py, pc):
    return 4 * px + 2 * py + pc


def _direct_exchange(src_for, land_ref, send_sems, recv_sems, local_sem):
    x, y, c = lax.axis_index("x"), lax.axis_index("y"), lax.axis_index("c")
    me = _device_index(x, y, c)
    peers = []
    for k in range(1, NDEV):
        pos = (1 - x if k & 4 else x, 1 - y if k & 2 else y, 1 - c if k & 1 else c)
        peers.append((k - 1, pos, _device_index(*pos)))

    def local():
        return pltpu.make_async_copy(src_for(me), land_ref.at[me], local_sem)

    def remote(k, pos, idx):
        return pltpu.make_async_remote_copy(
            src_ref=src_for(idx), dst_ref=land_ref.at[me], send_sem=send_sems.at[k], recv_sem=recv_sems.at[k],
            device_id=pos, device_id_type=pl.DeviceIdType.MESH)

    def arrival(k, idx):
        return pltpu.make_async_remote_copy(
            src_ref=src_for(idx), dst_ref=land_ref.at[idx], send_sem=send_sems.at[k], recv_sem=recv_sems.at[k],
            device_id=(x, y, c), device_id_type=pl.DeviceIdType.MESH)

    def start():
        local().start()
        for p in peers:
            remote(*p).start()

    def wait():
        for k, _, idx in peers:
            arrival(k, idx).wait_recv()
        for p in peers:
            remote(*p).wait_send()
        local().wait()

    return start, wait


def _mm_call(compute, name, grid, arrays, in_specs, out_shape, out_spec, acc_shape, exch):
    steps = grid[0] * grid[1]

    def body(*refs):
        refs = list(refs)
        a_ref, b_ref = refs[:2]
        src_ref = refs.pop(2) if exch else None
        o_ref = refs[2]
        land_ref = refs.pop(3) if exch else None
        acc_ref = refs[3] if acc_shape else None
        if exch:
            send_sems, recv_sems, local_sem = refs[-3:]
            src_for = (lambda d: src_ref) if exch[1] == "gather" else (lambda d: src_ref.at[d])
            start, wait = _direct_exchange(src_for, land_ref, send_sems, recv_sems, local_sem)
            step = pl.program_id(0) * grid[1] + pl.program_id(1)
            pl.when(step == 0)(start)
        compute(a_ref, b_ref, o_ref, acc_ref)
        if exch:
            pl.when(step == steps - 1)(wait)

    out_shapes, out_specs, scratch = [out_shape], [out_spec], []
    if acc_shape:
        scratch.append(pltpu.VMEM(acc_shape, F32))
    if exch:
        src = exch[0]
        any_spec = pl.BlockSpec(memory_space=pl.ANY)
        arrays, in_specs = list(arrays) + [src], list(in_specs) + [any_spec]
        land = src.shape if exch[1] == "scatter" else (NDEV,) + src.shape
        out_shapes.append(jax.ShapeDtypeStruct(land, src.dtype))
        out_specs.append(any_spec)
        scratch += [pltpu.SemaphoreType.DMA((NDEV - 1,)), pltpu.SemaphoreType.DMA((NDEV - 1,)),
                    pltpu.SemaphoreType.DMA(())]
    out = pl.pallas_call(
        body, name=name, grid=grid, in_specs=in_specs, out_specs=out_specs, out_shape=out_shapes,
        scratch_shapes=scratch,
        compiler_params=_params(*(("arbitrary", "arbitrary") if exch else ("parallel", "arbitrary"))),
    )(*arrays)
    return tuple(out) if exch else out[0]


def mm_nt(a, b, out_dtype, name, exch=None):
    m, k = a.shape
    n = b.shape[0]
    tm = _tile(m, (768, 512, 256))
    tn = _tile(n, WIDE_TILES)

    def compute(a_ref, b_ref, o_ref, acc_ref):
        o_ref[...] = _dot_nt(a_ref[...], b_ref[...]).astype(o_ref.dtype)

    return _mm_call(
        compute, name, (m // tm, n // tn), (a, b),
        [pl.BlockSpec((tm, k), lambda i, j: (i, 0)), pl.BlockSpec((tn, k), lambda i, j: (j, 0))],
        jax.ShapeDtypeStruct((m, n), out_dtype), pl.BlockSpec((tm, tn), lambda i, j: (i, j)), None, exch)


def _accumulate(dot, steps):
    def compute(a_ref, b_ref, o_ref, acc_ref):
        kk = pl.program_id(1)

        @pl.when(kk == 0)
        def _():
            acc_ref[...] = jnp.zeros_like(acc_ref)

        acc_ref[...] += dot(a_ref[...], b_ref[...])

        @pl.when(kk == steps - 1)
        def _():
            o_ref[...] = acc_ref[...].astype(o_ref.dtype)

    return compute


def mm_nn(a, b, out_dtype, name, exch=None):
    m, k = a.shape
    n = b.shape[1]
    tm = _tile(m, (768, 512, 256))
    tk = _tile(k, (2816,) + WIDE_TILES)
    return _mm_call(
        _accumulate(_dot, k // tk), name, (m // tm, k // tk), (a, b),
        [pl.BlockSpec((tm, tk), lambda i, kk: (i, kk)), pl.BlockSpec((tk, n), lambda i, kk: (kk, 0))],
        jax.ShapeDtypeStruct((m, n), out_dtype), pl.BlockSpec((tm, n), lambda i, kk: (i, 0)), (tm, n), exch)


def mm_tn(a, b, out_dtype, name, exch=None):
    m, ka = a.shape
    n = b.shape[1]
    ta = _tile(ka, WIDE_TILES)
    tl = _tile(m, (1408, 768, 512, 256))
    return _mm_call(
        _accumulate(_dot_tn, m // tl), name, (ka // ta, m // tl), (a, b),
        [pl.BlockSpec((tl, ta), lambda i, ll: (ll, i)), pl.BlockSpec((tl, n), lambda i, ll: (ll, 0))],
        jax.ShapeDtypeStruct((ka, n), out_dtype), pl.BlockSpec((ta, n), lambda i, ll: (i, 0)), (ta, n), exch)


TM = 256


def _rb(arr, width=None, col_block=0):
    w = arr.shape[1] if width is None else width
    return pl.BlockSpec((TM, w), lambda i: (i, col_block))


def _whole(arr):
    return pl.BlockSpec(arr.shape, lambda i: (0,) * arr.ndim)


def _rows_call(body, name, lq, ins, in_specs, out_shapes, out_specs):
    return pl.pallas_call(
        body, name=name, grid=(lq // TM,), in_specs=in_specs, out_specs=out_specs, out_shape=out_shapes,
        compiler_params=_params("arbitrary"),
    )(*ins)


def rms_fwd(h, g, name):
    lq = h.shape[0]

    def body(h_ref, g_ref, o_ref):
        xhat, _ = _rms_hat(h_ref[...])
        o_ref[...] = (xhat * g_ref[...]).astype(BF16)

    return _rows_call(body, name, lq, (h, g), [_rb(h), _whole(g)],
                      jax.ShapeDtypeStruct((lq, D), BF16), _rb(h))


def rotary_fwd(proj, cos, sin):
    lq = proj.shape[0]
    half = RDK // 2

    def body(p_ref, c_ref, s_ref, q_ref, k_ref):
        c, s = c_ref[...], s_ref[...]
        for col, o_ref, scale in ((C_RQ, q_ref, RDK ** -0.5), (C_RK, k_ref, 1.0)):
            for h in range(RH):
                x1 = p_ref[:, col + h * RDK: col + h * RDK + half].astype(F32)
                x2 = p_ref[:, col + h * RDK + half: col + (h + 1) * RDK].astype(F32)
                o_ref[:, h * RDK: h * RDK + half] = ((x1 * c - x2 * s) * scale).astype(BF16)
                o_ref[:, h * RDK + half: (h + 1) * RDK] = ((x1 * s + x2 * c) * scale).astype(BF16)

    out = jax.ShapeDtypeStruct((lq, RH * RDK), BF16)
    return _rows_call(body, "rotary_fwd", lq, (proj, cos, sin),
                      [_rb(proj, 2 * RH * RDK, 0), _rb(cos), _rb(sin)],
                      (out, out), (pl.BlockSpec((TM, RH * RDK), lambda i: (i, 0)),) * 2)


def rotary_bwd(dq, dk, cos, sin):
    lq = dq.shape[0]
    half = RDK // 2

    def body(dq_ref, dk_ref, c_ref, s_ref, o_ref):
        c, s = c_ref[...], s_ref[...]
        for col, d_ref, scale in ((C_RQ, dq_ref, RDK ** -0.5), (C_RK, dk_ref, 1.0)):
            for h in range(RH):
                d1 = d_ref[:, h * RDK: h * RDK + half]
                d2 = d_ref[:, h * RDK + half: (h + 1) * RDK]
                o_ref[:, col + h * RDK: col + h * RDK + half] = ((d1 * c + d2 * s) * scale).astype(BF16)
                o_ref[:, col + h * RDK + half: col + (h + 1) * RDK] = ((d2 * c - d1 * s) * scale).astype(BF16)

    return _rows_call(body, "rotary_bwd", lq, (dq, dk, cos, sin), [_rb(dq), _rb(dk), _rb(cos), _rb(sin)],
                      jax.ShapeDtypeStruct((lq, 2 * RH * RDK), BF16),
                      pl.BlockSpec((TM, 2 * RH * RDK), lambda i: (i, 0)))


def ret_gate_fwd(proj, o_ret):
    lq = proj.shape[0]

    def body(p_ref, y_ref, o_ref):
        for h in range(RH):
            sl = slice(h * RDV, (h + 1) * RDV)
            yh, _ = _gn(y_ref[:, sl])
            rg = p_ref[:, sl].astype(F32)
            o_ref[:, sl] = (rg * _sigmoid(rg) * yh).astype(BF16)

    return _rows_call(body, "ret_gate_fwd", lq, (proj, o_ret),
                      [_rb(proj, RH * RDV, C_RG // (RH * RDV)), _rb(o_ret)],
                      jax.ShapeDtypeStruct((lq, RH * RDV), BF16), _rb(o_ret))


def ret_gate_bwd(proj, o_ret, d_gr):
    lq = proj.shape[0]

    def body(p_ref, y_ref, d_ref, drg_ref, dy_ref):
        for h in range(RH):
            sl = slice(h * RDV, (h + 1) * RDV)
            yh, rs = _gn(y_ref[:, sl])
            rg = p_ref[:, sl].astype(F32)
            sg = _sigmoid(rg)
            d = d_ref[:, sl].astype(F32)
            drg_ref[:, sl] = (d * yh * sg * (1.0 + rg * (1.0 - sg))).astype(BF16)
            dy_ref[:, sl] = _gn_bwd(yh, rs, d * rg * sg).astype(BF16)

    out = jax.ShapeDtypeStruct((lq, RH * RDV), BF16)
    return _rows_call(body, "ret_gate_bwd", lq, (proj, o_ret, d_gr),
                      [_rb(proj, RH * RDV, C_RG // (RH * RDV)), _rb(o_ret), _rb(d_gr)],
                      (out, out), (_rb(o_ret), _rb(o_ret)))


def merge_fwd(proj, y_ret, y_sb):
    lq = proj.shape[0]

    def body(ga_ref, gb_ref, yr_ref, ys_ref, o_ref):
        o_ref[...] = (_sigmoid(ga_ref[...].astype(F32)) * yr_ref[...]
                      + _sigmoid(gb_ref[...].astype(F32)) * ys_ref[...]).astype(BF16)

    return _rows_call(body, "merge_fwd", lq, (proj, proj, y_ret, y_sb),
                      [_rb(proj, D, C_GA // D), _rb(proj, D, C_GB // D), _rb(y_ret), _rb(y_sb)],
                      jax.ShapeDtypeStruct((lq, D), BF16), _rb(y_ret))


def merge_bwd(proj, y_ret, y_sb, d_merged):
    lq = proj.shape[0]

    def body(ga_ref, gb_ref, yr_ref, ys_ref, d_ref, dyr_ref, dys_ref, dg_ref):
        d = d_ref[...].astype(F32)
        sa = _sigmoid(ga_ref[...].astype(F32))
        sb = _sigmoid(gb_ref[...].astype(F32))
        dyr_ref[...] = (d * sa).astype(BF16)
        dys_ref[...] = (d * sb).astype(BF16)
        dg_ref[:, :D] = (d * yr_ref[...] * sa * (1.0 - sa)).astype(BF16)
        dg_ref[:, D:] = (d * ys_ref[...] * sb * (1.0 - sb)).astype(BF16)

    o1 = jax.ShapeDtypeStruct((lq, D), BF16)
    return _rows_call(body, "merge_bwd", lq, (proj, proj, y_ret, y_sb, d_merged),
                      [_rb(proj, D, C_GA // D), _rb(proj, D, C_GB // D), _rb(y_ret), _rb(y_sb), _rb(d_merged)],
                      (o1, o1, jax.ShapeDtypeStruct((lq, 2 * D), BF16)),
                      (_rb(y_ret), _rb(y_ret), pl.BlockSpec((TM, 2 * D), lambda i: (i, 0))))


def post_mix_fwd(hp, mix, g_post, g_pre):
    lq = hp.shape[0]

    def body(h_ref, m_ref, g2_ref, g3_ref, h1_ref, hn_ref):
        mhat, _ = _rms_hat(m_ref[...])
        h1 = h_ref[...] + mhat * g2_ref[...]
        h1_ref[...] = h1
        hhat, _ = _rms_hat(h1)
        hn_ref[...] = (hhat * g3_ref[...]).astype(BF16)

    return _rows_call(body, "post_mix_fwd", lq, (hp, mix, g_post, g_pre),
                      [_rb(hp), _rb(mix), _whole(g_post), _whole(g_pre)],
                      (jax.ShapeDtypeStruct((lq, D), F32), jax.ShapeDtypeStruct((lq, D), BF16)),
                      (_rb(hp), _rb(hp)))


def swiglu_fwd(ab):
    lq = ab.shape[0]

    def body(a_ref, b_ref, o_ref):
        a = a_ref[...].astype(F32)
        o_ref[...] = (a * _sigmoid(a) * b_ref[...].astype(F32)).astype(BF16)

    return _rows_call(body, "swiglu_fwd", lq, (ab, ab), [_rb(ab, DFF, 0), _rb(ab, DFF, 1)],
                      jax.ShapeDtypeStruct((lq, DFF), BF16), pl.BlockSpec((TM, DFF), lambda i: (i, 0)))


def swiglu_bwd(ab, d_act):
    lq = ab.shape[0]

    def body(a_ref, b_ref, d_ref, o_ref):
        a = a_ref[...].astype(F32)
        b = b_ref[...].astype(F32)
        d = d_ref[...].astype(F32)
        sg = _sigmoid(a)
        o_ref[:, :DFF] = (d * b * sg * (1.0 + a * (1.0 - sg))).astype(BF16)
        o_ref[:, DFF:] = (d * a * sg).astype(BF16)

    return _rows_call(body, "swiglu_bwd", lq, (ab, ab, d_act), [_rb(ab, DFF, 0), _rb(ab, DFF, 1), _rb(d_act)],
                      jax.ShapeDtypeStruct((lq, 2 * DFF), BF16), pl.BlockSpec((TM, 2 * DFF), lambda i: (i, 0)))


def loss_head(h1, ff, g_post, target):
    lq = h1.shape[0]
    front_blocks = FRONT // TM

    def body(h_ref, f_ref, g_ref, t_ref, loss_ref, dh_ref, df_ref, dg_ref):
        i = pl.program_id(0)

        @pl.when(i == 0)
        def _():
            loss_ref[...] = jnp.zeros_like(loss_ref)
            dg_ref[...] = jnp.zeros_like(dg_ref)

        g = g_ref[...]
        fhat, r = _rms_hat(f_ref[...])
        is_x = (i >= front_blocks).astype(F32)
        diff = (h_ref[...] + fhat * g - t_ref[...]) * is_x
        loss_ref[...] += 0.5 * jnp.sum(diff * diff) / D
        dy = diff / D
        dh_ref[...] = dy
        df_ref[...] = _rms_bwd(fhat, r, g, dy).astype(BF16)
        dg_ref[...] += jnp.sum(dy * fhat, axis=0, keepdims=True)

    return _rows_call(
        body, "loss_head", lq, (h1, ff, g_post, target),
        [_rb(h1), _rb(ff), _whole(g_post),
         pl.BlockSpec((TM, D), lambda i: (jnp.maximum(i - front_blocks, 0), 0))],
        (jax.ShapeDtypeStruct((8, 128), F32), jax.ShapeDtypeStruct((lq, D), F32),
         jax.ShapeDtypeStruct((lq, D), BF16), jax.ShapeDtypeStruct((GAIN_ROWS, D), F32)),
        (pl.BlockSpec((8, 128), lambda i: (0, 0)), _rb(h1), _rb(h1), pl.BlockSpec((GAIN_ROWS, D), lambda i: (0, 0))))


def post_mix_bwd(h1, d_hn2, g_pre, d_h2, mix, g_post):
    lq = h1.shape[0]

    def body(h_ref, dn_ref, g3_ref, dh2_ref, m_ref, g2_ref, dh1_ref, dm_ref, dg3_ref, dg2_ref):
        i = pl.program_id(0)

        @pl.when(i == 0)
        def _():
            dg3_ref[...] = jnp.zeros_like(dg3_ref)
            dg2_ref[...] = jnp.zeros_like(dg2_ref)

        hhat, r = _rms_hat(h_ref[...])
        dn = dn_ref[...]
        d_h1 = dh2_ref[...] + _rms_bwd(hhat, r, g3_ref[...], dn)
        dh1_ref[...] = d_h1
        dg3_ref[...] += jnp.sum(dn * hhat, axis=0, keepdims=True)
        mhat, rm = _rms_hat(m_ref[...])
        dm_ref[...] = _rms_bwd(mhat, rm, g2_ref[...], d_h1).astype(BF16)
        dg2_ref[...] += jnp.sum(d_h1 * mhat, axis=0, keepdims=True)

    vec = jax.ShapeDtypeStruct((GAIN_ROWS, D), F32)
    vspec = pl.BlockSpec((GAIN_ROWS, D), lambda i: (0, 0))
    return _rows_call(body, "post_mix_bwd", lq, (h1, d_hn2, g_pre, d_h2, mix, g_post),
                      [_rb(h1), _rb(d_hn2), _whole(g_pre), _rb(d_h2), _rb(mix), _whole(g_post)],
                      (jax.ShapeDtypeStruct((lq, D), F32), jax.ShapeDtypeStruct((lq, D), BF16), vec, vec),
                      (_rb(h1), _rb(h1), vspec, vspec))


def pre_mix_bwd(hp, d_hn1, g_pre, d_h1):
    lq = hp.shape[0]

    def body(h_ref, dn_ref, g_ref, dh1_ref, dhp_ref, dg_ref):
        i = pl.program_id(0)

        @pl.when(i == 0)
        def _():
            dg_ref[...] = jnp.zeros_like(dg_ref)

        hhat, r = _rms_hat(h_ref[...])
        dn = dn_ref[...]
        dhp_ref[...] = dh1_ref[...] + _rms_bwd(hhat, r, g_ref[...], dn)
        dg_ref[...] += jnp.sum(dn * hhat, axis=0, keepdims=True)

    return _rows_call(body, "pre_mix_bwd", lq, (hp, d_hn1, g_pre, d_h1),
                      [_rb(hp), _rb(d_hn1), _whole(g_pre), _rb(d_h1)],
                      (jax.ShapeDtypeStruct((lq, D), F32), jax.ShapeDtypeStruct((GAIN_ROWS, D), F32)),
                      (_rb(hp), pl.BlockSpec((GAIN_ROWS, D), lambda i: (0, 0))))


def _retention_tables():
    log_g = jnp.log1p(-(2.0 ** (-5.0 - jnp.arange(RH, dtype=F32))))
    idx = jnp.arange(CHUNK, dtype=F32)
    diff = idx[:, None] - idx[None, :]
    decay = jnp.where(diff >= 0, jnp.exp(log_g[:, None, None] * jnp.maximum(diff, 0.0)), 0.0)
    zeta = jnp.exp(log_g[:, None] * (CHUNK - 1.0 - idx))
    xi = jnp.exp(log_g[:, None] * (idx + 1.0))
    g_chunk = jnp.broadcast_to(jnp.exp(log_g * CHUNK)[:, None], (RH, CHUNK))
    coef = jnp.stack([xi, zeta, g_chunk] + [jnp.zeros_like(xi)] * 125, axis=-1)
    return decay, coef


def retention_fwd(qr, kr, proj, decay, coef):
    lq = qr.shape[0]
    n = lq // CHUNK

    def body(q_ref, k_ref, v_ref, dec_ref, cf_ref, o_ref, st_ref, state):
        @pl.when(pl.program_id(0) == 0)
        def _():
            state[...] = jnp.zeros_like(state)

        for h in range(RH):
            qk, vv = slice(h * RDK, (h + 1) * RDK), slice(h * RDV, (h + 1) * RDV)
            q, k, v = q_ref[:, qk], k_ref[:, qk], v_ref[:, vv]
            xi, zeta, gch = cf_ref[h, :, 0:1], cf_ref[h, :, 1:2], cf_ref[h, 0:1, 2:3]
            st = state[h]
            stb = st.astype(BF16)
            st_ref[h] = stb
            s = _dot_nt(q, k) * dec_ref[h]
            o_ref[:, vv] = _dot(s.astype(BF16), v) + _dot(q, stb) * xi
            vz = (v.astype(F32) * zeta).astype(BF16)
            state[h] = gch * st + _dot_tn(k, vz)

    return pl.pallas_call(
        body, name="retention_fwd", grid=(n,),
        in_specs=[pl.BlockSpec((CHUNK, RH * RDK), lambda c: (c, 0)), pl.BlockSpec((CHUNK, RH * RDK), lambda c: (c, 0)),
                  pl.BlockSpec((CHUNK, RH * RDV), lambda c: (c, C_RV // (RH * RDV))),
                  pl.BlockSpec((RH, CHUNK, CHUNK), lambda c: (0, 0, 0)),
                  pl.BlockSpec((RH, CHUNK, 128), lambda c: (0, 0, 0))],
        out_specs=(pl.BlockSpec((CHUNK, RH * RDV), lambda c: (c, 0)),
                   pl.BlockSpec((None, RH, RDK, RDV), lambda c: (c, 0, 0, 0))),
        out_shape=(jax.ShapeDtypeStruct((lq, RH * RDV), F32), jax.ShapeDtypeStruct((n, RH, RDK, RDV), BF16)),
        scratch_shapes=[pltpu.VMEM((RH, RDK, RDV), F32)],
        compiler_params=_params("arbitrary"),
    )(qr, kr, proj, decay, coef)


def retention_bwd(qr, kr, proj, d_o, states, decay, coef):
    lq = qr.shape[0]
    n = lq // CHUNK

    def body(q_ref, k_ref, v_ref, do_ref, st_ref, dec_ref, cf_ref, dq_ref, dk_ref, dv_ref, dstate):
        @pl.when(pl.program_id(0) == 0)
        def _():
            dstate[...] = jnp.zeros_like(dstate)

        for h in range(RH):
            qk, vv = slice(h * RDK, (h + 1) * RDK), slice(h * RDV, (h + 1) * RDV)
            q, k, v, dob = q_ref[:, qk], k_ref[:, qk], v_ref[:, vv], do_ref[:, vv]
            xi, zeta, gch = cf_ref[h, :, 0:1], cf_ref[h, :, 1:2], cf_ref[h, 0:1, 2:3]
            dec = dec_ref[h]
            dsn = dstate[h]
            dsnb = dsn.astype(BF16)
            dox = (dob.astype(F32) * xi).astype(BF16)
            sb = (_dot_nt(q, k) * dec).astype(BF16)
            dsb = (_dot_nt(dob, v) * dec).astype(BF16)
            vz = (v.astype(F32) * zeta).astype(BF16)
            dq_ref[:, qk] = _dot(dsb, k) + _dot_nt(dox, st_ref[h])
            dk_ref[:, qk] = _dot_tn(dsb, q) + _dot_nt(vz, dsnb)
            dv_ref[:, vv] = (_dot_tn(sb, dob) + _dot(k, dsnb) * zeta).astype(BF16)
            dstate[h] = gch * dsn + _dot_tn(q, dox)

    rev = lambda c: n - 1 - c
    qk_spec = pl.BlockSpec((CHUNK, RH * RDK), lambda c: (rev(c), 0))
    v_spec = pl.BlockSpec((CHUNK, RH * RDV), lambda c: (rev(c), 0))
    return pl.pallas_call(
        body, name="retention_bwd", grid=(n,),
        in_specs=[qk_spec, qk_spec,
                  pl.BlockSpec((CHUNK, RH * RDV), lambda c: (rev(c), C_RV // (RH * RDV))), v_spec,
                  pl.BlockSpec((None, RH, RDK, RDV), lambda c: (rev(c), 0, 0, 0)),
                  pl.BlockSpec((RH, CHUNK, CHUNK), lambda c: (0, 0, 0)),
                  pl.BlockSpec((RH, CHUNK, 128), lambda c: (0, 0, 0))],
        out_specs=(qk_spec, qk_spec, v_spec),
        out_shape=(jax.ShapeDtypeStruct((lq, RH * RDK), F32), jax.ShapeDtypeStruct((lq, RH * RDK), F32),
                   jax.ShapeDtypeStruct((lq, RH * RDV), BF16)),
        scratch_shapes=[pltpu.VMEM((RH, RDK, RDV), F32)],
        compiler_params=_params("arbitrary"),
    )(qr, kr, proj, d_o, states, decay, coef)


def _sb_masks_and_u():
    lane = lax.broadcasted_iota(jnp.int32, (1, 2 * SB_DH), 1)
    lo = lane < SB_DH
    row = lax.broadcasted_iota(jnp.int32, (SB_T, SB_T), 0)
    col = lax.broadcasted_iota(jnp.int32, (SB_T, SB_T), 1)
    u = (row > col).astype(BF16)
    return lo, u, row, col


def _sb_rows(j):
    start = j * SB_T
    return pl.ds(start if isinstance(j, int) else pl.multiple_of(start, SB_T), SB_T)


SB_DEAD = -104.0
SB_ROWS = SB_T
SB_CHAINS = tuple((hh, slice(r, r + SB_ROWS)) for hh in range(2) for r in range(0, SB_T, SB_ROWS))


def _sb_alive(logs):
    m = logs[0]
    for l in logs[1:]:
        m = jnp.maximum(m, l)
    return (jnp.max(m) > SB_DEAD).astype(jnp.int32)


def _sb_walk(i, run, logs_of, step):
    def cond(c):
        return (c[0] >= 1) & (c[1] > 0)

    def body(c):
        r = step(c[0], c[2])
        return c[0] - 1, _sb_alive(logs_of(r)), r

    return lax.while_loop(cond, body, (i - 1, _sb_alive(logs_of(run)), run))


def _sb_logs(z):
    ls = jnp.minimum(z, 0.0) - jnp.log(1.0 + jnp.exp(-jnp.abs(z)))
    return ls, ls - z


def sb_fwd(proj):
    lq = proj.shape[0]
    t = SB_T

    def body(q_ref, k_ref, v_ref, o_ref, of_ref, acc_ref):
        i = pl.program_id(1)
        lo, u, row, col = _sb_masks_and_u()
        qs = (q_ref[...].astype(F32) * SB_DH ** -0.5).astype(BF16)
        zero = jnp.zeros_like(qs)
        qh = (jnp.where(lo, qs, zero), jnp.where(lo, zero, qs))
        acc_ref[...] = jnp.zeros_like(acc_ref)

        def block(j, run, masked):
            rows = _sb_rows(j)
            ks, vs = k_ref[rows, :], v_ref[rows, :]
            if masked:
                valid = (col + j * t < row + i * t) & (col + j * t >= META0)
            out = []
            for hh, rs in SB_CHAINS:
                ls, ln = _sb_logs(_dot_nt(qh[hh][rs], ks))
                if masked:
                    ln = jnp.where(valid[rs], ln, 0.0)
                a = jnp.exp(ls + _dot(ln.astype(BF16), u) + run[len(out)])
                if masked:
                    a = jnp.where(valid[rs], a, 0.0)
                acc_ref[hh, rs, :] += _dot(a.astype(BF16), vs)
                out.append(run[len(out)] + jnp.sum(ln, axis=1, keepdims=True))
            return tuple(out)

        zeros = jnp.zeros((SB_ROWS, 1), F32)
        run = block(i, (zeros,) * len(SB_CHAINS), True)
        _, go, run = _sb_walk(i, run, lambda r: r, lambda j, r: block(j, r, False))

        @pl.when((i > 0) & (go > 0))
        def _():
            block(0, run, True)

        o = jnp.where(lo, acc_ref[0], acc_ref[1])
        o_ref[...] = o.astype(BF16)
        of_ref[...] = o

    blk = pl.BlockSpec((t, 128), lambda p, i: (i, p))
    return pl.pallas_call(
        body, name="sb_fwd", grid=(D // 128, lq // t),
        in_specs=[pl.BlockSpec((t, 128), lambda p, i: (i, C_SQ // 128 + p)),
                  pl.BlockSpec((lq, 128), lambda p, i: (0, C_SK // 128 + p)),
                  pl.BlockSpec((lq, 128), lambda p, i: (0, C_SV // 128 + p))],
        out_specs=(blk, blk),
        out_shape=(jax.ShapeDtypeStruct((lq, D), BF16), jax.ShapeDtypeStruct((lq, D), F32)),
        scratch_shapes=[pltpu.VMEM((2, t, 128), F32)],
        compiler_params=_params("parallel", "arbitrary"),
    )(proj, proj, proj)


def sb_bwd(proj, o, d_o):
    lq = proj.shape[0]
    t = SB_T

    def body(q_ref, k_ref, v_ref, o_ref, do_ref, dq_ref, dk_ref, dv_ref, acc_ref):
        i = pl.program_id(1)

        @pl.when(i == 0)
        def _():
            dk_ref[...] = jnp.zeros_like(dk_ref)
            dv_ref[...] = jnp.zeros_like(dv_ref)

        lo, u, row, col = _sb_masks_and_u()
        incl = (row <= col).astype(BF16)
        qs = (q_ref[...].astype(F32) * SB_DH ** -0.5).astype(BF16)
        do = do_ref[...]
        zero = jnp.zeros_like(qs)
        qh = (jnp.where(lo, qs, zero), jnp.where(lo, zero, qs))
        doh = (jnp.where(lo, do, zero), jnp.where(lo, zero, do))
        prod = o_ref[...] * do.astype(F32)
        dsum = (jnp.sum(jnp.where(lo, prod, 0.0), axis=1, keepdims=True),
                jnp.sum(jnp.where(lo, 0.0, prod), axis=1, keepdims=True))
        acc_ref[...] = jnp.zeros_like(acc_ref)

        def block(j, run, masked):
            rows = _sb_rows(j)
            ks, vs = k_ref[rows, :], v_ref[rows, :]
            if masked:
                valid = (col + j * t < row + i * t) & (col + j * t >= META0)
            out = []
            for hh in range(2):
                run_ln, run_e = run[2 * hh], run[2 * hh + 1]
                ls, ln = _sb_logs(_dot_nt(qh[hh], ks))
                if masked:
                    ln = jnp.where(valid, ln, 0.0)
                a = jnp.exp(ls + _dot(ln.astype(BF16), u) + run_ln)
                if masked:
                    a = jnp.where(valid, a, 0.0)
                ab = a.astype(BF16)
                e = ab.astype(F32) * _dot_nt(doh[hh], vs)
                e_sum = jnp.sum(e, axis=1, keepdims=True)
                upto = (dsum[hh] - run_e - e_sum) + _dot(e.astype(BF16), incl)
                dz = e - jnp.exp(ls) * upto
                if masked:
                    dz = jnp.where(valid, dz, 0.0)
                dzb = dz.astype(BF16)
                acc_ref[hh] += _dot(dzb, ks)
                dk_ref[rows, :] += _dot_tn(dzb, qh[hh])
                dv_ref[rows, :] += _dot_tn(ab, doh[hh])
                out += [run_ln + jnp.sum(ln, axis=1, keepdims=True), run_e + e_sum]
            return tuple(out)

        zeros = jnp.zeros((t, 1), F32)
        run = block(i, (zeros,) * 4, True)
        _, go, run = _sb_walk(i, run, lambda r: (r[0], r[2]), lambda j, r: block(j, r, False))

        @pl.when((i > 0) & (go > 0))
        def _():
            block(0, run, True)

        dq_ref[...] = (jnp.where(lo, acc_ref[0], acc_ref[1]) * SB_DH ** -0.5).astype(BF16)

    blk = pl.BlockSpec((t, 128), lambda p, i: (i, p))
    col_blk = pl.BlockSpec((lq, 128), lambda p, i: (0, p))
    return pl.pallas_call(
        body, name="sb_bwd", grid=(D // 128, lq // t),
        in_specs=[pl.BlockSpec((t, 128), lambda p, i: (i, C_SQ // 128 + p)),
                  pl.BlockSpec((lq, 128), lambda p, i: (0, C_SK // 128 + p)),
                  pl.BlockSpec((lq, 128), lambda p, i: (0, C_SV // 128 + p)), blk, blk],
        out_specs=(blk, col_blk, col_blk),
        out_shape=(jax.ShapeDtypeStruct((lq, D), BF16), jax.ShapeDtypeStruct((lq, D), F32),
                   jax.ShapeDtypeStruct((lq, D), F32)),
        scratch_shapes=[pltpu.VMEM((2, t, 128), F32)],
        compiler_params=_params("parallel", "arbitrary"),
    )(proj, proj, proj, o, d_o)


def all_gather(blocks):
    nb = len(blocks)

    def body(*refs):
        x_refs, out_refs = refs[:nb], refs[nb:2 * nb]
        send_sems, recv_sems, local_sems = refs[2 * nb:]
        x, y, c = lax.axis_index("x"), lax.axis_index("y"), lax.axis_index("c")
        me, sibling = (x, y, c), (x, y, 1 - c)
        chips = [(1 - x, y), (x, 1 - y), (1 - x, 1 - y)]

        def copy(b, k, block, to, src=None):
            slot = out_refs[b].at[_device_index(*block)]
            return pltpu.make_async_remote_copy(
                src_ref=slot if src is None else src, dst_ref=slot,
                send_sem=send_sems.at[b, k], recv_sem=recv_sems.at[b, k],
                device_id=to, device_id_type=pl.DeviceIdType.MESH)

        mine = [pltpu.make_async_copy(x_refs[b], out_refs[b].at[_device_index(*me)], local_sems.at[b])
                for b in range(nb)]
        for cp in mine:
            cp.start()
        first = []
        for b in range(nb):
            first.append(copy(b, 0, me, sibling, src=x_refs[b]))
            first += [copy(b, 1 + j, me, (*chip, c), src=x_refs[b]) for j, chip in enumerate(chips)]
        for cp in first:
            cp.start()
        passed = []
        for j, chip in enumerate(chips):
            for b in range(nb):
                copy(b, 1 + j, (*chip, c), me).wait_recv()
                cp = copy(b, 4 + j, (*chip, c), sibling)
                cp.start()
                passed.append(cp)
        for b in range(nb):
            copy(b, 0, sibling, me).wait_recv()
            for j, chip in enumerate(chips):
                copy(b, 4 + j, (*chip, 1 - c), me).wait_recv()
        for cp in first + passed:
            cp.wait_send()
        for cp in mine:
            cp.wait()

    any_spec = pl.BlockSpec(memory_space=pl.ANY)
    return pl.pallas_call(
        body, name="all_gather",
        in_specs=[any_spec] * nb, out_specs=[any_spec] * nb,
        out_shape=[jax.ShapeDtypeStruct((NDEV,) + b.shape, b.dtype) for b in blocks],
        scratch_shapes=[pltpu.SemaphoreType.DMA((nb, 7)), pltpu.SemaphoreType.DMA((nb, 7)),
                        pltpu.SemaphoreType.DMA((nb,))],
    )(*blocks)


def gather_small(small):
    def body(s_ref, land_ref, send_sems, recv_sems, local_sem):
        start, wait = _direct_exchange(lambda d: s_ref, land_ref, send_sems, recv_sems, local_sem)
        start()
        wait()

    any_spec = pl.BlockSpec(memory_space=pl.ANY)
    return pl.pallas_call(
        body, name="gather_small", in_specs=[any_spec], out_specs=any_spec,
        out_shape=jax.ShapeDtypeStruct((NDEV,) + small.shape, small.dtype),
        scratch_shapes=[pltpu.SemaphoreType.DMA((NDEV - 1,)), pltpu.SemaphoreType.DMA((NDEV - 1,)),
                        pltpu.SemaphoreType.DMA(())],
    )(small)


def sum_slots(landed, name):
    _, r, c = landed.shape
    tr = _tile(r, (352, 224, 8))

    def body(l_ref, o_ref):
        acc = l_ref[0].astype(F32)
        for p in range(1, NDEV):
            acc = acc + l_ref[p].astype(F32)
        o_ref[...] = acc

    return pl.pallas_call(
        body, name=name, grid=(r // tr,),
        in_specs=[pl.BlockSpec((NDEV, tr, c), lambda i: (0, i, 0))],
        out_specs=pl.BlockSpec((tr, c), lambda i: (i, 0)),
        out_shape=jax.ShapeDtypeStruct((r, c), F32),
        compiler_params=_params("parallel"),
    )(landed)


def adamw(w, g, m, v, name):
    r, c = w.shape
    tr = _tile(r, (256, 128))

    def body(w_ref, g_ref, m_ref, v_ref, d_ref, nm_ref, nv_ref):
        g_ = g_ref[...]
        m_ = ADAM_B1 * m_ref[...] + (1.0 - ADAM_B1) * g_
        v_ = ADAM_B2 * v_ref[...] + (1.0 - ADAM_B2) * jnp.square(g_)
        m_hat = m_ / (1.0 - ADAM_B1 ** ADAM_STEP)
        v_hat = v_ / (1.0 - ADAM_B2 ** ADAM_STEP)
        d_ref[...] = -ADAM_LR * (m_hat / (jnp.sqrt(v_hat) + ADAM_EPS) + ADAM_WD * w_ref[...])
        nm_ref[...] = m_
        nv_ref[...] = v_

    spec = pl.BlockSpec((tr, c), lambda i: (i, 0))
    out = jax.ShapeDtypeStruct((r, c), F32)
    return pl.pallas_call(
        body, name=name, grid=(r // tr,), in_specs=[spec] * 4, out_specs=(spec,) * 3, out_shape=(out,) * 3,
        compiler_params=_params("parallel"),
    )(w, g, m, v)


def kernel(x, meta_tokens, w_in, w_ret_out, w_sb_out, w_out, w_ffn_in, w_ffn_out, norm_mix_pre, norm_mix_post, norm_ffn_pre, norm_ffn_post, loss_target, m_meta_tokens, m_w_in, m_w_ret_out, m_w_sb_out, m_w_out, m_w_ffn_in, m_w_ffn_out, m_norm_mix_pre, m_norm_mix_post, m_norm_ffn_pre, m_norm_ffn_post, v_meta_tokens, v_w_in, v_w_ret_out, v_w_sb_out, v_w_out, v_w_ffn_in, v_w_ffn_out, v_norm_mix_pre, v_norm_mix_post, v_norm_ffn_pre, v_norm_ffn_post):
    seq = x.shape[1]
    lq = seq + FRONT
    me = _device_index(lax.axis_index("x"), lax.axis_index("y"), lax.axis_index("c"))

    shards = {"w_in_t": w_in[0].T, "w_ffn_in_t": w_ffn_in[0].T, "w_ret_out": w_ret_out[0],
              "w_sb_out": w_sb_out[0], "w_out": w_out[0], "w_ffn_out": w_ffn_out[0]}
    rest_rows = PACK_ROWS[1:]
    pack_rest = jnp.concatenate([shards[n].astype(BF16) for n, _ in rest_rows], axis=0)
    gathered_in, meta_all = all_gather([shards["w_in_t"].astype(BF16), meta_tokens])
    full = {"w_in_t": gathered_in.reshape(PROJ, D)}
    meta_full = meta_all.transpose(1, 0, 2).reshape(N_META, D)

    pos = jnp.arange(lq, dtype=F32) - META0
    half = RDK // 2
    ang = pos[:, None] * (ROPE_BASE ** (-jnp.arange(half, dtype=F32) / half))[None, :]
    cos, sin = jnp.cos(ang), jnp.sin(ang)
    decay, coef = _retention_tables()

    hp = jnp.concatenate([jnp.zeros((META0, D), F32), meta_full, x[0]], axis=0)
    hn1 = rms_fwd(hp, norm_mix_pre, "rms_mix_pre")
    proj, gathered_rest = mm_nt(hn1, full["w_in_t"], BF16, "proj", exch=(pack_rest, "gather"))
    off = 0
    for n, r in rest_rows:
        full[n] = gathered_rest[:, off:off + r, :].reshape(NDEV * r, D)
        off += r
    qr, kr = rotary_fwd(proj, cos, sin)
    o_ret, states = retention_fwd(qr, kr, proj, decay, coef)
    gr = ret_gate_fwd(proj, o_ret)
    o_sb, o_sb_f32 = sb_fwd(proj)
    y_ret = mm_nn(gr, full["w_ret_out"], F32, "y_ret")
    y_sb = mm_nn(o_sb, full["w_sb_out"], F32, "y_sb")
    merged = merge_fwd(proj, y_ret, y_sb)
    mix = mm_nn(merged, full["w_out"], F32, "mix")
    h1, hn2 = post_mix_fwd(hp, mix, norm_mix_post, norm_ffn_pre)
    ab = mm_nt(hn2, full["w_ffn_in_t"], BF16, "ffn_in")
    act = swiglu_fwd(ab)
    ff = mm_nn(act, full["w_ffn_out"], F32, "ffn_out")
    loss_blk, d_h2, d_ff, dg_ffn_post = loss_head(h1, ff, norm_ffn_post, loss_target[0])
    loss = lax.psum(loss_blk[0, 0], ("x", "y", "c"))

    grads = {}
    d_act = mm_nt(d_ff, full["w_ffn_out"], BF16, "d_act")
    grads["w_ffn_out"] = mm_tn(act, d_ff, BF16, "dw_ffn_out")
    d_ab = swiglu_bwd(ab, d_act)
    d_hn2 = mm_nn(d_ab, full["w_ffn_in_t"], F32, "d_hn2")
    grads["w_ffn_in_t"] = mm_tn(d_ab, hn2, BF16, "dw_ffn_in")
    d_h1, d_mix, dg_ffn_pre, dg_mix_post = post_mix_bwd(h1, d_hn2, norm_ffn_pre, d_h2, mix, norm_mix_post)
    d_merged = mm_nt(d_mix, full["w_out"], BF16, "d_merged")
    grads["w_out"] = mm_tn(merged, d_mix, BF16, "dw_out")
    d_y_ret, d_y_sb, d_gates = merge_bwd(proj, y_ret, y_sb, d_merged)
    d_gr = mm_nt(d_y_ret, full["w_ret_out"], BF16, "d_gr")
    grads["w_ret_out"] = mm_tn(gr, d_y_ret, BF16, "dw_ret_out")
    d_o_sb = mm_nt(d_y_sb, full["w_sb_out"], BF16, "d_o_sb")
    grads["w_sb_out"] = mm_tn(o_sb, d_y_sb, BF16, "dw_sb_out")
    d_rg, d_o_ret = ret_gate_bwd(proj, o_ret, d_gr)
    d_qr, d_kr, d_rv = retention_bwd(qr, kr, proj, d_o_ret, states, decay, coef)
    d_rqk = rotary_bwd(d_qr, d_kr, cos, sin)
    d_sq, d_sk, d_sv = sb_bwd(proj, o_sb_f32, d_o_sb)
    d_proj = jnp.concatenate([d_rqk, d_rv, d_rg, d_sq, d_sk.astype(BF16), d_sv.astype(BF16), d_gates], axis=1)
    parts_rest = jnp.concatenate([grads[n].reshape(NDEV, r, D) for n, r in rest_rows], axis=1)
    dw_in_t, landed_rest = mm_tn(d_proj, hn1, BF16, "dw_in", exch=(parts_rest, "scatter"))
    d_hn1, landed_in = mm_nn(d_proj, full["w_in_t"], F32, "d_hn1",
                             exch=(dw_in_t.reshape(NDEV, PROJ // NDEV, D), "scatter"))
    d_hp, dg_mix_pre = pre_mix_bwd(hp, d_hn1, norm_mix_pre, d_h1)
    grad_x = d_hp[FRONT:][None]
    small = jnp.concatenate([dg_mix_pre, dg_mix_post, dg_ffn_pre, dg_ffn_post, d_hp[META0:FRONT]], axis=0)
    ssum = sum_slots(gather_small(small), "sum_small")
    g = {"w_in_t": sum_slots(landed_in, "sum_grads_in")}
    gsum = sum_slots(landed_rest, "sum_grads_rest")
    off = 0
    for n, r in rest_rows:
        g[n] = gsum[off:off + r]
        off += r
    gain = lambda k: ssum[k * GAIN_ROWS:k * GAIN_ROWS + 1]
    g_w = {"meta_tokens": lax.dynamic_slice(ssum[4 * GAIN_ROWS:], (0, me * (D // NDEV)), (N_META, D // NDEV)),
           "w_in": g["w_in_t"].T[None], "w_ret_out": g["w_ret_out"][None], "w_sb_out": g["w_sb_out"][None],
           "w_out": g["w_out"][None], "w_ffn_in": g["w_ffn_in_t"].T[None], "w_ffn_out": g["w_ffn_out"][None],
           "norm_mix_pre": gain(0), "norm_mix_post": gain(1), "norm_ffn_pre": gain(2), "norm_ffn_post": gain(3)}

    names = ["meta_tokens", "w_in", "w_ret_out", "w_sb_out", "w_out", "w_ffn_in", "w_ffn_out",
             "norm_mix_pre", "norm_mix_post", "norm_ffn_pre", "norm_ffn_post"]
    w_of = dict(zip(names, (meta_tokens, w_in, w_ret_out, w_sb_out, w_out, w_ffn_in, w_ffn_out,
                            norm_mix_pre, norm_mix_post, norm_ffn_pre, norm_ffn_post)))
    m_of = dict(zip(names, (m_meta_tokens, m_w_in, m_w_ret_out, m_w_sb_out, m_w_out, m_w_ffn_in, m_w_ffn_out,
                            m_norm_mix_pre, m_norm_mix_post, m_norm_ffn_pre, m_norm_ffn_post)))
    v_of = dict(zip(names, (v_meta_tokens, v_w_in, v_w_ret_out, v_w_sb_out, v_w_out, v_w_ffn_in, v_w_ffn_out,
                            v_norm_mix_pre, v_norm_mix_post, v_norm_ffn_pre, v_norm_ffn_post)))
    delta, new_m, new_v = {}, {}, {}
    for n in names:
        shape = w_of[n].shape
        two_d = (shape[-2], shape[-1])
        d_, m_, v_ = adamw(w_of[n].reshape(two_d), g_w[n].reshape(two_d), m_of[n].reshape(two_d),
                           v_of[n].reshape(two_d), "adamw_" + n)
        delta[n], new_m[n], new_v[n] = d_.reshape(shape), m_.reshape(shape), v_.reshape(shape)

    return (loss, grad_x, *[g_w[n] for n in names], *[delta[n] for n in names],
            *[new_m[n] for n in names], *[new_v[n] for n in names])
```

```python
import jax
import jax.numpy as jnp
from jax import lax
from jax.experimental import pallas as pl
from jax.experimental.pallas import tpu as pltpu

F32 = jnp.float32
BF16 = jnp.bfloat16

D = 1024
N_META = 16
CHUNK = 128
FRONT = 256
META0 = FRONT - N_META
RH, RDK, RDV = 4, 256, 512
SB_DH = 64
DFF = 2816
NDEV = 8
ROPE_BASE = 10000.0
NORM_EPS = 1e-6
GN_EPS = 1e-5
C_RQ, C_RK, C_RV, C_RG, C_SQ, C_SK, C_SV, C_GA, C_GB = 0, 1024, 2048, 4096, 6144, 7168, 8192, 9216, 10240
PROJ = 11264
PACK_ROWS = (("w_in_t", PROJ // NDEV), ("w_ffn_in_t", 2 * DFF // NDEV), ("w_ret_out", RH * RDV // NDEV),
             ("w_sb_out", D // NDEV), ("w_out", D // NDEV), ("w_ffn_out", DFF // NDEV))
PACK = sum(r for _, r in PACK_ROWS)
GAIN_ROWS = 8

ADAM_LR = 0.001
ADAM_B1 = 0.9
ADAM_B2 = 0.999
ADAM_EPS = 1e-08
ADAM_WD = 0.01
ADAM_STEP = 10

VMEM_LIMIT = 56 * 1024 * 1024
SB_T = 256

NT = (((1,), (1,)), ((), ()))
TN = (((0,), (0,)), ((), ()))


def _dot(a, b):
    return jnp.dot(a, b, preferred_element_type=F32)


def _dot_nt(a, b):
    return lax.dot_general(a, b, NT, preferred_element_type=F32)


def _dot_tn(a, b):
    return lax.dot_general(a, b, TN, preferred_element_type=F32)


WIDE_TILES = (1024, 1408, 512, 256)


def _tile(n, candidates):
    for t in candidates:
        if n % t == 0:
            return t
    return n


def _params(*sem):
    return pltpu.CompilerParams(dimension_semantics=sem, vmem_limit_bytes=VMEM_LIMIT)


def _rms_hat(x):
    r = lax.rsqrt(jnp.mean(x * x, axis=-1, keepdims=True) + NORM_EPS)
    return x * r, r


def _rms_bwd(xhat, r, g, dy):
    u = dy * g
    return r * (u - xhat * jnp.mean(u * xhat, axis=-1, keepdims=True))


def _gn(y):
    mu = jnp.mean(y, axis=-1, keepdims=True)
    yc = y - mu
    rs = lax.rsqrt(jnp.mean(yc * yc, axis=-1, keepdims=True) + GN_EPS)
    return yc * rs, rs


def _gn_bwd(yh, rs, d):
    return rs * (d - jnp.mean(d, axis=-1, keepdims=True) - yh * jnp.mean(d * yh, axis=-1, keepdims=True))


def _sigmoid(x):
    return 1.0 / (1.0 + jnp.exp(-x))


def _device_index(px, py, pc):
    return 4 * px + 2 * py + pc


def _direct_exchange(src_for, land_ref, send_sems, recv_sems, local_sem):
    x, y, c = lax.axis_index("x"), lax.axis_index("y"), lax.axis_index("c")
    me = _device_index(x, y, c)
    peers = []
    for k in range(1, NDEV):
        pos = (1 - x if k & 4 else x, 1 - y if k & 2 else y, 1 - c if k & 1 else c)
        peers.append((k - 1, pos, _device_index(*pos)))

    def local():
        return pltpu.make_async_copy(src_for(me), land_ref.at[me], local_sem)

    def remote(k, pos, idx):
        return pltpu.make_async_remote_copy(
            src_ref=src_for(idx), dst_ref=land_ref.at[me], send_sem=send_sems.at[k], recv_sem=recv_sems.at[k],
            device_id=pos, device_id_type=pl.DeviceIdType.MESH)

    def arrival(k, idx):
        return pltpu.make_async_remote_copy(
            src_ref=src_for(idx), dst_ref=land_ref.at[idx], send_sem=send_sems.at[k], recv_sem=recv_sems.at[k],
            device_id=(x, y, c), device_id_type=pl.DeviceIdType.MESH)

    def start():
        local().start()
        for p in peers:
            remote(*p).start()

    def wait():
        for k, _, idx in peers:
            arrival(k, idx).wait_recv()
        for p in peers:
            remote(*p).wait_send()
        local().wait()

    return start, wait


def _grid_call(compute, name, grid, arrays, in_specs, out_shapes, out_specs, scratch, exch, aliases=None):
    arrays, in_specs = list(arrays), list(in_specs)
    out_shapes, out_specs, scratch = list(out_shapes), list(out_specs), list(scratch)
    n_in, n_out, steps = len(arrays), len(out_shapes), grid[0] * grid[1]
    aliases = aliases or {}
    if exch is None:
        return pl.pallas_call(
            compute, name=name, grid=grid, in_specs=in_specs, out_specs=out_specs, out_shape=out_shapes,
            scratch_shapes=scratch, input_output_aliases=aliases,
            compiler_params=_params("parallel", "arbitrary"))(*arrays)
    src, mode = exch

    def body(*refs):
        refs = list(refs)
        src_ref = refs.pop(n_in)
        land_ref = refs.pop(n_in + n_out)
        send_sems, recv_sems, local_sem = refs[-3:]
        src_for = (lambda d: src_ref) if mode == "gather" else (lambda d: src_ref.at[d])
        start, wait = _direct_exchange(src_for, land_ref, send_sems, recv_sems, local_sem)
        step = pl.program_id(0) * grid[1] + pl.program_id(1)
        pl.when(step == 0)(start)
        compute(*refs[:-3])
        pl.when(step == steps - 1)(wait)

    any_spec = pl.BlockSpec(memory_space=pl.ANY)
    land = src.shape if mode == "scatter" else (NDEV,) + src.shape
    return pl.pallas_call(
        body, name=name, grid=grid, in_specs=in_specs + [any_spec], out_specs=out_specs + [any_spec],
        out_shape=out_shapes + [jax.ShapeDtypeStruct(land, src.dtype)],
        scratch_shapes=scratch + [pltpu.SemaphoreType.DMA((NDEV - 1,)), pltpu.SemaphoreType.DMA((NDEV - 1,)),
                                  pltpu.SemaphoreType.DMA(())],
        input_output_aliases=aliases, compiler_params=_params("arbitrary", "arbitrary"))(*(arrays + [src]))


def _mm_call(compute, name, grid, arrays, in_specs, out_shape, out_spec, acc_shape, exch):
    if acc_shape:
        body, scratch = compute, [pltpu.VMEM(acc_shape, F32)]
    else:
        body, scratch = (lambda a_ref, b_ref, o_ref: compute(a_ref, b_ref, o_ref, None)), []
    out = _grid_call(body, name, grid, arrays, in_specs, [out_shape], [out_spec], scratch, exch)
    return tuple(out) if exch else out[0]


def mm_nt(a, b, out_dtype, name, exch=None):
    m, k = a.shape
    n = b.shape[0]
    tm = _tile(m, (768, 512, 256))
    tn = _tile(n, WIDE_TILES)

    def compute(a_ref, b_ref, o_ref, acc_ref):
        o_ref[...] = _dot_nt(a_ref[...], b_ref[...]).astype(o_ref.dtype)

    return _mm_call(
        compute, name, (m // tm, n // tn), (a, b),
        [pl.BlockSpec((tm, k), lambda i, j: (i, 0)), pl.BlockSpec((tn, k), lambda i, j: (j, 0))],
        jax.ShapeDtypeStruct((m, n), out_dtype), pl.BlockSpec((tm, tn), lambda i, j: (i, j)), None, exch)


def _accumulate(dot, steps):
    def compute(a_ref, b_ref, o_ref, acc_ref):
        kk = pl.program_id(1)

        @pl.when(kk == 0)
        def _():
            acc_ref[...] = jnp.zeros_like(acc_ref)

        acc_ref[...] += dot(a_ref[...], b_ref[...])

        @pl.when(kk == steps - 1)
        def _():
            o_ref[...] = acc_ref[...].astype(o_ref.dtype)

    return compute


def mm_nn(a, b, out_dtype, name, exch=None):
    m, k = a.shape
    n = b.shape[1]
    tm = _tile(m, (768, 512, 256))
    tk = _tile(k, (2816,) + WIDE_TILES)
    return _mm_call(
        _accumulate(_dot, k // tk), name, (m // tm, k // tk), (a, b),
        [pl.BlockSpec((tm, tk), lambda i, kk: (i, kk)), pl.BlockSpec((tk, n), lambda i, kk: (kk, 0))],
        jax.ShapeDtypeStruct((m, n), out_dtype), pl.BlockSpec((tm, n), lambda i, kk: (i, 0)), (tm, n), exch)


def mm_tn(a, b, out_dtype, name, exch=None):
    m, ka = a.shape
    n = b.shape[1]
    ta = _tile(ka, WIDE_TILES)
    tl = _tile(m, (1408, 768, 512, 256))
    return _mm_call(
        _accumulate(_dot_tn, m // tl), name, (ka // ta, m // tl), (a, b),
        [pl.BlockSpec((tl, ta), lambda i, ll: (ll, i)), pl.BlockSpec((tl, n), lambda i, ll: (ll, 0))],
        jax.ShapeDtypeStruct((ka, n), out_dtype), pl.BlockSpec((ta, n), lambda i, ll: (i, 0)), (ta, n), exch)


TM = 256


def _rb(arr, width=None, col_block=0):
    w = arr.shape[1] if width is None else width
    return pl.BlockSpec((TM, w), lambda i: (i, col_block))


def _whole(arr):
    return pl.BlockSpec(arr.shape, lambda i: (0,) * arr.ndim)


def _rows_call(body, name, lq, ins, in_specs, out_shapes, out_specs, aliases=None):
    return pl.pallas_call(
        body, name=name, grid=(lq // TM,), in_specs=in_specs, out_specs=out_specs, out_shape=out_shapes,
        input_output_aliases=aliases or {}, compiler_params=_params("arbitrary"),
    )(*ins)


ANY_SPEC = pl.BlockSpec(memory_space=pl.ANY)


def _d_proj_shape(lq):
    return jax.ShapeDtypeStruct((lq, PROJ), BF16)


def rms_fwd(h, g, name):
    lq = h.shape[0]

    def body(h_ref, g_ref, o_ref):
        xhat, _ = _rms_hat(h_ref[...])
        o_ref[...] = (xhat * g_ref[...]).astype(BF16)

    return _rows_call(body, name, lq, (h, g), [_rb(h), _whole(g)],
                      jax.ShapeDtypeStruct((lq, D), BF16), _rb(h))


def rotary_fwd(proj, cos, sin):
    lq = proj.shape[0]
    half = RDK // 2

    def body(p_ref, c_ref, s_ref, q_ref, k_ref):
        c, s = c_ref[...], s_ref[...]
        for col, o_ref, scale in ((C_RQ, q_ref, RDK ** -0.5), (C_RK, k_ref, 1.0)):
            for h in range(RH):
                x1 = p_ref[:, col + h * RDK: col + h * RDK + half].astype(F32)
                x2 = p_ref[:, col + h * RDK + half: col + (h + 1) * RDK].astype(F32)
                o_ref[:, h * RDK: h * RDK + half] = ((x1 * c - x2 * s) * scale).astype(BF16)
                o_ref[:, h * RDK + half: (h + 1) * RDK] = ((x1 * s + x2 * c) * scale).astype(BF16)

    out = jax.ShapeDtypeStruct((lq, RH * RDK), BF16)
    return _rows_call(body, "rotary_fwd", lq, (proj, cos, sin),
                      [_rb(proj, 2 * RH * RDK, 0), _rb(cos), _rb(sin)],
                      (out, out), (pl.BlockSpec((TM, RH * RDK), lambda i: (i, 0)),) * 2)


def rotary_bwd(dq, dk, cos, sin, d_proj):
    lq = dq.shape[0]
    half = RDK // 2

    def body(dq_ref, dk_ref, c_ref, s_ref, _, o_ref):
        c, s = c_ref[...], s_ref[...]
        for col, d_ref, scale in ((C_RQ, dq_ref, RDK ** -0.5), (C_RK, dk_ref, 1.0)):
            for h in range(RH):
                d1 = d_ref[:, h * RDK: h * RDK + half]
                d2 = d_ref[:, h * RDK + half: (h + 1) * RDK]
                o_ref[:, col + h * RDK: col + h * RDK + half] = ((d1 * c + d2 * s) * scale).astype(BF16)
                o_ref[:, col + h * RDK + half: col + (h + 1) * RDK] = ((d2 * c - d1 * s) * scale).astype(BF16)

    return _rows_call(body, "rotary_bwd", lq, (dq, dk, cos, sin, d_proj),
                      [_rb(dq), _rb(dk), _rb(cos), _rb(sin), ANY_SPEC], _d_proj_shape(lq),
                      pl.BlockSpec((TM, 2 * RH * RDK), lambda i: (i, C_RQ // (2 * RH * RDK))), {4: 0})


def ret_gate_fwd(proj, o_ret):
    lq = proj.shape[0]

    def body(p_ref, y_ref, o_ref):
        for h in range(RH):
            sl = slice(h * RDV, (h + 1) * RDV)
            yh, _ = _gn(y_ref[:, sl])
            rg = p_ref[:, sl].astype(F32)
            o_ref[:, sl] = (rg * _sigmoid(rg) * yh).astype(BF16)

    return _rows_call(body, "ret_gate_fwd", lq, (proj, o_ret),
                      [_rb(proj, RH * RDV, C_RG // (RH * RDV)), _rb(o_ret)],
                      jax.ShapeDtypeStruct((lq, RH * RDV), BF16), _rb(o_ret))


def ret_gate_bwd(proj, o_ret, d_gr, d_proj):
    lq = proj.shape[0]

    def body(p_ref, y_ref, d_ref, _, drg_ref, dy_ref):
        for h in range(RH):
            sl = slice(h * RDV, (h + 1) * RDV)
            yh, rs = _gn(y_ref[:, sl])
            rg = p_ref[:, sl].astype(F32)
            sg = _sigmoid(rg)
            d = d_ref[:, sl].astype(F32)
            drg_ref[:, sl] = (d * yh * sg * (1.0 + rg * (1.0 - sg))).astype(BF16)
            dy_ref[:, sl] = _gn_bwd(yh, rs, d * rg * sg).astype(BF16)

    return _rows_call(body, "ret_gate_bwd", lq, (proj, o_ret, d_gr, d_proj),
                      [_rb(proj, RH * RDV, C_RG // (RH * RDV)), _rb(o_ret), _rb(d_gr), ANY_SPEC],
                      (_d_proj_shape(lq), jax.ShapeDtypeStruct((lq, RH * RDV), BF16)),
                      (_rb(o_ret, col_block=C_RG // (RH * RDV)), _rb(o_ret)), {3: 0})


def merge_fwd(proj, y_ret, y_sb):
    lq = proj.shape[0]

    def body(ga_ref, gb_ref, yr_ref, ys_ref, o_ref):
        o_ref[...] = (_sigmoid(ga_ref[...].astype(F32)) * yr_ref[...]
                      + _sigmoid(gb_ref[...].astype(F32)) * ys_ref[...]).astype(BF16)

    return _rows_call(body, "merge_fwd", lq, (proj, proj, y_ret, y_sb),
                      [_rb(proj, D, C_GA // D), _rb(proj, D, C_GB // D), _rb(y_ret), _rb(y_sb)],
                      jax.ShapeDtypeStruct((lq, D), BF16), _rb(y_ret))


def merge_bwd(proj, y_ret, y_sb, d_merged):
    lq = proj.shape[0]

    def body(gate_ref, yr_ref, ys_ref, d_ref, dyr_ref, dys_ref, dg_ref):
        d = d_ref[...].astype(F32)
        sg = _sigmoid(gate_ref[...].astype(F32))
        for g, y_ref, dy_ref in ((0, yr_ref, dyr_ref), (1, ys_ref, dys_ref)):
            @pl.when(pl.program_id(1) == g)
            def _():
                dy_ref[...] = (d * sg).astype(BF16)
                dg_ref[...] = (d * y_ref[...] * sg * (1.0 - sg)).astype(BF16)

    row = pl.BlockSpec((TM, D), lambda i, g: (i, 0))
    o1 = jax.ShapeDtypeStruct((lq, D), BF16)
    return pl.pallas_call(
        body, name="merge_bwd", grid=(lq // TM, 2),
        in_specs=[pl.BlockSpec((TM, D), lambda i, g: (i, C_GA // D + g)), row, row, row],
        out_specs=(row, row, pl.BlockSpec((TM, D), lambda i, g: (i, C_GA // D + g))),
        out_shape=(o1, o1, _d_proj_shape(lq)),
        compiler_params=_params("arbitrary", "arbitrary"),
    )(proj, y_ret, y_sb, d_merged)


def post_mix_fwd(hp, mix, g_post, g_pre):
    lq = hp.shape[0]

    def body(h_ref, m_ref, g2_ref, g3_ref, h1_ref, hn_ref):
        mhat, _ = _rms_hat(m_ref[...])
        h1 = h_ref[...] + mhat * g2_ref[...]
        h1_ref[...] = h1
        hhat, _ = _rms_hat(h1)
        hn_ref[...] = (hhat * g3_ref[...]).astype(BF16)

    return _rows_call(body, "post_mix_fwd", lq, (hp, mix, g_post, g_pre),
                      [_rb(hp), _rb(mix), _whole(g_post), _whole(g_pre)],
                      (jax.ShapeDtypeStruct((lq, D), F32), jax.ShapeDtypeStruct((lq, D), BF16)),
                      (_rb(hp), _rb(hp)))


def swiglu_fwd(ab):
    lq = ab.shape[0]

    def body(a_ref, b_ref, o_ref):
        a = a_ref[...].astype(F32)
        o_ref[...] = (a * _sigmoid(a) * b_ref[...].astype(F32)).astype(BF16)

    return _rows_call(body, "swiglu_fwd", lq, (ab, ab), [_rb(ab, DFF, 0), _rb(ab, DFF, 1)],
                      jax.ShapeDtypeStruct((lq, DFF), BF16), pl.BlockSpec((TM, DFF), lambda i: (i, 0)))


def swiglu_bwd(ab, d_act):
    lq = ab.shape[0]

    def body(a_ref, b_ref, d_ref, o_ref):
        a = a_ref[...].astype(F32)
        b = b_ref[...].astype(F32)
        d = d_ref[...].astype(F32)
        sg = _sigmoid(a)
        o_ref[:, :DFF] = (d * b * sg * (1.0 + a * (1.0 - sg))).astype(BF16)
        o_ref[:, DFF:] = (d * a * sg).astype(BF16)

    return _rows_call(body, "swiglu_bwd", lq, (ab, ab, d_act), [_rb(ab, DFF, 0), _rb(ab, DFF, 1), _rb(d_act)],
                      jax.ShapeDtypeStruct((lq, 2 * DFF), BF16), pl.BlockSpec((TM, 2 * DFF), lambda i: (i, 0)))


def loss_head(h1, ff, g_post, target):
    lq = h1.shape[0]
    front_blocks = FRONT // TM

    def body(h_ref, f_ref, g_ref, t_ref, loss_ref, dh_ref, df_ref, dg_ref):
        i = pl.program_id(0)

        @pl.when(i == 0)
        def _():
            loss_ref[...] = jnp.zeros_like(loss_ref)
            dg_ref[...] = jnp.zeros_like(dg_ref)

        g = g_ref[...]
        fhat, r = _rms_hat(f_ref[...])
        is_x = (i >= front_blocks).astype(F32)
        diff = (h_ref[...] + fhat * g - t_ref[...]) * is_x
        loss_ref[...] += 0.5 * jnp.sum(diff * diff) / D
        dy = diff / D
        dh_ref[...] = dy
        df_ref[...] = _rms_bwd(fhat, r, g, dy).astype(BF16)
        dg_ref[...] += jnp.sum(dy * fhat, axis=0, keepdims=True)

    return _rows_call(
        body, "loss_head", lq, (h1, ff, g_post, target),
        [_rb(h1), _rb(ff), _whole(g_post),
         pl.BlockSpec((TM, D), lambda i: (jnp.maximum(i - front_blocks, 0), 0))],
        (jax.ShapeDtypeStruct((8, 128), F32), jax.ShapeDtypeStruct((lq, D), F32),
         jax.ShapeDtypeStruct((lq, D), BF16), jax.ShapeDtypeStruct((GAIN_ROWS, D), F32)),
        (pl.BlockSpec((8, 128), lambda i: (0, 0)), _rb(h1), _rb(h1), pl.BlockSpec((GAIN_ROWS, D), lambda i: (0, 0))))


def post_mix_bwd(h1, d_hn2, g_pre, d_h2, mix, g_post):
    lq = h1.shape[0]

    def body(h_ref, dn_ref, g3_ref, dh2_ref, m_ref, g2_ref, dh1_ref, dm_ref, dg3_ref, dg2_ref):
        i = pl.program_id(0)

        @pl.when(i == 0)
        def _():
            dg3_ref[...] = jnp.zeros_like(dg3_ref)
            dg2_ref[...] = jnp.zeros_like(dg2_ref)

        hhat, r = _rms_hat(h_ref[...])
        dn = dn_ref[...]
        d_h1 = dh2_ref[...] + _rms_bwd(hhat, r, g3_ref[...], dn)
        dh1_ref[...] = d_h1
        dg3_ref[...] += jnp.sum(dn * hhat, axis=0, keepdims=True)
        mhat, rm = _rms_hat(m_ref[...])
        dm_ref[...] = _rms_bwd(mhat, rm, g2_ref[...], d_h1).astype(BF16)
        dg2_ref[...] += jnp.sum(d_h1 * mhat, axis=0, keepdims=True)

    vec = jax.ShapeDtypeStruct((GAIN_ROWS, D), F32)
    vspec = pl.BlockSpec((GAIN_ROWS, D), lambda i: (0, 0))
    return _rows_call(body, "post_mix_bwd", lq, (h1, d_hn2, g_pre, d_h2, mix, g_post),
                      [_rb(h1), _rb(d_hn2), _whole(g_pre), _rb(d_h2), _rb(mix), _whole(g_post)],
                      (jax.ShapeDtypeStruct((lq, D), F32), jax.ShapeDtypeStruct((lq, D), BF16), vec, vec),
                      (_rb(h1), _rb(h1), vspec, vspec))


def pre_mix_bwd(hp, d_hn1, g_pre, d_h1):
    lq = hp.shape[0]

    def body(h_ref, dn_ref, g_ref, dh1_ref, dhp_ref, dg_ref):
        i = pl.program_id(0)

        @pl.when(i == 0)
        def _():
            dg_ref[...] = jnp.zeros_like(dg_ref)

        hhat, r = _rms_hat(h_ref[...])
        dn = dn_ref[...]
        dhp_ref[...] = dh1_ref[...] + _rms_bwd(hhat, r, g_ref[...], dn)
        dg_ref[...] += jnp.sum(dn * hhat, axis=0, keepdims=True)

    return _rows_call(body, "pre_mix_bwd", lq, (hp, d_hn1, g_pre, d_h1),
                      [_rb(hp), _rb(d_hn1), _whole(g_pre), _rb(d_h1)],
                      (jax.ShapeDtypeStruct((lq, D), F32), jax.ShapeDtypeStruct((GAIN_ROWS, D), F32)),
                      (_rb(hp), pl.BlockSpec((GAIN_ROWS, D), lambda i: (0, 0))))


def _retention_tables():
    log_g = jnp.log1p(-(2.0 ** (-5.0 - jnp.arange(RH, dtype=F32))))
    idx = jnp.arange(CHUNK, dtype=F32)
    diff = idx[:, None] - idx[None, :]
    decay = jnp.where(diff >= 0, jnp.exp(log_g[:, None, None] * jnp.maximum(diff, 0.0)), 0.0)
    zeta = jnp.exp(log_g[:, None] * (CHUNK - 1.0 - idx))
    xi = jnp.exp(log_g[:, None] * (idx + 1.0))
    g_chunk = jnp.broadcast_to(jnp.exp(log_g * CHUNK)[:, None], (RH, CHUNK))
    coef = jnp.stack([xi, zeta, g_chunk] + [jnp.zeros_like(xi)] * 125, axis=-1)
    return decay, coef


def retention_fwd(qr, kr, proj, decay, coef):
    lq = qr.shape[0]
    n = lq // CHUNK

    def body(q_ref, k_ref, v_ref, dec_ref, cf_ref, o_ref, st_ref, state):
        @pl.when(pl.program_id(0) == 0)
        def _():
            state[...] = jnp.zeros_like(state)

        for h in range(RH):
            qk, vv = slice(h * RDK, (h + 1) * RDK), slice(h * RDV, (h + 1) * RDV)
            q, k, v = q_ref[:, qk], k_ref[:, qk], v_ref[:, vv]
            xi, zeta, gch = cf_ref[h, :, 0:1], cf_ref[h, :, 1:2], cf_ref[h, 0:1, 2:3]
            st = state[h]
            stb = st.astype(BF16)
            st_ref[h] = stb
            s = _dot_nt(q, k) * dec_ref[h]
            o_ref[:, vv] = _dot(s.astype(BF16), v) + _dot(q, stb) * xi
            vz = (v.astype(F32) * zeta).astype(BF16)
            state[h] = gch * st + _dot_tn(k, vz)

    return pl.pallas_call(
        body, name="retention_fwd", grid=(n,),
        in_specs=[pl.BlockSpec((CHUNK, RH * RDK), lambda c: (c, 0)), pl.BlockSpec((CHUNK, RH * RDK), lambda c: (c, 0)),
                  pl.BlockSpec((CHUNK, RH * RDV), lambda c: (c, C_RV // (RH * RDV))),
                  pl.BlockSpec((RH, CHUNK, CHUNK), lambda c: (0, 0, 0)),
                  pl.BlockSpec((RH, CHUNK, 128), lambda c: (0, 0, 0))],
        out_specs=(pl.BlockSpec((CHUNK, RH * RDV), lambda c: (c, 0)),
                   pl.BlockSpec((None, RH, RDK, RDV), lambda c: (c, 0, 0, 0))),
        out_shape=(jax.ShapeDtypeStruct((lq, RH * RDV), F32), jax.ShapeDtypeStruct((n, RH, RDK, RDV), BF16)),
        scratch_shapes=[pltpu.VMEM((RH, RDK, RDV), F32)],
        compiler_params=_params("arbitrary"),
    )(qr, kr, proj, decay, coef)


def retention_bwd(qr, kr, proj, d_o, states, decay, coef, d_proj):
    lq = qr.shape[0]
    n = lq // CHUNK

    def body(q_ref, k_ref, v_ref, do_ref, st_ref, dec_ref, cf_ref, _, dq_ref, dk_ref, dv_ref, dstate):
        @pl.when(pl.program_id(0) == 0)
        def _():
            dstate[...] = jnp.zeros_like(dstate)

        for h in range(RH):
            qk, vv = slice(h * RDK, (h + 1) * RDK), slice(h * RDV, (h + 1) * RDV)
            q, k, v, dob = q_ref[:, qk], k_ref[:, qk], v_ref[:, vv], do_ref[:, vv]
            xi, zeta, gch = cf_ref[h, :, 0:1], cf_ref[h, :, 1:2], cf_ref[h, 0:1, 2:3]
            dec = dec_ref[h]
            dsn = dstate[h]
            dsnb = dsn.astype(BF16)
            dox = (dob.astype(F32) * xi).astype(BF16)
            sb = (_dot_nt(q, k) * dec).astype(BF16)
            dsb = (_dot_nt(dob, v) * dec).astype(BF16)
            vz = (v.astype(F32) * zeta).astype(BF16)
            dq_ref[:, qk] = _dot(dsb, k) + _dot_nt(dox, st_ref[h])
            dk_ref[:, qk] = _dot_tn(dsb, q) + _dot_nt(vz, dsnb)
            dv_ref[:, vv] = (_dot_tn(sb, dob) + _dot(k, dsnb) * zeta).astype(BF16)
            dstate[h] = gch * dsn + _dot_tn(q, dox)

    rev = lambda c: n - 1 - c
    qk_spec = pl.BlockSpec((CHUNK, RH * RDK), lambda c: (rev(c), 0))
    v_spec = pl.BlockSpec((CHUNK, RH * RDV), lambda c: (rev(c), 0))
    v_cols = pl.BlockSpec((CHUNK, RH * RDV), lambda c: (rev(c), C_RV // (RH * RDV)))
    return pl.pallas_call(
        body, name="retention_bwd", grid=(n,),
        in_specs=[qk_spec, qk_spec, v_cols, v_spec,
                  pl.BlockSpec((None, RH, RDK, RDV), lambda c: (rev(c), 0, 0, 0)),
                  pl.BlockSpec((RH, CHUNK, CHUNK), lambda c: (0, 0, 0)),
                  pl.BlockSpec((RH, CHUNK, 128), lambda c: (0, 0, 0)), ANY_SPEC],
        out_specs=(qk_spec, qk_spec, v_cols),
        out_shape=(jax.ShapeDtypeStruct((lq, RH * RDK), F32), jax.ShapeDtypeStruct((lq, RH * RDK), F32),
                   _d_proj_shape(lq)),
        scratch_shapes=[pltpu.VMEM((RH, RDK, RDV), F32)],
        input_output_aliases={7: 2}, compiler_params=_params("arbitrary"),
    )(qr, kr, proj, d_o, states, decay, coef, d_proj)


def _sb_masks_and_u():
    lane = lax.broadcasted_iota(jnp.int32, (1, 2 * SB_DH), 1)
    lo = lane < SB_DH
    row = lax.broadcasted_iota(jnp.int32, (SB_T, SB_T), 0)
    col = lax.broadcasted_iota(jnp.int32, (SB_T, SB_T), 1)
    u = (row > col).astype(BF16)
    return lo, u, row, col


def _sb_rows(j):
    start = j * SB_T
    return pl.ds(start if isinstance(j, int) else pl.multiple_of(start, SB_T), SB_T)


SB_DEAD = -104.0
SB_ROWS = SB_T
SB_CHAINS = tuple((hh, slice(r, r + SB_ROWS)) for hh in range(2) for r in range(0, SB_T, SB_ROWS))


def _sb_alive(logs):
    m = logs[0]
    for l in logs[1:]:
        m = jnp.maximum(m, l)
    return (jnp.max(m) > SB_DEAD).astype(jnp.int32)


def _sb_walk(i, run, logs_of, step):
    def cond(c):
        return (c[0] >= 1) & (c[1] > 0)

    def body(c):
        r = step(c[0], c[2])
        return c[0] - 1, _sb_alive(logs_of(r)), r

    return lax.while_loop(cond, body, (i - 1, _sb_alive(logs_of(run)), run))


def _sb_logs(z):
    ls = jnp.minimum(z, 0.0) - jnp.log(1.0 + jnp.exp(-jnp.abs(z)))
    return ls, ls - z


def sb_fwd(proj, exch):
    lq = proj.shape[0]
    t = SB_T

    def body(q_ref, k_ref, v_ref, o_ref, of_ref, acc_ref):
        i = pl.program_id(1)
        lo, u, row, col = _sb_masks_and_u()
        qs = (q_ref[...].astype(F32) * SB_DH ** -0.5).astype(BF16)
        zero = jnp.zeros_like(qs)
        qh = (jnp.where(lo, qs, zero), jnp.where(lo, zero, qs))
        acc_ref[...] = jnp.zeros_like(acc_ref)

        def block(j, run, masked):
            rows = _sb_rows(j)
            ks, vs = k_ref[rows, :], v_ref[rows, :]
            if masked:
                valid = (col + j * t < row + i * t) & (col + j * t >= META0)
            out = []
            for hh, rs in SB_CHAINS:
                ls, ln = _sb_logs(_dot_nt(qh[hh][rs], ks))
                if masked:
                    ln = jnp.where(valid[rs], ln, 0.0)
                a = jnp.exp(ls + _dot(ln.astype(BF16), u) + run[len(out)])
                if masked:
                    a = jnp.where(valid[rs], a, 0.0)
                acc_ref[hh, rs, :] += _dot(a.astype(BF16), vs)
                out.append(run[len(out)] + jnp.sum(ln, axis=1, keepdims=True))
            return tuple(out)

        zeros = jnp.zeros((SB_ROWS, 1), F32)
        run = block(i, (zeros,) * len(SB_CHAINS), True)
        _, go, run = _sb_walk(i, run, lambda r: r, lambda j, r: block(j, r, False))

        @pl.when((i > 0) & (go > 0))
        def _():
            block(0, run, True)

        o = jnp.where(lo, acc_ref[0], acc_ref[1])
        o_ref[...] = o.astype(BF16)
        of_ref[...] = o

    blk = pl.BlockSpec((t, 128), lambda p, i: (i, p))
    return _grid_call(
        body, "sb_fwd", (D // 128, lq // t), (proj, proj, proj),
        [pl.BlockSpec((t, 128), lambda p, i: (i, C_SQ // 128 + p)),
         pl.BlockSpec((lq, 128), lambda p, i: (0, C_SK // 128 + p)),
         pl.BlockSpec((lq, 128), lambda p, i: (0, C_SV // 128 + p))],
        (jax.ShapeDtypeStruct((lq, D), BF16), jax.ShapeDtypeStruct((lq, D), F32)), (blk, blk),
        [pltpu.VMEM((2, t, 128), F32)], exch)


def sb_bwd(proj, o, d_o, d_proj, exch):
    lq = proj.shape[0]
    t = SB_T
    last = lq // t - 1

    def body(q_ref, k_ref, v_ref, o_ref, do_ref, _, dq_ref, dk_out, dv_out, acc_ref, dk_ref, dv_ref):
        i = pl.program_id(1)

        @pl.when(i == 0)
        def _():
            dk_ref[...] = jnp.zeros_like(dk_ref)
            dv_ref[...] = jnp.zeros_like(dv_ref)

        lo, u, row, col = _sb_masks_and_u()
        incl = (row <= col).astype(BF16)
        qs = (q_ref[...].astype(F32) * SB_DH ** -0.5).astype(BF16)
        do = do_ref[...]
        zero = jnp.zeros_like(qs)
        qh = (jnp.where(lo, qs, zero), jnp.where(lo, zero, qs))
        doh = (jnp.where(lo, do, zero), jnp.where(lo, zero, do))
        prod = o_ref[...] * do.astype(F32)
        dsum = (jnp.sum(jnp.where(lo, prod, 0.0), axis=1, keepdims=True),
                jnp.sum(jnp.where(lo, 0.0, prod), axis=1, keepdims=True))
        acc_ref[...] = jnp.zeros_like(acc_ref)

        def block(j, run, masked):
            rows = _sb_rows(j)
            ks, vs = k_ref[rows, :], v_ref[rows, :]
            if masked:
                valid = (col + j * t < row + i * t) & (col + j * t >= META0)
            out = []
            for hh in range(2):
                run_ln, run_e = run[2 * hh], run[2 * hh + 1]
                ls, ln = _sb_logs(_dot_nt(qh[hh], ks))
                if masked:
                    ln = jnp.where(valid, ln, 0.0)
                a = jnp.exp(ls + _dot(ln.astype(BF16), u) + run_ln)
                if masked:
                    a = jnp.where(valid, a, 0.0)
                ab = a.astype(BF16)
                e = ab.astype(F32) * _dot_nt(doh[hh], vs)
                e_sum = jnp.sum(e, axis=1, keepdims=True)
                upto = (dsum[hh] - run_e - e_sum) + _dot(e.astype(BF16), incl)
                dz = e - jnp.exp(ls) * upto
                if masked:
                    dz = jnp.where(valid, dz, 0.0)
                dzb = dz.astype(BF16)
                acc_ref[hh] += _dot(dzb, ks)
                dk_ref[rows, :] += _dot_tn(dzb, qh[hh])
                dv_ref[rows, :] += _dot_tn(ab, doh[hh])
                out += [run_ln + jnp.sum(ln, axis=1, keepdims=True), run_e + e_sum]
            return tuple(out)

        zeros = jnp.zeros((t, 1), F32)
        run = block(i, (zeros,) * 4, True)
        _, go, run = _sb_walk(i, run, lambda r: (r[0], r[2]), lambda j, r: block(j, r, False))

        @pl.when((i > 0) & (go > 0))
        def _():
            block(0, run, True)

        dq_ref[...] = (jnp.where(lo, acc_ref[0], acc_ref[1]) * SB_DH ** -0.5).astype(BF16)

        @pl.when(i == last)
        def _():
            dk_out[...] = dk_ref[...].astype(BF16)
            dv_out[...] = dv_ref[...].astype(BF16)

    blk = pl.BlockSpec((t, 128), lambda p, i: (i, p))
    q_cols = pl.BlockSpec((t, 128), lambda p, i: (i, C_SQ // 128 + p))
    col_blk = pl.BlockSpec((lq, 128), lambda p, i: (0, p))
    return _grid_call(
        body, "sb_bwd", (D // 128, lq // t), (proj, proj, proj, o, d_o, d_proj),
        [q_cols, pl.BlockSpec((lq, 128), lambda p, i: (0, C_SK // 128 + p)),
         pl.BlockSpec((lq, 128), lambda p, i: (0, C_SV // 128 + p)), blk, blk, ANY_SPEC],
        (_d_proj_shape(lq), jax.ShapeDtypeStruct((lq, D), BF16), jax.ShapeDtypeStruct((lq, D), BF16)),
        (q_cols, col_blk, col_blk),
        [pltpu.VMEM((2, t, 128), F32), pltpu.VMEM((lq, 128), F32), pltpu.VMEM((lq, 128), F32)], exch, {5: 0})


def all_gather(blocks):
    nb = len(blocks)

    def body(*refs):
        x_refs, out_refs = refs[:nb], refs[nb:2 * nb]
        send_sems, recv_sems, local_sems = refs[2 * nb:]
        x, y, c = lax.axis_index("x"), lax.axis_index("y"), lax.axis_index("c")
        me, sibling = (x, y, c), (x, y, 1 - c)
        chips = [(1 - x, y), (x, 1 - y), (1 - x, 1 - y)]

        def copy(b, k, block, to, src=None):
            slot = out_refs[b].at[_device_index(*block)]
            return pltpu.make_async_remote_copy(
                src_ref=slot if src is None else src, dst_ref=slot,
                send_sem=send_sems.at[b, k], recv_sem=recv_sems.at[b, k],
                device_id=to, device_id_type=pl.DeviceIdType.MESH)

        mine = [pltpu.make_async_copy(x_refs[b], out_refs[b].at[_device_index(*me)], local_sems.at[b])
                for b in range(nb)]
        for cp in mine:
            cp.start()
        first = []
        for b in range(nb):
            first.append(copy(b, 0, me, sibling, src=x_refs[b]))
            first += [copy(b, 1 + j, me, (*chip, c), src=x_refs[b]) for j, chip in enumerate(chips)]
        for cp in first:
            cp.start()
        passed = []
        for j, chip in enumerate(chips):
            for b in range(nb):
                copy(b, 1 + j, (*chip, c), me).wait_recv()
                cp = copy(b, 4 + j, (*chip, c), sibling)
                cp.start()
                passed.append(cp)
        for b in range(nb):
            copy(b, 0, sibling, me).wait_recv()
            for j, chip in enumerate(chips):
                copy(b, 4 + j, (*chip, 1 - c), me).wait_recv()
        for cp in first + passed:
            cp.wait_send()
        for cp in mine:
            cp.wait()

    any_spec = pl.BlockSpec(memory_space=pl.ANY)
    return pl.pallas_call(
        body, name="all_gather",
        in_specs=[any_spec] * nb, out_specs=[any_spec] * nb,
        out_shape=[jax.ShapeDtypeStruct((NDEV,) + b.shape, b.dtype) for b in blocks],
        scratch_shapes=[pltpu.SemaphoreType.DMA((nb, 7)), pltpu.SemaphoreType.DMA((nb, 7)),
                        pltpu.SemaphoreType.DMA((nb,))],
    )(*blocks)


def gather_small(small):
    def body(s_ref, land_ref, send_sems, recv_sems, local_sem):
        start, wait = _direct_exchange(lambda d: s_ref, land_ref, send_sems, recv_sems, local_sem)
        start()
        wait()

    any_spec = pl.BlockSpec(memory_space=pl.ANY)
    return pl.pallas_call(
        body, name="gather_small", in_specs=[any_spec], out_specs=any_spec,
        out_shape=jax.ShapeDtypeStruct((NDEV,) + small.shape, small.dtype),
        scratch_shapes=[pltpu.SemaphoreType.DMA((NDEV - 1,)), pltpu.SemaphoreType.DMA((NDEV - 1,)),
                        pltpu.SemaphoreType.DMA(())],
    )(small)


def sum_slots(landed, name):
    _, r, c = landed.shape
    tr = _tile(r, (352, 224, 8))

    def body(l_ref, o_ref):
        acc = l_ref[0].astype(F32)
        for p in range(1, NDEV):
            acc = acc + l_ref[p].astype(F32)
        o_ref[...] = acc

    return pl.pallas_call(
        body, name=name, grid=(r // tr,),
        in_specs=[pl.BlockSpec((NDEV, tr, c), lambda i: (0, i, 0))],
        out_specs=pl.BlockSpec((tr, c), lambda i: (i, 0)),
        out_shape=jax.ShapeDtypeStruct((r, c), F32),
        compiler_params=_params("parallel"),
    )(landed)


def adamw(w, g, m, v, name):
    r, c = w.shape
    tr = _tile(r, (256, 128))

    def body(w_ref, g_ref, m_ref, v_ref, d_ref, nm_ref, nv_ref):
        g_ = g_ref[...]
        m_ = ADAM_B1 * m_ref[...] + (1.0 - ADAM_B1) * g_
        v_ = ADAM_B2 * v_ref[...] + (1.0 - ADAM_B2) * jnp.square(g_)
        m_hat = m_ / (1.0 - ADAM_B1 ** ADAM_STEP)
        v_hat = v_ / (1.0 - ADAM_B2 ** ADAM_STEP)
        d_ref[...] = -ADAM_LR * (m_hat / (jnp.sqrt(v_hat) + ADAM_EPS) + ADAM_WD * w_ref[...])
        nm_ref[...] = m_
        nv_ref[...] = v_

    spec = pl.BlockSpec((tr, c), lambda i: (i, 0))
    out = jax.ShapeDtypeStruct((r, c), F32)
    return pl.pallas_call(
        body, name=name, grid=(r // tr,), in_specs=[spec] * 4, out_specs=(spec,) * 3, out_shape=(out,) * 3,
        compiler_params=_params("parallel"),
    )(w, g, m, v)


def kernel(x, meta_tokens, w_in, w_ret_out, w_sb_out, w_out, w_ffn_in, w_ffn_out, norm_mix_pre, norm_mix_post, norm_ffn_pre, norm_ffn_post, loss_target, m_meta_tokens, m_w_in, m_w_ret_out, m_w_sb_out, m_w_out, m_w_ffn_in, m_w_ffn_out, m_norm_mix_pre, m_norm_mix_post, m_norm_ffn_pre, m_norm_ffn_post, v_meta_tokens, v_w_in, v_w_ret_out, v_w_sb_out, v_w_out, v_w_ffn_in, v_w_ffn_out, v_norm_mix_pre, v_norm_mix_post, v_norm_ffn_pre, v_norm_ffn_post):
    seq = x.shape[1]
    lq = seq + FRONT
    me = _device_index(lax.axis_index("x"), lax.axis_index("y"), lax.axis_index("c"))

    shards = {"w_in_t": w_in[0].T, "w_ffn_in_t": w_ffn_in[0].T, "w_ret_out": w_ret_out[0],
              "w_sb_out": w_sb_out[0], "w_out": w_out[0], "w_ffn_out": w_ffn_out[0]}
    rest_rows = PACK_ROWS[1:]
    pack_rest = jnp.concatenate([shards[n].astype(BF16) for n, _ in rest_rows], axis=0)
    gathered_in, meta_all = all_gather([shards["w_in_t"].astype(BF16), meta_tokens])
    full = {"w_in_t": gathered_in.reshape(PROJ, D)}
    meta_full = meta_all.transpose(1, 0, 2).reshape(N_META, D)

    pos = jnp.arange(lq, dtype=F32) - META0
    half = RDK // 2
    ang = pos[:, None] * (ROPE_BASE ** (-jnp.arange(half, dtype=F32) / half))[None, :]
    cos, sin = jnp.cos(ang), jnp.sin(ang)
    decay, coef = _retention_tables()

    hp = jnp.concatenate([jnp.zeros((META0, D), F32), meta_full, x[0]], axis=0)
    hn1 = rms_fwd(hp, norm_mix_pre, "rms_mix_pre")
    proj = mm_nt(hn1, full["w_in_t"], BF16, "proj")
    qr, kr = rotary_fwd(proj, cos, sin)
    o_ret, states = retention_fwd(qr, kr, proj, decay, coef)
    gr = ret_gate_fwd(proj, o_ret)
    o_sb, o_sb_f32, gathered_rest = sb_fwd(proj, (pack_rest, "gather"))
    off = 0
    for n, r in rest_rows:
        full[n] = gathered_rest[:, off:off + r, :].reshape(NDEV * r, D)
        off += r
    y_ret = mm_nn(gr, full["w_ret_out"], F32, "y_ret")
    y_sb = mm_nn(o_sb, full["w_sb_out"], F32, "y_sb")
    merged = merge_fwd(proj, y_ret, y_sb)
    mix = mm_nn(merged, full["w_out"], F32, "mix")
    h1, hn2 = post_mix_fwd(hp, mix, norm_mix_post, norm_ffn_pre)
    ab = mm_nt(hn2, full["w_ffn_in_t"], BF16, "ffn_in")
    act = swiglu_fwd(ab)
    ff = mm_nn(act, full["w_ffn_out"], F32, "ffn_out")
    loss_blk, d_h2, d_ff, dg_ffn_post = loss_head(h1, ff, norm_ffn_post, loss_target[0])
    loss = lax.psum(loss_blk[0, 0], ("x", "y", "c"))

    grads = {}
    d_act = mm_nt(d_ff, full["w_ffn_out"], BF16, "d_act")
    grads["w_ffn_out"] = mm_tn(act, d_ff, BF16, "dw_ffn_out")
    d_ab = swiglu_bwd(ab, d_act)
    d_hn2 = mm_nn(d_ab, full["w_ffn_in_t"], F32, "d_hn2")
    grads["w_ffn_in_t"] = mm_tn(d_ab, hn2, BF16, "dw_ffn_in")
    d_h1, d_mix, dg_ffn_pre, dg_mix_post = post_mix_bwd(h1, d_hn2, norm_ffn_pre, d_h2, mix, norm_mix_post)
    d_merged = mm_nt(d_mix, full["w_out"], BF16, "d_merged")
    grads["w_out"] = mm_tn(merged, d_mix, BF16, "dw_out")
    d_y_ret, d_y_sb, d_proj = merge_bwd(proj, y_ret, y_sb, d_merged)
    d_gr = mm_nt(d_y_ret, full["w_ret_out"], BF16, "d_gr")
    grads["w_ret_out"] = mm_tn(gr, d_y_ret, BF16, "dw_ret_out")
    d_o_sb = mm_nt(d_y_sb, full["w_sb_out"], BF16, "d_o_sb")
    grads["w_sb_out"] = mm_tn(o_sb, d_y_sb, BF16, "dw_sb_out")
    d_proj, d_o_ret = ret_gate_bwd(proj, o_ret, d_gr, d_proj)
    d_qr, d_kr, d_proj = retention_bwd(qr, kr, proj, d_o_ret, states, decay, coef, d_proj)
    d_proj = rotary_bwd(d_qr, d_kr, cos, sin, d_proj)
    parts_rest = jnp.concatenate([grads[n].reshape(NDEV, r, D) for n, r in rest_rows], axis=1)
    d_proj, d_sk, d_sv, landed_rest = sb_bwd(proj, o_sb_f32, d_o_sb, d_proj, (parts_rest, "scatter"))
    d_proj = lax.dynamic_update_slice(d_proj, d_sk, (0, C_SK))
    d_proj = lax.dynamic_update_slice(d_proj, d_sv, (0, C_SV))
    dw_in_t = mm_tn(d_proj, hn1, BF16, "dw_in")
    d_hn1, landed_in = mm_nn(d_proj, full["w_in_t"], F32, "d_hn1",
                             exch=(dw_in_t.reshape(NDEV, PROJ // NDEV, D), "scatter"))
    d_hp, dg_mix_pre = pre_mix_bwd(hp, d_hn1, norm_mix_pre, d_h1)
    grad_x = d_hp[FRONT:][None]
    small = jnp.concatenate([dg_mix_pre, dg_mix_post, dg_ffn_pre, dg_ffn_post, d_hp[META0:FRONT]], axis=0)
    ssum = sum_slots(gather_small(small), "sum_small")
    g = {"w_in_t": sum_slots(landed_in, "sum_grads_in")}
    gsum = sum_slots(landed_rest, "sum_grads_rest")
    off = 0
    for n, r in rest_rows:
        g[n] = gsum[off:off + r]
        off += r
    gain = lambda k: ssum[k * GAIN_ROWS:k * GAIN_ROWS + 1]
    g_w = {"meta_tokens": lax.dynamic_slice(ssum[4 * GAIN_ROWS:], (0, me * (D // NDEV)), (N_META, D // NDEV)),
           "w_in": g["w_in_t"].T[None], "w_ret_out": g["w_ret_out"][None], "w_sb_out": g["w_sb_out"][None],
           "w_out": g["w_out"][None], "w_ffn_in": g["w_ffn_in_t"].T[None], "w_ffn_out": g["w_ffn_out"][None],
           "norm_mix_pre": gain(0), "norm_mix_post": gain(1), "norm_ffn_pre": gain(2), "norm_ffn_post": gain(3)}

    names = ["meta_tokens", "w_in", "w_ret_out", "w_sb_out", "w_out", "w_ffn_in", "w_ffn_out",
             "norm_mix_pre", "norm_mix_post", "norm_ffn_pre", "norm_ffn_post"]
    w_of = dict(zip(names, (meta_tokens, w_in, w_ret_out, w_sb_out, w_out, w_ffn_in, w_ffn_out,
                            norm_mix_pre, norm_mix_post, norm_ffn_pre, norm_ffn_post)))
    m_of = dict(zip(names, (m_meta_tokens, m_w_in, m_w_ret_out, m_w_sb_out, m_w_out, m_w_ffn_in, m_w_ffn_out,
                            m_norm_mix_pre, m_norm_mix_post, m_norm_ffn_pre, m_norm_ffn_post)))
    v_of = dict(zip(names, (v_meta_tokens, v_w_in, v_w_ret_out, v_w_sb_out, v_w_out, v_w_ffn_in, v_w_ffn_out,
                            v_norm_mix_pre, v_norm_mix_post, v_norm_ffn_pre, v_norm_ffn_post)))
    delta, new_m, new_v = {}, {}, {}
    for n in names:
        shape = w_of[n].shape
        two_d = (shape[-2], shape[-1])
        d_, m_, v_ = adamw(w_of[n].reshape(two_d), g_w[n].reshape(two_d), m_of[n].reshape(two_d),
                           v_of[n].reshape(two_d), "adamw_" + n)
        delta[n], new_m[n], new_v[n] = d_.reshape(shape), m_.reshape(shape), v_.reshape(shape)

    return (loss, grad_x, *[g_w[n] for n in names], *[delta[n] for n in names],
            *[new_m[n] for n in names], *[new_v[n] for n in names])
```

```python
import jax
import jax.numpy as jnp
from jax import lax
from jax.experimental import pallas as pl
from jax.experimental.pallas import tpu as pltpu

F32 = jnp.float32
BF16 = jnp.bfloat16

D = 1024
N_META = 16
CHUNK = 128
FRONT = 256
META0 = FRONT - N_META
RH, RDK, RDV = 4, 256, 512
SB_DH = 64
DFF = 2816
NDEV = 8
ROPE_BASE = 10000.0
NORM_EPS = 1e-6
GN_EPS = 1e-5
C_RQ, C_RK, C_RV, C_RG, C_SQ, C_SK, C_SV, C_GA, C_GB = 0, 1024, 2048, 4096, 6144, 7168, 8192, 9216, 10240
PROJ = 11264
PACK_ROWS = (("w_in_t", PROJ // NDEV), ("w_ffn_in_t", 2 * DFF // NDEV), ("w_ret_out", RH * RDV // NDEV),
             ("w_sb_out", D // NDEV), ("w_out", D // NDEV), ("w_ffn_out", DFF // NDEV))
PACK = sum(r for _, r in PACK_ROWS)
GAIN_ROWS = 8

ADAM_LR = 0.001
ADAM_B1 = 0.9
ADAM_B2 = 0.999
ADAM_EPS = 1e-08
ADAM_WD = 0.01
ADAM_STEP = 10

VMEM_LIMIT = 56 * 1024 * 1024
SB_T = 256

NT = (((1,), (1,)), ((), ()))
TN = (((0,), (0,)), ((), ()))


def _dot(a, b):
    return jnp.dot(a, b, preferred_element_type=F32)


def _dot_nt(a, b):
    return lax.dot_general(a, b, NT, preferred_element_type=F32)


def _dot_tn(a, b):
    return lax.dot_general(a, b, TN, preferred_element_type=F32)


WIDE_TILES = (1024, 1408, 512, 256)


def _tile(n, candidates):
    for t in candidates:
        if n % t == 0:
            return t
    return n


def _params(*sem):
    return pltpu.CompilerParams(dimension_semantics=sem, vmem_limit_bytes=VMEM_LIMIT)


def _rms_hat(x):
    r = lax.rsqrt(jnp.mean(x * x, axis=-1, keepdims=True) + NORM_EPS)
    return x * r, r


def _rms_bwd(xhat, r, g, dy):
    u = dy * g
    return r * (u - xhat * jnp.mean(u * xhat, axis=-1, keepdims=True))


def _gn(y):
    mu = jnp.mean(y, axis=-1, keepdims=True)
    yc = y - mu
    rs = lax.rsqrt(jnp.mean(yc * yc, axis=-1, keepdims=True) + GN_EPS)
    return yc * rs, rs


def _gn_bwd(yh, rs, d):
    return rs * (d - jnp.mean(d, axis=-1, keepdims=True) - yh * jnp.mean(d * yh, axis=-1, keepdims=True))


def _sigmoid(x):
    return 1.0 / (1.0 + jnp.exp(-x))


def _device_index(px, py, pc):
    return 4 * px + 2 * py + pc


def _direct_exchange(src_for, land_ref, send_sems, recv_sems, local_sem):
    x, y, c = lax.axis_index("x"), lax.axis_index("y"), lax.axis_index("c")
    me = _device_index(x, y, c)
    peers = []
    for k in range(1, NDEV):
        pos = (1 - x if k & 4 else x, 1 - y if k & 2 else y, 1 - c if k & 1 else c)
        peers.append((k - 1, pos, _device_index(*pos)))

    def local():
        return pltpu.make_async_copy(src_for(me), land_ref.at[me], local_sem)

    def remote(k, pos, idx):
        return pltpu.make_async_remote_copy(
            src_ref=src_for(idx), dst_ref=land_ref.at[me], send_sem=send_sems.at[k], recv_sem=recv_sems.at[k],
            device_id=pos, device_id_type=pl.DeviceIdType.MESH)

    def arrival(k, idx):
        return pltpu.make_async_remote_copy(
            src_ref=src_for(idx), dst_ref=land_ref.at[idx], send_sem=send_sems.at[k], recv_sem=recv_sems.at[k],
            device_id=(x, y, c), device_id_type=pl.DeviceIdType.MESH)

    def start():
        local().start()
        for p in peers:
            remote(*p).start()

    def wait():
        for k, _, idx in peers:
            arrival(k, idx).wait_recv()
        for p in peers:
            remote(*p).wait_send()
        local().wait()

    return start, wait


def _grid_call(compute, name, grid, arrays, in_specs, out_shapes, out_specs, scratch, exch, aliases=None):
    arrays, in_specs = list(arrays), list(in_specs)
    out_shapes, out_specs, scratch = list(out_shapes), list(out_specs), list(scratch)
    n_in, n_out, steps = len(arrays), len(out_shapes), grid[0] * grid[1]
    aliases = aliases or {}
    if exch is None:
        return pl.pallas_call(
            compute, name=name, grid=grid, in_specs=in_specs, out_specs=out_specs, out_shape=out_shapes,
            scratch_shapes=scratch, input_output_aliases=aliases,
            compiler_params=_params("parallel", "arbitrary"))(*arrays)
    src, mode = exch

    def body(*refs):
        refs = list(refs)
        src_ref = refs.pop(n_in)
        land_ref = refs.pop(n_in + n_out)
        send_sems, recv_sems, local_sem = refs[-3:]
        src_for = (lambda d: src_ref) if mode == "gather" else (lambda d: src_ref.at[d])
        start, wait = _direct_exchange(src_for, land_ref, send_sems, recv_sems, local_sem)
        step = pl.program_id(0) * grid[1] + pl.program_id(1)
        pl.when(step == 0)(start)
        compute(*refs[:-3])
        pl.when(step == steps - 1)(wait)

    any_spec = pl.BlockSpec(memory_space=pl.ANY)
    land = src.shape if mode == "scatter" else (NDEV,) + src.shape
    return pl.pallas_call(
        body, name=name, grid=grid, in_specs=in_specs + [any_spec], out_specs=out_specs + [any_spec],
        out_shape=out_shapes + [jax.ShapeDtypeStruct(land, src.dtype)],
        scratch_shapes=scratch + [pltpu.SemaphoreType.DMA((NDEV - 1,)), pltpu.SemaphoreType.DMA((NDEV - 1,)),
                                  pltpu.SemaphoreType.DMA(())],
        input_output_aliases=aliases, compiler_params=_params("arbitrary", "arbitrary"))(*(arrays + [src]))


def _mm_call(compute, name, grid, arrays, in_specs, out_shape, out_spec, acc_shape, exch):
    if acc_shape:
        body, scratch = compute, [pltpu.VMEM(acc_shape, F32)]
    else:
        body, scratch = (lambda a_ref, b_ref, o_ref: compute(a_ref, b_ref, o_ref, None)), []
    out = _grid_call(body, name, grid, arrays, in_specs, [out_shape], [out_spec], scratch, exch)
    return tuple(out) if exch else out[0]


def mm_nt(a, b, out_dtype, name, exch=None):
    m, k = a.shape
    n = b.shape[0]
    tm = _tile(m, (768, 512, 256))
    tn = _tile(n, WIDE_TILES)

    def compute(a_ref, b_ref, o_ref, acc_ref):
        o_ref[...] = _dot_nt(a_ref[...], b_ref[...]).astype(o_ref.dtype)

    return _mm_call(
        compute, name, (m // tm, n // tn), (a, b),
        [pl.BlockSpec((tm, k), lambda i, j: (i, 0)), pl.BlockSpec((tn, k), lambda i, j: (j, 0))],
        jax.ShapeDtypeStruct((m, n), out_dtype), pl.BlockSpec((tm, tn), lambda i, j: (i, j)), None, exch)


def _accumulate(dot, steps):
    def compute(a_ref, b_ref, o_ref, acc_ref):
        kk = pl.program_id(1)

        @pl.when(kk == 0)
        def _():
            acc_ref[...] = jnp.zeros_like(acc_ref)

        acc_ref[...] += dot(a_ref[...], b_ref[...])

        @pl.when(kk == steps - 1)
        def _():
            o_ref[...] = acc_ref[...].astype(o_ref.dtype)

    return compute


def mm_nn(a, b, out_dtype, name, exch=None):
    m, k = a.shape
    n = b.shape[1]
    tm = _tile(m, (768, 512, 256))
    tk = _tile(k, (2816,) + WIDE_TILES)
    return _mm_call(
        _accumulate(_dot, k // tk), name, (m // tm, k // tk), (a, b),
        [pl.BlockSpec((tm, tk), lambda i, kk: (i, kk)), pl.BlockSpec((tk, n), lambda i, kk: (kk, 0))],
        jax.ShapeDtypeStruct((m, n), out_dtype), pl.BlockSpec((tm, n), lambda i, kk: (i, 0)), (tm, n), exch)


def mm_tn(a, b, out_dtype, name, exch=None):
    m, ka = a.shape
    n = b.shape[1]
    ta = _tile(ka, WIDE_TILES)
    tl = _tile(m, (1408, 768, 512, 256))
    return _mm_call(
        _accumulate(_dot_tn, m // tl), name, (ka // ta, m // tl), (a, b),
        [pl.BlockSpec((tl, ta), lambda i, ll: (ll, i)), pl.BlockSpec((tl, n), lambda i, ll: (ll, 0))],
        jax.ShapeDtypeStruct((ka, n), out_dtype), pl.BlockSpec((ta, n), lambda i, ll: (i, 0)), (ta, n), exch)


TM = 384


def _rb(arr, width=None, col_block=0, tm=TM):
    w = arr.shape[1] if width is None else width
    return pl.BlockSpec((tm, w), lambda i: (i, col_block))


def _whole(arr):
    return pl.BlockSpec(arr.shape, lambda i: (0,) * arr.ndim)


def _rows_call(body, name, lq, ins, in_specs, out_shapes, out_specs, aliases=None, tm=TM):
    return pl.pallas_call(
        body, name=name, grid=(lq // tm,), in_specs=in_specs, out_specs=out_specs, out_shape=out_shapes,
        input_output_aliases=aliases or {}, compiler_params=_params("arbitrary"),
    )(*ins)


ANY_SPEC = pl.BlockSpec(memory_space=pl.ANY)


def _d_proj_shape(lq):
    return jax.ShapeDtypeStruct((lq, PROJ), BF16)


def rms_fwd(h, g, name):
    lq = h.shape[0]

    def body(h_ref, g_ref, o_ref):
        xhat, _ = _rms_hat(h_ref[...])
        o_ref[...] = (xhat * g_ref[...]).astype(BF16)

    return _rows_call(body, name, lq, (h, g), [_rb(h), _whole(g)],
                      jax.ShapeDtypeStruct((lq, D), BF16), _rb(h))


def rotary_fwd(proj, cos, sin):
    lq = proj.shape[0]
    half = RDK // 2

    def body(p_ref, c_ref, s_ref, q_ref, k_ref):
        c, s = c_ref[...], s_ref[...]
        for col, o_ref, scale in ((C_RQ, q_ref, RDK ** -0.5), (C_RK, k_ref, 1.0)):
            for h in range(RH):
                x1 = p_ref[:, col + h * RDK: col + h * RDK + half].astype(F32)
                x2 = p_ref[:, col + h * RDK + half: col + (h + 1) * RDK].astype(F32)
                o_ref[:, h * RDK: h * RDK + half] = ((x1 * c - x2 * s) * scale).astype(BF16)
                o_ref[:, h * RDK + half: (h + 1) * RDK] = ((x1 * s + x2 * c) * scale).astype(BF16)

    out = jax.ShapeDtypeStruct((lq, RH * RDK), BF16)
    return _rows_call(body, "rotary_fwd", lq, (proj, cos, sin),
                      [_rb(proj, 2 * RH * RDK, 0), _rb(cos), _rb(sin)],
                      (out, out), (pl.BlockSpec((TM, RH * RDK), lambda i: (i, 0)),) * 2)


def rotary_bwd(dq, dk, cos, sin, d_proj):
    lq = dq.shape[0]
    half = RDK // 2

    def body(dq_ref, dk_ref, c_ref, s_ref, _, o_ref):
        c, s = c_ref[...], s_ref[...]
        for col, d_ref, scale in ((C_RQ, dq_ref, RDK ** -0.5), (C_RK, dk_ref, 1.0)):
            for h in range(RH):
                d1 = d_ref[:, h * RDK: h * RDK + half]
                d2 = d_ref[:, h * RDK + half: (h + 1) * RDK]
                o_ref[:, col + h * RDK: col + h * RDK + half] = ((d1 * c + d2 * s) * scale).astype(BF16)
                o_ref[:, col + h * RDK + half: col + (h + 1) * RDK] = ((d2 * c - d1 * s) * scale).astype(BF16)

    return _rows_call(body, "rotary_bwd", lq, (dq, dk, cos, sin, d_proj),
                      [_rb(dq), _rb(dk), _rb(cos), _rb(sin), ANY_SPEC], _d_proj_shape(lq),
                      pl.BlockSpec((TM, 2 * RH * RDK), lambda i: (i, C_RQ // (2 * RH * RDK))), {4: 0})


def ret_gate_fwd(proj, o_ret):
    lq = proj.shape[0]

    def body(p_ref, y_ref, o_ref):
        for h in range(RH):
            sl = slice(h * RDV, (h + 1) * RDV)
            yh, _ = _gn(y_ref[:, sl])
            rg = p_ref[:, sl].astype(F32)
            o_ref[:, sl] = (rg * _sigmoid(rg) * yh).astype(BF16)

    return _rows_call(body, "ret_gate_fwd", lq, (proj, o_ret),
                      [_rb(proj, RH * RDV, C_RG // (RH * RDV)), _rb(o_ret)],
                      jax.ShapeDtypeStruct((lq, RH * RDV), BF16), _rb(o_ret))


def ret_gate_bwd(proj, o_ret, d_gr, d_proj):
    lq = proj.shape[0]

    def body(p_ref, y_ref, d_ref, _, drg_ref, dy_ref):
        for h in range(RH):
            sl = slice(h * RDV, (h + 1) * RDV)
            yh, rs = _gn(y_ref[:, sl])
            rg = p_ref[:, sl].astype(F32)
            sg = _sigmoid(rg)
            d = d_ref[:, sl].astype(F32)
            drg_ref[:, sl] = (d * yh * sg * (1.0 + rg * (1.0 - sg))).astype(BF16)
            dy_ref[:, sl] = _gn_bwd(yh, rs, d * rg * sg).astype(BF16)

    return _rows_call(body, "ret_gate_bwd", lq, (proj, o_ret, d_gr, d_proj),
                      [_rb(proj, RH * RDV, C_RG // (RH * RDV)), _rb(o_ret), _rb(d_gr), ANY_SPEC],
                      (_d_proj_shape(lq), jax.ShapeDtypeStruct((lq, RH * RDV), BF16)),
                      (_rb(o_ret, col_block=C_RG // (RH * RDV)), _rb(o_ret)), {3: 0})


def merge_fwd(proj, y_ret, y_sb):
    lq = proj.shape[0]

    def body(ga_ref, gb_ref, yr_ref, ys_ref, o_ref):
        o_ref[...] = (_sigmoid(ga_ref[...].astype(F32)) * yr_ref[...]
                      + _sigmoid(gb_ref[...].astype(F32)) * ys_ref[...]).astype(BF16)

    return _rows_call(body, "merge_fwd", lq, (proj, proj, y_ret, y_sb),
                      [_rb(proj, D, C_GA // D), _rb(proj, D, C_GB // D), _rb(y_ret), _rb(y_sb)],
                      jax.ShapeDtypeStruct((lq, D), BF16), _rb(y_ret))


def merge_bwd(proj, y_ret, y_sb, d_merged):
    lq = proj.shape[0]

    def body(gate_ref, yr_ref, ys_ref, d_ref, dyr_ref, dys_ref, dg_ref):
        d = d_ref[...].astype(F32)
        sg = _sigmoid(gate_ref[...].astype(F32))
        for g, y_ref, dy_ref in ((0, yr_ref, dyr_ref), (1, ys_ref, dys_ref)):
            @pl.when(pl.program_id(1) == g)
            def _():
                dy_ref[...] = (d * sg).astype(BF16)
                dg_ref[...] = (d * y_ref[...] * sg * (1.0 - sg)).astype(BF16)

    tm = 2 * TM
    row = pl.BlockSpec((tm, D), lambda i, g: (i, 0))
    o1 = jax.ShapeDtypeStruct((lq, D), BF16)
    return pl.pallas_call(
        body, name="merge_bwd", grid=(lq // tm, 2),
        in_specs=[pl.BlockSpec((tm, D), lambda i, g: (i, C_GA // D + g)), row, row, row],
        out_specs=(row, row, pl.BlockSpec((tm, D), lambda i, g: (i, C_GA // D + g))),
        out_shape=(o1, o1, _d_proj_shape(lq)),
        compiler_params=_params("arbitrary", "arbitrary"),
    )(proj, y_ret, y_sb, d_merged)


def post_mix_fwd(hp, mix, g_post, g_pre):
    lq = hp.shape[0]

    def body(h_ref, m_ref, g2_ref, g3_ref, h1_ref, hn_ref):
        mhat, _ = _rms_hat(m_ref[...])
        h1 = h_ref[...] + mhat * g2_ref[...]
        h1_ref[...] = h1
        hhat, _ = _rms_hat(h1)
        hn_ref[...] = (hhat * g3_ref[...]).astype(BF16)

    return _rows_call(body, "post_mix_fwd", lq, (hp, mix, g_post, g_pre),
                      [_rb(hp), _rb(mix), _whole(g_post), _whole(g_pre)],
                      (jax.ShapeDtypeStruct((lq, D), F32), jax.ShapeDtypeStruct((lq, D), BF16)),
                      (_rb(hp), _rb(hp)))


def swiglu_fwd(ab):
    lq = ab.shape[0]

    def body(a_ref, b_ref, o_ref):
        a = a_ref[...].astype(F32)
        o_ref[...] = (a * _sigmoid(a) * b_ref[...].astype(F32)).astype(BF16)

    return _rows_call(body, "swiglu_fwd", lq, (ab, ab), [_rb(ab, DFF, 0), _rb(ab, DFF, 1)],
                      jax.ShapeDtypeStruct((lq, DFF), BF16), pl.BlockSpec((TM, DFF), lambda i: (i, 0)))


def swiglu_bwd(ab, d_act):
    lq = ab.shape[0]

    def body(a_ref, b_ref, d_ref, o_ref):
        a = a_ref[...].astype(F32)
        b = b_ref[...].astype(F32)
        d = d_ref[...].astype(F32)
        sg = _sigmoid(a)
        o_ref[:, :DFF] = (d * b * sg * (1.0 + a * (1.0 - sg))).astype(BF16)
        o_ref[:, DFF:] = (d * a * sg).astype(BF16)

    return _rows_call(body, "swiglu_bwd", lq, (ab, ab, d_act), [_rb(ab, DFF, 0), _rb(ab, DFF, 1), _rb(d_act)],
                      jax.ShapeDtypeStruct((lq, 2 * DFF), BF16), pl.BlockSpec((TM, 2 * DFF), lambda i: (i, 0)))


def loss_head(h1, ff, g_post, target):
    lq = h1.shape[0]
    front_blocks = 1
    rb = lambda a: _rb(a, tm=FRONT)

    def body(h_ref, f_ref, g_ref, t_ref, loss_ref, dh_ref, df_ref, dg_ref):
        i = pl.program_id(0)

        @pl.when(i == 0)
        def _():
            loss_ref[...] = jnp.zeros_like(loss_ref)
            dg_ref[...] = jnp.zeros_like(dg_ref)

        g = g_ref[...]
        fhat, r = _rms_hat(f_ref[...])
        is_x = (i >= front_blocks).astype(F32)
        diff = (h_ref[...] + fhat * g - t_ref[...]) * is_x
        loss_ref[...] += 0.5 * jnp.sum(diff * diff) / D
        dy = diff / D
        dh_ref[...] = dy
        df_ref[...] = _rms_bwd(fhat, r, g, dy).astype(BF16)
        dg_ref[...] += jnp.sum(dy * fhat, axis=0, keepdims=True)

    return _rows_call(
        body, "loss_head", lq, (h1, ff, g_post, target),
        [rb(h1), rb(ff), _whole(g_post),
         pl.BlockSpec((FRONT, D), lambda i: (jnp.maximum(i - front_blocks, 0), 0))],
        (jax.ShapeDtypeStruct((8, 128), F32), jax.ShapeDtypeStruct((lq, D), F32),
         jax.ShapeDtypeStruct((lq, D), BF16), jax.ShapeDtypeStruct((GAIN_ROWS, D), F32)),
        (pl.BlockSpec((8, 128), lambda i: (0, 0)), rb(h1), rb(h1), pl.BlockSpec((GAIN_ROWS, D), lambda i: (0, 0))),
        tm=FRONT)


def post_mix_bwd(h1, d_hn2, g_pre, d_h2, mix, g_post):
    lq = h1.shape[0]

    def body(h_ref, dn_ref, g3_ref, dh2_ref, m_ref, g2_ref, dh1_ref, dm_ref, dg3_ref, dg2_ref):
        i = pl.program_id(0)

        @pl.when(i == 0)
        def _():
            dg3_ref[...] = jnp.zeros_like(dg3_ref)
            dg2_ref[...] = jnp.zeros_like(dg2_ref)

        hhat, r = _rms_hat(h_ref[...])
        dn = dn_ref[...]
        d_h1 = dh2_ref[...] + _rms_bwd(hhat, r, g3_ref[...], dn)
        dh1_ref[...] = d_h1
        dg3_ref[...] += jnp.sum(dn * hhat, axis=0, keepdims=True)
        mhat, rm = _rms_hat(m_ref[...])
        dm_ref[...] = _rms_bwd(mhat, rm, g2_ref[...], d_h1).astype(BF16)
        dg2_ref[...] += jnp.sum(d_h1 * mhat, axis=0, keepdims=True)

    vec = jax.ShapeDtypeStruct((GAIN_ROWS, D), F32)
    vspec = pl.BlockSpec((GAIN_ROWS, D), lambda i: (0, 0))
    return _rows_call(body, "post_mix_bwd", lq, (h1, d_hn2, g_pre, d_h2, mix, g_post),
                      [_rb(h1), _rb(d_hn2), _whole(g_pre), _rb(d_h2), _rb(mix), _whole(g_post)],
                      (jax.ShapeDtypeStruct((lq, D), F32), jax.ShapeDtypeStruct((lq, D), BF16), vec, vec),
                      (_rb(h1), _rb(h1), vspec, vspec))


def pre_mix_bwd(hp, d_hn1, g_pre, d_h1):
    lq = hp.shape[0]

    def body(h_ref, dn_ref, g_ref, dh1_ref, dhp_ref, dg_ref):
        i = pl.program_id(0)

        @pl.when(i == 0)
        def _():
            dg_ref[...] = jnp.zeros_like(dg_ref)

        hhat, r = _rms_hat(h_ref[...])
        dn = dn_ref[...]
        dhp_ref[...] = dh1_ref[...] + _rms_bwd(hhat, r, g_ref[...], dn)
        dg_ref[...] += jnp.sum(dn * hhat, axis=0, keepdims=True)

    return _rows_call(body, "pre_mix_bwd", lq, (hp, d_hn1, g_pre, d_h1),
                      [_rb(hp), _rb(d_hn1), _whole(g_pre), _rb(d_h1)],
                      (jax.ShapeDtypeStruct((lq, D), F32), jax.ShapeDtypeStruct((GAIN_ROWS, D), F32)),
                      (_rb(hp), pl.BlockSpec((GAIN_ROWS, D), lambda i: (0, 0))))


def _retention_tables():
    log_g = jnp.log1p(-(2.0 ** (-5.0 - jnp.arange(RH, dtype=F32))))
    idx = jnp.arange(CHUNK, dtype=F32)
    diff = idx[:, None] - idx[None, :]
    decay = jnp.where(diff >= 0, jnp.exp(log_g[:, None, None] * jnp.maximum(diff, 0.0)), 0.0)
    zeta = jnp.exp(log_g[:, None] * (CHUNK - 1.0 - idx))
    xi = jnp.exp(log_g[:, None] * (idx + 1.0))
    g_chunk = jnp.broadcast_to(jnp.exp(log_g * CHUNK)[:, None], (RH, CHUNK))
    coef = jnp.stack([xi, zeta, g_chunk] + [jnp.zeros_like(xi)] * 125, axis=-1)
    return decay, coef


def retention_fwd(qr, kr, proj, decay, coef):
    lq = qr.shape[0]
    n = lq // CHUNK

    def body(q_ref, k_ref, v_ref, dec_ref, cf_ref, o_ref, st_ref, state):
        @pl.when(pl.program_id(0) == 0)
        def _():
            state[...] = jnp.zeros_like(state)

        for h in range(RH):
            qk, vv = slice(h * RDK, (h + 1) * RDK), slice(h * RDV, (h + 1) * RDV)
            q, k, v = q_ref[:, qk], k_ref[:, qk], v_ref[:, vv]
            xi, zeta, gch = cf_ref[h, :, 0:1], cf_ref[h, :, 1:2], cf_ref[h, 0:1, 2:3]
            st = state[h]
            stb = st.astype(BF16)
            st_ref[h] = stb
            s = _dot_nt(q, k) * dec_ref[h]
            o_ref[:, vv] = _dot(s.astype(BF16), v) + _dot(q, stb) * xi
            vz = (v.astype(F32) * zeta).astype(BF16)
            state[h] = gch * st + _dot_tn(k, vz)

    return pl.pallas_call(
        body, name="retention_fwd", grid=(n,),
        in_specs=[pl.BlockSpec((CHUNK, RH * RDK), lambda c: (c, 0)), pl.BlockSpec((CHUNK, RH * RDK), lambda c: (c, 0)),
                  pl.BlockSpec((CHUNK, RH * RDV), lambda c: (c, C_RV // (RH * RDV))),
                  pl.BlockSpec((RH, CHUNK, CHUNK), lambda c: (0, 0, 0)),
                  pl.BlockSpec((RH, CHUNK, 128), lambda c: (0, 0, 0))],
        out_specs=(pl.BlockSpec((CHUNK, RH * RDV), lambda c: (c, 0)),
                   pl.BlockSpec((None, RH, RDK, RDV), lambda c: (c, 0, 0, 0))),
        out_shape=(jax.ShapeDtypeStruct((lq, RH * RDV), F32), jax.ShapeDtypeStruct((n, RH, RDK, RDV), BF16)),
        scratch_shapes=[pltpu.VMEM((RH, RDK, RDV), F32)],
        compiler_params=_params("arbitrary"),
    )(qr, kr, proj, decay, coef)


def retention_bwd(qr, kr, proj, d_o, states, decay, coef, d_proj):
    lq = qr.shape[0]
    n = lq // CHUNK

    def body(q_ref, k_ref, v_ref, do_ref, st_ref, dec_ref, cf_ref, _, dq_ref, dk_ref, dv_ref, dstate):
        @pl.when(pl.program_id(0) == 0)
        def _():
            dstate[...] = jnp.zeros_like(dstate)

        for h in range(RH):
            qk, vv = slice(h * RDK, (h + 1) * RDK), slice(h * RDV, (h + 1) * RDV)
            q, k, v, dob = q_ref[:, qk], k_ref[:, qk], v_ref[:, vv], do_ref[:, vv]
            xi, zeta, gch = cf_ref[h, :, 0:1], cf_ref[h, :, 1:2], cf_ref[h, 0:1, 2:3]
            dec = dec_ref[h]
            dsn = dstate[h]
            dsnb = dsn.astype(BF16)
            dox = (dob.astype(F32) * xi).astype(BF16)
            sb = (_dot_nt(q, k) * dec).astype(BF16)
            dsb = (_dot_nt(dob, v) * dec).astype(BF16)
            vz = (v.astype(F32) * zeta).astype(BF16)
            dq_ref[:, qk] = _dot(dsb, k) + _dot_nt(dox, st_ref[h])
            dk_ref[:, qk] = _dot_tn(dsb, q) + _dot_nt(vz, dsnb)
            dv_ref[:, vv] = (_dot_tn(sb, dob) + _dot(k, dsnb) * zeta).astype(BF16)
            dstate[h] = gch * dsn + _dot_tn(q, dox)

    rev = lambda c: n - 1 - c
    qk_spec = pl.BlockSpec((CHUNK, RH * RDK), lambda c: (rev(c), 0))
    v_spec = pl.BlockSpec((CHUNK, RH * RDV), lambda c: (rev(c), 0))
    v_cols = pl.BlockSpec((CHUNK, RH * RDV), lambda c: (rev(c), C_RV // (RH * RDV)))
    return pl.pallas_call(
        body, name="retention_bwd", grid=(n,),
        in_specs=[qk_spec, qk_spec, v_cols, v_spec,
                  pl.BlockSpec((None, RH, RDK, RDV), lambda c: (rev(c), 0, 0, 0)),
                  pl.BlockSpec((RH, CHUNK, CHUNK), lambda c: (0, 0, 0)),
                  pl.BlockSpec((RH, CHUNK, 128), lambda c: (0, 0, 0)), ANY_SPEC],
        out_specs=(qk_spec, qk_spec, v_cols),
        out_shape=(jax.ShapeDtypeStruct((lq, RH * RDK), F32), jax.ShapeDtypeStruct((lq, RH * RDK), F32),
                   _d_proj_shape(lq)),
        scratch_shapes=[pltpu.VMEM((RH, RDK, RDV), F32)],
        input_output_aliases={7: 2}, compiler_params=_params("arbitrary"),
    )(qr, kr, proj, d_o, states, decay, coef, d_proj)


def _sb_masks_and_u():
    lane = lax.broadcasted_iota(jnp.int32, (1, 2 * SB_DH), 1)
    lo = lane < SB_DH
    row = lax.broadcasted_iota(jnp.int32, (SB_T, SB_T), 0)
    col = lax.broadcasted_iota(jnp.int32, (SB_T, SB_T), 1)
    u = (row > col).astype(BF16)
    return lo, u, row, col


def _sb_rows(j):
    start = j * SB_T
    return pl.ds(start if isinstance(j, int) else pl.multiple_of(start, SB_T), SB_T)


SB_DEAD = -104.0
SB_ROWS = SB_T
SB_CHAINS = tuple((hh, slice(r, r + SB_ROWS)) for hh in range(2) for r in range(0, SB_T, SB_ROWS))


def _sb_alive(logs):
    m = logs[0]
    for l in logs[1:]:
        m = jnp.maximum(m, l)
    return (jnp.max(m) > SB_DEAD).astype(jnp.int32)


def _sb_walk(i, run, logs_of, step):
    def cond(c):
        return (c[0] >= 1) & (c[1] > 0)

    def body(c):
        r = step(c[0], c[2])
        return c[0] - 1, _sb_alive(logs_of(r)), r

    return lax.while_loop(cond, body, (i - 1, _sb_alive(logs_of(run)), run))


def _sb_logs(z):
    ls = jnp.minimum(z, 0.0) - jnp.log(1.0 + jnp.exp(-jnp.abs(z)))
    return ls, ls - z


def sb_fwd(proj, exch):
    lq = proj.shape[0]
    t = SB_T

    def body(q_ref, k_ref, v_ref, o_ref, of_ref, acc_ref):
        i = pl.program_id(1)
        lo, u, row, col = _sb_masks_and_u()
        qs = (q_ref[...].astype(F32) * SB_DH ** -0.5).astype(BF16)
        zero = jnp.zeros_like(qs)
        qh = (jnp.where(lo, qs, zero), jnp.where(lo, zero, qs))
        acc_ref[...] = jnp.zeros_like(acc_ref)

        def block(j, run, masked):
            rows = _sb_rows(j)
            ks, vs = k_ref[rows, :], v_ref[rows, :]
            if masked:
                valid = (col + j * t < row + i * t) & (col + j * t >= META0)
            out = []
            for hh, rs in SB_CHAINS:
                ls, ln = _sb_logs(_dot_nt(qh[hh][rs], ks))
                if masked:
                    ln = jnp.where(valid[rs], ln, 0.0)
                a = jnp.exp(ls + _dot(ln.astype(BF16), u) + run[len(out)])
                if masked:
                    a = jnp.where(valid[rs], a, 0.0)
                acc_ref[hh, rs, :] += _dot(a.astype(BF16), vs)
                out.append(run[len(out)] + jnp.sum(ln, axis=1, keepdims=True))
            return tuple(out)

        zeros = jnp.zeros((SB_ROWS, 1), F32)
        run = block(i, (zeros,) * len(SB_CHAINS), True)
        _, go, run = _sb_walk(i, run, lambda r: r, lambda j, r: block(j, r, False))

        @pl.when((i > 0) & (go > 0))
        def _():
            block(0, run, True)

        o = jnp.where(lo, acc_ref[0], acc_ref[1])
        o_ref[...] = o.astype(BF16)
        of_ref[...] = o

    blk = pl.BlockSpec((t, 128), lambda p, i: (i, p))
    return _grid_call(
        body, "sb_fwd", (D // 128, lq // t), (proj, proj, proj),
        [pl.BlockSpec((t, 128), lambda p, i: (i, C_SQ // 128 + p)),
         pl.BlockSpec((lq, 128), lambda p, i: (0, C_SK // 128 + p)),
         pl.BlockSpec((lq, 128), lambda p, i: (0, C_SV // 128 + p))],
        (jax.ShapeDtypeStruct((lq, D), BF16), jax.ShapeDtypeStruct((lq, D), F32)), (blk, blk),
        [pltpu.VMEM((2, t, 128), F32)], exch)


def sb_bwd(proj, o, d_o, d_proj, exch):
    lq = proj.shape[0]
    t = SB_T
    last = lq // t - 1

    def body(q_ref, k_ref, v_ref, o_ref, do_ref, _, dq_ref, dk_out, dv_out, acc_ref, dk_ref, dv_ref):
        i = pl.program_id(1)

        @pl.when(i == 0)
        def _():
            dk_ref[...] = jnp.zeros_like(dk_ref)
            dv_ref[...] = jnp.zeros_like(dv_ref)

        lo, u, row, col = _sb_masks_and_u()
        incl = (row <= col).astype(BF16)
        qs = (q_ref[...].astype(F32) * SB_DH ** -0.5).astype(BF16)
        do = do_ref[...]
        zero = jnp.zeros_like(qs)
        qh = (jnp.where(lo, qs, zero), jnp.where(lo, zero, qs))
        doh = (jnp.where(lo, do, zero), jnp.where(lo, zero, do))
        prod = o_ref[...] * do.astype(F32)
        dsum = (jnp.sum(jnp.where(lo, prod, 0.0), axis=1, keepdims=True),
                jnp.sum(jnp.where(lo, 0.0, prod), axis=1, keepdims=True))
        acc_ref[...] = jnp.zeros_like(acc_ref)

        def block(j, run, masked):
            rows = _sb_rows(j)
            ks, vs = k_ref[rows, :], v_ref[rows, :]
            if masked:
                valid = (col + j * t < row + i * t) & (col + j * t >= META0)
            out = []
            for hh in range(2):
                run_ln, run_e = run[2 * hh], run[2 * hh + 1]
                ls, ln = _sb_logs(_dot_nt(qh[hh], ks))
                if masked:
                    ln = jnp.where(valid, ln, 0.0)
                a = jnp.exp(ls + _dot(ln.astype(BF16), u) + run_ln)
                if masked:
                    a = jnp.where(valid, a, 0.0)
                ab = a.astype(BF16)
                e = ab.astype(F32) * _dot_nt(doh[hh], vs)
                e_sum = jnp.sum(e, axis=1, keepdims=True)
                upto = (dsum[hh] - run_e - e_sum) + _dot(e.astype(BF16), incl)
                dz = e - jnp.exp(ls) * upto
                if masked:
                    dz = jnp.where(valid, dz, 0.0)
                dzb = dz.astype(BF16)
                acc_ref[hh] += _dot(dzb, ks)
                dk_ref[rows, :] += _dot_tn(dzb, qh[hh])
                dv_ref[rows, :] += _dot_tn(ab, doh[hh])
                out += [run_ln + jnp.sum(ln, axis=1, keepdims=True), run_e + e_sum]
            return tuple(out)

        zeros = jnp.zeros((t, 1), F32)
        run = block(i, (zeros,) * 4, True)
        _, go, run = _sb_walk(i, run, lambda r: (r[0], r[2]), lambda j, r: block(j, r, False))

        @pl.when((i > 0) & (go > 0))
        def _():
            block(0, run, True)

        dq_ref[...] = (jnp.where(lo, acc_ref[0], acc_ref[1]) * SB_DH ** -0.5).astype(BF16)

        @pl.when(i == last)
        def _():
            dk_out[...] = dk_ref[...].astype(BF16)
            dv_out[...] = dv_ref[...].astype(BF16)

    blk = pl.BlockSpec((t, 128), lambda p, i: (i, p))
    q_cols = pl.BlockSpec((t, 128), lambda p, i: (i, C_SQ // 128 + p))
    col_blk = pl.BlockSpec((lq, 128), lambda p, i: (0, p))
    return _grid_call(
        body, "sb_bwd", (D // 128, lq // t), (proj, proj, proj, o, d_o, d_proj),
        [q_cols, pl.BlockSpec((lq, 128), lambda p, i: (0, C_SK // 128 + p)),
         pl.BlockSpec((lq, 128), lambda p, i: (0, C_SV // 128 + p)), blk, blk, ANY_SPEC],
        (_d_proj_shape(lq), jax.ShapeDtypeStruct((lq, D), BF16), jax.ShapeDtypeStruct((lq, D), BF16)),
        (q_cols, col_blk, col_blk),
        [pltpu.VMEM((2, t, 128), F32), pltpu.VMEM((lq, 128), F32), pltpu.VMEM((lq, 128), F32)], exch, {5: 0})


def all_gather(blocks):
    nb = len(blocks)

    def body(*refs):
        x_refs, out_refs = refs[:nb], refs[nb:2 * nb]
        send_sems, recv_sems, local_sems = refs[2 * nb:]
        x, y, c = lax.axis_index("x"), lax.axis_index("y"), lax.axis_index("c")
        me, sibling = (x, y, c), (x, y, 1 - c)
        chips = [(1 - x, y), (x, 1 - y), (1 - x, 1 - y)]

        def copy(b, k, block, to, src=None):
            slot = out_refs[b].at[_device_index(*block)]
            return pltpu.make_async_remote_copy(
                src_ref=slot if src is None else src, dst_ref=slot,
                send_sem=send_sems.at[b, k], recv_sem=recv_sems.at[b, k],
                device_id=to, device_id_type=pl.DeviceIdType.MESH)

        mine = [pltpu.make_async_copy(x_refs[b], out_refs[b].at[_device_index(*me)], local_sems.at[b])
                for b in range(nb)]
        for cp in mine:
            cp.start()
        first = []
        for b in range(nb):
            first.append(copy(b, 0, me, sibling, src=x_refs[b]))
            first += [copy(b, 1 + j, me, (*chip, c), src=x_refs[b]) for j, chip in enumerate(chips)]
        for cp in first:
            cp.start()
        passed = []
        for j, chip in enumerate(chips):
            for b in range(nb):
                copy(b, 1 + j, (*chip, c), me).wait_recv()
                cp = copy(b, 4 + j, (*chip, c), sibling)
                cp.start()
                passed.append(cp)
        for b in range(nb):
            copy(b, 0, sibling, me).wait_recv()
            for j, chip in enumerate(chips):
                copy(b, 4 + j, (*chip, 1 - c), me).wait_recv()
        for cp in first + passed:
            cp.wait_send()
        for cp in mine:
            cp.wait()

    any_spec = pl.BlockSpec(memory_space=pl.ANY)
    return pl.pallas_call(
        body, name="all_gather",
        in_specs=[any_spec] * nb, out_specs=[any_spec] * nb,
        out_shape=[jax.ShapeDtypeStruct((NDEV,) + b.shape, b.dtype) for b in blocks],
        scratch_shapes=[pltpu.SemaphoreType.DMA((nb, 7)), pltpu.SemaphoreType.DMA((nb, 7)),
                        pltpu.SemaphoreType.DMA((nb,))],
    )(*blocks)


def gather_small(small):
    def body(s_ref, land_ref, send_sems, recv_sems, local_sem):
        start, wait = _direct_exchange(lambda d: s_ref, land_ref, send_sems, recv_sems, local_sem)
        start()
        wait()

    any_spec = pl.BlockSpec(memory_space=pl.ANY)
    return pl.pallas_call(
        body, name="gather_small", in_specs=[any_spec], out_specs=any_spec,
        out_shape=jax.ShapeDtypeStruct((NDEV,) + small.shape, small.dtype),
        scratch_shapes=[pltpu.SemaphoreType.DMA((NDEV - 1,)), pltpu.SemaphoreType.DMA((NDEV - 1,)),
                        pltpu.SemaphoreType.DMA(())],
    )(small)


def sum_slots(landed, name):
    _, r, c = landed.shape
    tr = _tile(r, (352, 224, 8))

    def body(l_ref, o_ref):
        acc = l_ref[0].astype(F32)
        for p in range(1, NDEV):
            acc = acc + l_ref[p].astype(F32)
        o_ref[...] = acc

    return pl.pallas_call(
        body, name=name, grid=(r // tr,),
        in_specs=[pl.BlockSpec((NDEV, tr, c), lambda i: (0, i, 0))],
        out_specs=pl.BlockSpec((tr, c), lambda i: (i, 0)),
        out_shape=jax.ShapeDtypeStruct((r, c), F32),
        compiler_params=_params("parallel"),
    )(landed)


def adamw(w, g, m, v, name):
    r, c = w.shape
    tr = _tile(r, (256, 128))

    def body(w_ref, g_ref, m_ref, v_ref, d_ref, nm_ref, nv_ref):
        g_ = g_ref[...]
        m_ = ADAM_B1 * m_ref[...] + (1.0 - ADAM_B1) * g_
        v_ = ADAM_B2 * v_ref[...] + (1.0 - ADAM_B2) * jnp.square(g_)
        m_hat = m_ / (1.0 - ADAM_B1 ** ADAM_STEP)
        v_hat = v_ / (1.0 - ADAM_B2 ** ADAM_STEP)
        d_ref[...] = -ADAM_LR * (m_hat / (jnp.sqrt(v_hat) + ADAM_EPS) + ADAM_WD * w_ref[...])
        nm_ref[...] = m_
        nv_ref[...] = v_

    spec = pl.BlockSpec((tr, c), lambda i: (i, 0))
    out = jax.ShapeDtypeStruct((r, c), F32)
    return pl.pallas_call(
        body, name=name, grid=(r // tr,), in_specs=[spec] * 4, out_specs=(spec,) * 3, out_shape=(out,) * 3,
        compiler_params=_params("parallel"),
    )(w, g, m, v)


def kernel(x, meta_tokens, w_in, w_ret_out, w_sb_out, w_out, w_ffn_in, w_ffn_out, norm_mix_pre, norm_mix_post, norm_ffn_pre, norm_ffn_post, loss_target, m_meta_tokens, m_w_in, m_w_ret_out, m_w_sb_out, m_w_out, m_w_ffn_in, m_w_ffn_out, m_norm_mix_pre, m_norm_mix_post, m_norm_ffn_pre, m_norm_ffn_post, v_meta_tokens, v_w_in, v_w_ret_out, v_w_sb_out, v_w_out, v_w_ffn_in, v_w_ffn_out, v_norm_mix_pre, v_norm_mix_post, v_norm_ffn_pre, v_norm_ffn_post):
    seq = x.shape[1]
    lq = seq + FRONT
    me = _device_index(lax.axis_index("x"), lax.axis_index("y"), lax.axis_index("c"))

    shards = {"w_in_t": w_in[0].T, "w_ffn_in_t": w_ffn_in[0].T, "w_ret_out": w_ret_out[0],
              "w_sb_out": w_sb_out[0], "w_out": w_out[0], "w_ffn_out": w_ffn_out[0]}
    rest_rows = PACK_ROWS[1:]
    pack_rest = jnp.concatenate([shards[n].astype(BF16) for n, _ in rest_rows], axis=0)
    gathered_in, meta_all = all_gather([shards["w_in_t"].astype(BF16), meta_tokens])
    full = {"w_in_t": gathered_in.reshape(PROJ, D)}
    meta_full = meta_all.transpose(1, 0, 2).reshape(N_META, D)

    pos = jnp.arange(lq, dtype=F32) - META0
    half = RDK // 2
    ang = pos[:, None] * (ROPE_BASE ** (-jnp.arange(half, dtype=F32) / half))[None, :]
    cos, sin = jnp.cos(ang), jnp.sin(ang)
    decay, coef = _retention_tables()

    hp = jnp.concatenate([jnp.zeros((META0, D), F32), meta_full, x[0]], axis=0)
    hn1 = rms_fwd(hp, norm_mix_pre, "rms_mix_pre")
    proj = mm_nt(hn1, full["w_in_t"], BF16, "proj")
    qr, kr = rotary_fwd(proj, cos, sin)
    o_ret, states = retention_fwd(qr, kr, proj, decay, coef)
    gr = ret_gate_fwd(proj, o_ret)
    o_sb, o_sb_f32, gathered_rest = sb_fwd(proj, (pack_rest, "gather"))
    off = 0
    for n, r in rest_rows:
        full[n] = gathered_rest[:, off:off + r, :].reshape(NDEV * r, D)
        off += r
    y_ret = mm_nn(gr, full["w_ret_out"], F32, "y_ret")
    y_sb = mm_nn(o_sb, full["w_sb_out"], F32, "y_sb")
    merged = merge_fwd(proj, y_ret, y_sb)
    mix = mm_nn(merged, full["w_out"], F32, "mix")
    h1, hn2 = post_mix_fwd(hp, mix, norm_mix_post, norm_ffn_pre)
    ab = mm_nt(hn2, full["w_ffn_in_t"], BF16, "ffn_in")
    act = swiglu_fwd(ab)
    ff = mm_nn(act, full["w_ffn_out"], F32, "ffn_out")
    loss_blk, d_h2, d_ff, dg_ffn_post = loss_head(h1, ff, norm_ffn_post, loss_target[0])
    loss = lax.psum(loss_blk[0, 0], ("x", "y", "c"))

    grads = {}
    d_act = mm_nt(d_ff, full["w_ffn_out"], BF16, "d_act")
    grads["w_ffn_out"] = mm_tn(act, d_ff, BF16, "dw_ffn_out")
    d_ab = swiglu_bwd(ab, d_act)
    d_hn2 = mm_nn(d_ab, full["w_ffn_in_t"], F32, "d_hn2")
    grads["w_ffn_in_t"] = mm_tn(d_ab, hn2, BF16, "dw_ffn_in")
    d_h1, d_mix, dg_ffn_pre, dg_mix_post = post_mix_bwd(h1, d_hn2, norm_ffn_pre, d_h2, mix, norm_mix_post)
    d_merged = mm_nt(d_mix, full["w_out"], BF16, "d_merged")
    grads["w_out"] = mm_tn(merged, d_mix, BF16, "dw_out")
    d_y_ret, d_y_sb, d_proj = merge_bwd(proj, y_ret, y_sb, d_merged)
    d_gr = mm_nt(d_y_ret, full["w_ret_out"], BF16, "d_gr")
    grads["w_ret_out"] = mm_tn(gr, d_y_ret, BF16, "dw_ret_out")
    d_o_sb = mm_nt(d_y_sb, full["w_sb_out"], BF16, "d_o_sb")
    grads["w_sb_out"] = mm_tn(o_sb, d_y_sb, BF16, "dw_sb_out")
    d_proj, d_o_ret = ret_gate_bwd(proj, o_ret, d_gr, d_proj)
    d_qr, d_kr, d_proj = retention_bwd(qr, kr, proj, d_o_ret, states, decay, coef, d_proj)
    d_proj = rotary_bwd(d_qr, d_kr, cos, sin, d_proj)
    parts_rest = jnp.concatenate([grads[n].reshape(NDEV, r, D) for n, r in rest_rows], axis=1)
    d_proj, d_sk, d_sv, landed_rest = sb_bwd(proj, o_sb_f32, d_o_sb, d_proj, (parts_rest, "scatter"))
    d_proj = lax.dynamic_update_slice(d_proj, d_sk, (0, C_SK))
    d_proj = lax.dynamic_update_slice(d_proj, d_sv, (0, C_SV))
    dw_in_t = mm_tn(d_proj, hn1, BF16, "dw_in")
    d_hn1, landed_in = mm_nn(d_proj, full["w_in_t"], F32, "d_hn1",
                             exch=(dw_in_t.reshape(NDEV, PROJ // NDEV, D), "scatter"))
    d_hp, dg_mix_pre = pre_mix_bwd(hp, d_hn1, norm_mix_pre, d_h1)
    grad_x = d_hp[FRONT:][None]
    small = jnp.concatenate([dg_mix_pre, dg_mix_post, dg_ffn_pre, dg_ffn_post, d_hp[META0:FRONT]], axis=0)
    ssum = sum_slots(gather_small(small), "sum_small")
    g = {"w_in_t": sum_slots(landed_in, "sum_grads_in")}
    gsum = sum_slots(landed_rest, "sum_grads_rest")
    off = 0
    for n, r in rest_rows:
        g[n] = gsum[off:off + r]
        off += r
    gain = lambda k: ssum[k * GAIN_ROWS:k * GAIN_ROWS + 1]
    g_w = {"meta_tokens": lax.dynamic_slice(ssum[4 * GAIN_ROWS:], (0, me * (D // NDEV)), (N_META, D // NDEV)),
           "w_in": g["w_in_t"].T[None], "w_ret_out": g["w_ret_out"][None], "w_sb_out": g["w_sb_out"][None],
           "w_out": g["w_out"][None], "w_ffn_in": g["w_ffn_in_t"].T[None], "w_ffn_out": g["w_ffn_out"][None],
           "norm_mix_pre": gain(0), "norm_mix_post": gain(1), "norm_ffn_pre": gain(2), "norm_ffn_post": gain(3)}

    names = ["meta_tokens", "w_in", "w_ret_out", "w_sb_out", "w_out", "w_ffn_in", "w_ffn_out",
             "norm_mix_pre", "norm_mix_post", "norm_ffn_pre", "norm_ffn_post"]
    w_of = dict(zip(names, (meta_tokens, w_in, w_ret_out, w_sb_out, w_out, w_ffn_in, w_ffn_out,
                            norm_mix_pre, norm_mix_post, norm_ffn_pre, norm_ffn_post)))
    m_of = dict(zip(names, (m_meta_tokens, m_w_in, m_w_ret_out, m_w_sb_out, m_w_out, m_w_ffn_in, m_w_ffn_out,
                            m_norm_mix_pre, m_norm_mix_post, m_norm_ffn_pre, m_norm_ffn_post)))
    v_of = dict(zip(names, (v_meta_tokens, v_w_in, v_w_ret_out, v_w_sb_out, v_w_out, v_w_ffn_in, v_w_ffn_out,
                            v_norm_mix_pre, v_norm_mix_post, v_norm_ffn_pre, v_norm_ffn_post)))
    delta, new_m, new_v = {}, {}, {}
    for n in names:
        shape = w_of[n].shape
        two_d = (shape[-2], shape[-1])
        d_, m_, v_ = adamw(w_of[n].reshape(two_d), g_w[n].reshape(two_d), m_of[n].reshape(two_d),
                           v_of[n].reshape(two_d), "adamw_" + n)
        delta[n], new_m[n], new_v[n] = d_.reshape(shape), m_.reshape(shape), v_.reshape(shape)

    return (loss, grad_x, *[g_w[n] for n in names], *[delta[n] for n in names],
            *[new_m[n] for n in names], *[new_v[n] for n in names])
```

```python
import jax
import jax.numpy as jnp
from jax import lax
from jax.experimental import pallas as pl
from jax.experimental.pallas import tpu as pltpu

F32 = jnp.float32
BF16 = jnp.bfloat16

D = 1024
N_META = 16
CHUNK = 128
FRONT = 256
META0 = FRONT - N_META
RH, RDK, RDV = 4, 256, 512
SB_DH = 64
DFF = 2816
NDEV = 8
ROPE_BASE = 10000.0
NORM_EPS = 1e-6
GN_EPS = 1e-5
C_RQ, C_RK, C_RV, C_RG, C_SQ, C_SK, C_SV, C_GA, C_GB = 0, 1024, 2048, 4096, 6144, 7168, 8192, 9216, 10240
PROJ = 11264
PACK_ROWS = (("w_in_t", PROJ // NDEV), ("w_ffn_in_t", 2 * DFF // NDEV), ("w_ret_out", RH * RDV // NDEV),
             ("w_sb_out", D // NDEV), ("w_out", D // NDEV), ("w_ffn_out", DFF // NDEV))
PACK = sum(r for _, r in PACK_ROWS)
GAIN_ROWS = 8

ADAM_LR = 0.001
ADAM_B1 = 0.9
ADAM_B2 = 0.999
ADAM_EPS = 1e-08
ADAM_WD = 0.01
ADAM_STEP = 10

VMEM_LIMIT = 56 * 1024 * 1024
SB_T = 256

NT = (((1,), (1,)), ((), ()))
TN = (((0,), (0,)), ((), ()))


def _dot(a, b):
    return jnp.dot(a, b, preferred_element_type=F32)


def _dot_nt(a, b):
    return lax.dot_general(a, b, NT, preferred_element_type=F32)


def _dot_tn(a, b):
    return lax.dot_general(a, b, TN, preferred_element_type=F32)


WIDE_TILES = (1024, 1408, 512, 256)


def _tile(n, candidates):
    for t in candidates:
        if n % t == 0:
            return t
    return n


def _params(*sem):
    return pltpu.CompilerParams(dimension_semantics=sem, vmem_limit_bytes=VMEM_LIMIT)


def _rms_hat(x):
    r = lax.rsqrt(jnp.mean(x * x, axis=-1, keepdims=True) + NORM_EPS)
    return x * r, r


def _rms_bwd(xhat, r, g, dy):
    u = dy * g
    return r * (u - xhat * jnp.mean(u * xhat, axis=-1, keepdims=True))


def _gn(y):
    mu = jnp.mean(y, axis=-1, keepdims=True)
    yc = y - mu
    rs = lax.rsqrt(jnp.mean(yc * yc, axis=-1, keepdims=True) + GN_EPS)
    return yc * rs, rs


def _gn_bwd(yh, rs, d):
    return rs * (d - jnp.mean(d, axis=-1, keepdims=True) - yh * jnp.mean(d * yh, axis=-1, keepdims=True))


def _sigmoid(x):
    return 1.0 / (1.0 + jnp.exp(-x))


def _device_index(px, py, pc):
    return 4 * px + 2 * py + pc


def _direct_exchange(src_for, land_ref, send_sems, recv_sems, local_sem):
    x, y, c = lax.axis_index("x"), lax.axis_index("y"), lax.axis_index("c")
    me = _device_index(x, y, c)
    peers = []
    for k in range(1, NDEV):
        pos = (1 - x if k & 4 else x, 1 - y if k & 2 else y, 1 - c if k & 1 else c)
        peers.append((k - 1, pos, _device_index(*pos)))

    def local():
        return pltpu.make_async_copy(src_for(me), land_ref.at[me], local_sem)

    def remote(k, pos, idx):
        return pltpu.make_async_remote_copy(
            src_ref=src_for(idx), dst_ref=land_ref.at[me], send_sem=send_sems.at[k], recv_sem=recv_sems.at[k],
            device_id=pos, device_id_type=pl.DeviceIdType.MESH)

    def arrival(k, idx):
        return pltpu.make_async_remote_copy(
            src_ref=src_for(idx), dst_ref=land_ref.at[idx], send_sem=send_sems.at[k], recv_sem=recv_sems.at[k],
            device_id=(x, y, c), device_id_type=pl.DeviceIdType.MESH)

    def start():
        local().start()
        for p in peers:
            remote(*p).start()

    def wait():
        for k, _, idx in peers:
            arrival(k, idx).wait_recv()
        for p in peers:
            remote(*p).wait_send()
        local().wait()

    return start, wait


def _grid_call(compute, name, grid, arrays, in_specs, out_shapes, out_specs, scratch, exch, aliases=None):
    arrays, in_specs = list(arrays), list(in_specs)
    out_shapes, out_specs, scratch = list(out_shapes), list(out_specs), list(scratch)
    n_in, n_out, steps = len(arrays), len(out_shapes), grid[0] * grid[1]
    aliases = aliases or {}
    if exch is None:
        return pl.pallas_call(
            compute, name=name, grid=grid, in_specs=in_specs, out_specs=out_specs, out_shape=out_shapes,
            scratch_shapes=scratch, input_output_aliases=aliases,
            compiler_params=_params("parallel", "arbitrary"))(*arrays)
    src, mode = exch

    def body(*refs):
        refs = list(refs)
        src_ref = refs.pop(n_in)
        land_ref = refs.pop(n_in + n_out)
        send_sems, recv_sems, local_sem = refs[-3:]
        src_for = (lambda d: src_ref) if mode == "gather" else (lambda d: src_ref.at[d])
        start, wait = _direct_exchange(src_for, land_ref, send_sems, recv_sems, local_sem)
        step = pl.program_id(0) * grid[1] + pl.program_id(1)
        pl.when(step == 0)(start)
        compute(*refs[:-3])
        pl.when(step == steps - 1)(wait)

    any_spec = pl.BlockSpec(memory_space=pl.ANY)
    land = src.shape if mode == "scatter" else (NDEV,) + src.shape
    return pl.pallas_call(
        body, name=name, grid=grid, in_specs=in_specs + [any_spec], out_specs=out_specs + [any_spec],
        out_shape=out_shapes + [jax.ShapeDtypeStruct(land, src.dtype)],
        scratch_shapes=scratch + [pltpu.SemaphoreType.DMA((NDEV - 1,)), pltpu.SemaphoreType.DMA((NDEV - 1,)),
                                  pltpu.SemaphoreType.DMA(())],
        input_output_aliases=aliases, compiler_params=_params("arbitrary", "arbitrary"))(*(arrays + [src]))


def _mm_call(compute, name, grid, arrays, in_specs, out_shape, out_spec, acc_shape, exch):
    if acc_shape:
        body, scratch = compute, [pltpu.VMEM(acc_shape, F32)]
    else:
        body, scratch = (lambda a_ref, b_ref, o_ref: compute(a_ref, b_ref, o_ref, None)), []
    out = _grid_call(body, name, grid, arrays, in_specs, [out_shape], [out_spec], scratch, exch)
    return tuple(out) if exch else out[0]


def mm_nt(a, b, out_dtype, name, exch=None):
    m, k = a.shape
    n = b.shape[0]
    tm = _tile(m, (768, 512, 256))
    tn = _tile(n, WIDE_TILES)

    def compute(a_ref, b_ref, o_ref, acc_ref):
        o_ref[...] = _dot_nt(a_ref[...], b_ref[...]).astype(o_ref.dtype)

    return _mm_call(
        compute, name, (m // tm, n // tn), (a, b),
        [pl.BlockSpec((tm, k), lambda i, j: (i, 0)), pl.BlockSpec((tn, k), lambda i, j: (j, 0))],
        jax.ShapeDtypeStruct((m, n), out_dtype), pl.BlockSpec((tm, tn), lambda i, j: (i, j)), None, exch)


def _accumulate(dot, steps):
    def compute(a_ref, b_ref, o_ref, acc_ref):
        kk = pl.program_id(1)

        @pl.when(kk == 0)
        def _():
            acc_ref[...] = jnp.zeros_like(acc_ref)

        acc_ref[...] += dot(a_ref[...], b_ref[...])

        @pl.when(kk == steps - 1)
        def _():
            o_ref[...] = acc_ref[...].astype(o_ref.dtype)

    return compute


def mm_nn(a, b, out_dtype, name, exch=None):
    m, k = a.shape
    n = b.shape[1]
    tm = _tile(m, (768, 512, 256))
    tk = _tile(k, (2816,) + WIDE_TILES)
    return _mm_call(
        _accumulate(_dot, k // tk), name, (m // tm, k // tk), (a, b),
        [pl.BlockSpec((tm, tk), lambda i, kk: (i, kk)), pl.BlockSpec((tk, n), lambda i, kk: (kk, 0))],
        jax.ShapeDtypeStruct((m, n), out_dtype), pl.BlockSpec((tm, n), lambda i, kk: (i, 0)), (tm, n), exch)


def mm_tn(a, b, out_dtype, name, exch=None):
    m, ka = a.shape
    n = b.shape[1]
    ta = _tile(ka, WIDE_TILES)
    tl = _tile(m, (1408, 768, 512, 256))
    return _mm_call(
        _accumulate(_dot_tn, m // tl), name, (ka // ta, m // tl), (a, b),
        [pl.BlockSpec((tl, ta), lambda i, ll: (ll, i)), pl.BlockSpec((tl, n), lambda i, ll: (ll, 0))],
        jax.ShapeDtypeStruct((ka, n), out_dtype), pl.BlockSpec((ta, n), lambda i, ll: (i, 0)), (ta, n), exch)


TM = 384


def _rb(arr, width=None, col_block=0, tm=TM):
    w = arr.shape[1] if width is None else width
    return pl.BlockSpec((tm, w), lambda i: (i, col_block))


def _whole(arr):
    return pl.BlockSpec(arr.shape, lambda i: (0,) * arr.ndim)


def _rows_call(body, name, lq, ins, in_specs, out_shapes, out_specs, aliases=None, tm=TM):
    return pl.pallas_call(
        body, name=name, grid=(lq // tm,), in_specs=in_specs, out_specs=out_specs, out_shape=out_shapes,
        input_output_aliases=aliases or {}, compiler_params=_params("arbitrary"),
    )(*ins)


ANY_SPEC = pl.BlockSpec(memory_space=pl.ANY)


def _d_proj_shape(lq):
    return jax.ShapeDtypeStruct((lq, PROJ), BF16)


def rms_fwd(h, g, name):
    lq = h.shape[0]

    def body(h_ref, g_ref, o_ref):
        xhat, _ = _rms_hat(h_ref[...])
        o_ref[...] = (xhat * g_ref[...]).astype(BF16)

    return _rows_call(body, name, lq, (h, g), [_rb(h), _whole(g)],
                      jax.ShapeDtypeStruct((lq, D), BF16), _rb(h))


def merge_fwd(proj, y_ret, y_sb):
    lq = proj.shape[0]

    def body(ga_ref, gb_ref, yr_ref, ys_ref, o_ref):
        o_ref[...] = (_sigmoid(ga_ref[...].astype(F32)) * yr_ref[...]
                      + _sigmoid(gb_ref[...].astype(F32)) * ys_ref[...]).astype(BF16)

    return _rows_call(body, "merge_fwd", lq, (proj, proj, y_ret, y_sb),
                      [_rb(proj, D, C_GA // D), _rb(proj, D, C_GB // D), _rb(y_ret), _rb(y_sb)],
                      jax.ShapeDtypeStruct((lq, D), BF16), _rb(y_ret))


def merge_bwd(proj, y_ret, y_sb, d_merged):
    lq = proj.shape[0]

    def body(gate_ref, yr_ref, ys_ref, d_ref, dyr_ref, dys_ref, dg_ref):
        d = d_ref[...].astype(F32)
        sg = _sigmoid(gate_ref[...].astype(F32))
        for g, y_ref, dy_ref in ((0, yr_ref, dyr_ref), (1, ys_ref, dys_ref)):
            @pl.when(pl.program_id(1) == g)
            def _():
                dy_ref[...] = (d * sg).astype(BF16)
                dg_ref[...] = (d * y_ref[...] * sg * (1.0 - sg)).astype(BF16)

    tm = 2 * TM
    row = pl.BlockSpec((tm, D), lambda i, g: (i, 0))
    o1 = jax.ShapeDtypeStruct((lq, D), BF16)
    return pl.pallas_call(
        body, name="merge_bwd", grid=(lq // tm, 2),
        in_specs=[pl.BlockSpec((tm, D), lambda i, g: (i, C_GA // D + g)), row, row, row],
        out_specs=(row, row, pl.BlockSpec((tm, D), lambda i, g: (i, C_GA // D + g))),
        out_shape=(o1, o1, _d_proj_shape(lq)),
        compiler_params=_params("arbitrary", "arbitrary"),
    )(proj, y_ret, y_sb, d_merged)


def post_mix_fwd(hp, mix, g_post, g_pre):
    lq = hp.shape[0]

    def body(h_ref, m_ref, g2_ref, g3_ref, h1_ref, hn_ref):
        mhat, _ = _rms_hat(m_ref[...])
        h1 = h_ref[...] + mhat * g2_ref[...]
        h1_ref[...] = h1
        hhat, _ = _rms_hat(h1)
        hn_ref[...] = (hhat * g3_ref[...]).astype(BF16)

    return _rows_call(body, "post_mix_fwd", lq, (hp, mix, g_post, g_pre),
                      [_rb(hp), _rb(mix), _whole(g_post), _whole(g_pre)],
                      (jax.ShapeDtypeStruct((lq, D), F32), jax.ShapeDtypeStruct((lq, D), BF16)),
                      (_rb(hp), _rb(hp)))


def swiglu_fwd(ab):
    lq = ab.shape[0]

    def body(a_ref, b_ref, o_ref):
        a = a_ref[...].astype(F32)
        o_ref[...] = (a * _sigmoid(a) * b_ref[...].astype(F32)).astype(BF16)

    return _rows_call(body, "swiglu_fwd", lq, (ab, ab), [_rb(ab, DFF, 0), _rb(ab, DFF, 1)],
                      jax.ShapeDtypeStruct((lq, DFF), BF16), pl.BlockSpec((TM, DFF), lambda i: (i, 0)))


def swiglu_bwd(ab, d_act):
    lq = ab.shape[0]

    def body(a_ref, b_ref, d_ref, o_ref):
        a = a_ref[...].astype(F32)
        b = b_ref[...].astype(F32)
        d = d_ref[...].astype(F32)
        sg = _sigmoid(a)
        o_ref[:, :DFF] = (d * b * sg * (1.0 + a * (1.0 - sg))).astype(BF16)
        o_ref[:, DFF:] = (d * a * sg).astype(BF16)

    return _rows_call(body, "swiglu_bwd", lq, (ab, ab, d_act), [_rb(ab, DFF, 0), _rb(ab, DFF, 1), _rb(d_act)],
                      jax.ShapeDtypeStruct((lq, 2 * DFF), BF16), pl.BlockSpec((TM, 2 * DFF), lambda i: (i, 0)))


def loss_head(h1, ff, g_post, target):
    lq = h1.shape[0]
    front_blocks = 1
    rb = lambda a: _rb(a, tm=FRONT)

    def body(h_ref, f_ref, g_ref, t_ref, loss_ref, dh_ref, df_ref, dg_ref):
        i = pl.program_id(0)

        @pl.when(i == 0)
        def _():
            loss_ref[...] = jnp.zeros_like(loss_ref)
            dg_ref[...] = jnp.zeros_like(dg_ref)

        g = g_ref[...]
        fhat, r = _rms_hat(f_ref[...])
        is_x = (i >= front_blocks).astype(F32)
        diff = (h_ref[...] + fhat * g - t_ref[...]) * is_x
        loss_ref[...] += 0.5 * jnp.sum(diff * diff) / D
        dy = diff / D
        dh_ref[...] = dy
        df_ref[...] = _rms_bwd(fhat, r, g, dy).astype(BF16)
        dg_ref[...] += jnp.sum(dy * fhat, axis=0, keepdims=True)

    return _rows_call(
        body, "loss_head", lq, (h1, ff, g_post, target),
        [rb(h1), rb(ff), _whole(g_post),
         pl.BlockSpec((FRONT, D), lambda i: (jnp.maximum(i - front_blocks, 0), 0))],
        (jax.ShapeDtypeStruct((8, 128), F32), jax.ShapeDtypeStruct((lq, D), F32),
         jax.ShapeDtypeStruct((lq, D), BF16), jax.ShapeDtypeStruct((GAIN_ROWS, D), F32)),
        (pl.BlockSpec((8, 128), lambda i: (0, 0)), rb(h1), rb(h1), pl.BlockSpec((GAIN_ROWS, D), lambda i: (0, 0))),
        tm=FRONT)


def post_mix_bwd(h1, d_hn2, g_pre, d_h2, mix, g_post):
    lq = h1.shape[0]

    def body(h_ref, dn_ref, g3_ref, dh2_ref, m_ref, g2_ref, dh1_ref, dm_ref, dg3_ref, dg2_ref):
        i = pl.program_id(0)

        @pl.when(i == 0)
        def _():
            dg3_ref[...] = jnp.zeros_like(dg3_ref)
            dg2_ref[...] = jnp.zeros_like(dg2_ref)

        hhat, r = _rms_hat(h_ref[...])
        dn = dn_ref[...]
        d_h1 = dh2_ref[...] + _rms_bwd(hhat, r, g3_ref[...], dn)
        dh1_ref[...] = d_h1
        dg3_ref[...] += jnp.sum(dn * hhat, axis=0, keepdims=True)
        mhat, rm = _rms_hat(m_ref[...])
        dm_ref[...] = _rms_bwd(mhat, rm, g2_ref[...], d_h1).astype(BF16)
        dg2_ref[...] += jnp.sum(d_h1 * mhat, axis=0, keepdims=True)

    vec = jax.ShapeDtypeStruct((GAIN_ROWS, D), F32)
    vspec = pl.BlockSpec((GAIN_ROWS, D), lambda i: (0, 0))
    return _rows_call(body, "post_mix_bwd", lq, (h1, d_hn2, g_pre, d_h2, mix, g_post),
                      [_rb(h1), _rb(d_hn2), _whole(g_pre), _rb(d_h2), _rb(mix), _whole(g_post)],
                      (jax.ShapeDtypeStruct((lq, D), F32), jax.ShapeDtypeStruct((lq, D), BF16), vec, vec),
                      (_rb(h1), _rb(h1), vspec, vspec))


def pre_mix_bwd(hp, d_hn1, g_pre, d_h1):
    lq = hp.shape[0]

    def body(h_ref, dn_ref, g_ref, dh1_ref, dhp_ref, dg_ref):
        i = pl.program_id(0)

        @pl.when(i == 0)
        def _():
            dg_ref[...] = jnp.zeros_like(dg_ref)

        hhat, r = _rms_hat(h_ref[...])
        dn = dn_ref[...]
        dhp_ref[...] = dh1_ref[...] + _rms_bwd(hhat, r, g_ref[...], dn)
        dg_ref[...] += jnp.sum(dn * hhat, axis=0, keepdims=True)

    return _rows_call(body, "pre_mix_bwd", lq, (hp, d_hn1, g_pre, d_h1),
                      [_rb(hp), _rb(d_hn1), _whole(g_pre), _rb(d_h1)],
                      (jax.ShapeDtypeStruct((lq, D), F32), jax.ShapeDtypeStruct((GAIN_ROWS, D), F32)),
                      (_rb(hp), pl.BlockSpec((GAIN_ROWS, D), lambda i: (0, 0))))


def _retention_tables():
    log_g = jnp.log1p(-(2.0 ** (-5.0 - jnp.arange(RH, dtype=F32))))
    idx = jnp.arange(CHUNK, dtype=F32)
    diff = idx[:, None] - idx[None, :]
    decay = jnp.where(diff >= 0, jnp.exp(log_g[:, None, None] * jnp.maximum(diff, 0.0)), 0.0)
    zeta = jnp.exp(log_g[:, None] * (CHUNK - 1.0 - idx))
    xi = jnp.exp(log_g[:, None] * (idx + 1.0))
    g_chunk = jnp.broadcast_to(jnp.exp(log_g * CHUNK)[:, None], (RH, CHUNK))
    coef = jnp.stack([xi, zeta, g_chunk] + [jnp.zeros_like(xi)] * 125, axis=-1)
    return decay, coef


def _rotated_qk(p_ref, h, cos, sin):
    half = RDK // 2
    out = []
    for col, scale in ((C_RQ, RDK ** -0.5), (C_RK, 1.0)):
        x1 = p_ref[:, col + h * RDK: col + h * RDK + half].astype(F32)
        x2 = p_ref[:, col + h * RDK + half: col + (h + 1) * RDK].astype(F32)
        out.append(jnp.concatenate([(x1 * cos - x2 * sin) * scale, (x1 * sin + x2 * cos) * scale],
                                   axis=1).astype(BF16))
    return out


RET_COLS = C_RG + RH * RDV


def retention_fwd(proj, cos, sin, decay, coef):
    lq = proj.shape[0]
    n = lq // CHUNK

    def body(p_ref, c_ref, s_ref, dec_ref, cf_ref, o_ref, gr_ref, st_ref, state):
        @pl.when(pl.program_id(0) == 0)
        def _():
            state[...] = jnp.zeros_like(state)

        cos_, sin_ = c_ref[...], s_ref[...]
        for h in range(RH):
            vv = slice(h * RDV, (h + 1) * RDV)
            q, k = _rotated_qk(p_ref, h, cos_, sin_)
            v = p_ref[:, C_RV + h * RDV: C_RV + (h + 1) * RDV]
            xi, zeta, gch = cf_ref[h, :, 0:1], cf_ref[h, :, 1:2], cf_ref[h, 0:1, 2:3]
            st = state[h]
            stb = st.astype(BF16)
            st_ref[h] = stb
            s = _dot_nt(q, k) * dec_ref[h]
            o = _dot(s.astype(BF16), v) + _dot(q, stb) * xi
            o_ref[:, vv] = o
            yh, _ = _gn(o)
            rg = p_ref[:, C_RG + h * RDV: C_RG + (h + 1) * RDV].astype(F32)
            gr_ref[:, vv] = (rg * _sigmoid(rg) * yh).astype(BF16)
            vz = (v.astype(F32) * zeta).astype(BF16)
            state[h] = gch * st + _dot_tn(k, vz)

    v_spec = pl.BlockSpec((CHUNK, RH * RDV), lambda c: (c, 0))
    pos_spec = pl.BlockSpec((CHUNK, RDK // 2), lambda c: (c, 0))
    return pl.pallas_call(
        body, name="retention_fwd", grid=(n,),
        in_specs=[pl.BlockSpec((CHUNK, RET_COLS), lambda c: (c, 0)), pos_spec, pos_spec,
                  pl.BlockSpec((RH, CHUNK, CHUNK), lambda c: (0, 0, 0)),
                  pl.BlockSpec((RH, CHUNK, 128), lambda c: (0, 0, 0))],
        out_specs=(v_spec, v_spec, pl.BlockSpec((None, RH, RDK, RDV), lambda c: (c, 0, 0, 0))),
        out_shape=(jax.ShapeDtypeStruct((lq, RH * RDV), F32), jax.ShapeDtypeStruct((lq, RH * RDV), BF16),
                   jax.ShapeDtypeStruct((n, RH, RDK, RDV), BF16)),
        scratch_shapes=[pltpu.VMEM((RH, RDK, RDV), F32)],
        compiler_params=_params("arbitrary"),
    )(proj, cos, sin, decay, coef)


def retention_bwd(proj, cos, sin, o_ret, d_gr, states, decay, coef, d_proj):
    lq = proj.shape[0]
    n = lq // CHUNK
    half = RDK // 2

    def body(p_ref, c_ref, s_ref, y_ref, dgr_ref, st_ref, dec_ref, cf_ref, _, dp_ref, dstate):
        @pl.when(pl.program_id(0) == 0)
        def _():
            dstate[...] = jnp.zeros_like(dstate)

        cos_, sin_ = c_ref[...], s_ref[...]
        for h in range(RH):
            vv = slice(h * RDV, (h + 1) * RDV)
            q, k = _rotated_qk(p_ref, h, cos_, sin_)
            v = p_ref[:, C_RV + h * RDV: C_RV + (h + 1) * RDV]
            yh, rs = _gn(y_ref[:, vv])
            rg = p_ref[:, C_RG + h * RDV: C_RG + (h + 1) * RDV].astype(F32)
            sg = _sigmoid(rg)
            d = dgr_ref[:, vv].astype(F32)
            dp_ref[:, C_RG + h * RDV: C_RG + (h + 1) * RDV] = (d * yh * sg * (1.0 + rg * (1.0 - sg))).astype(BF16)
            dob = _gn_bwd(yh, rs, d * rg * sg).astype(BF16)
            xi, zeta, gch = cf_ref[h, :, 0:1], cf_ref[h, :, 1:2], cf_ref[h, 0:1, 2:3]
            dec = dec_ref[h]
            dsn = dstate[h]
            dsnb = dsn.astype(BF16)
            dox = (dob.astype(F32) * xi).astype(BF16)
            sb = (_dot_nt(q, k) * dec).astype(BF16)
            dsb = (_dot_nt(dob, v) * dec).astype(BF16)
            vz = (v.astype(F32) * zeta).astype(BF16)
            dq = _dot(dsb, k) + _dot_nt(dox, st_ref[h])
            dk = _dot_tn(dsb, q) + _dot_nt(vz, dsnb)
            dp_ref[:, C_RV + h * RDV: C_RV + (h + 1) * RDV] = (_dot_tn(sb, dob) + _dot(k, dsnb) * zeta).astype(BF16)
            dstate[h] = gch * dsn + _dot_tn(q, dox)
            for col, g, scale in ((C_RQ, dq, RDK ** -0.5), (C_RK, dk, 1.0)):
                d1, d2 = g[:, :half], g[:, half:]
                dp_ref[:, col + h * RDK: col + h * RDK + half] = ((d1 * cos_ + d2 * sin_) * scale).astype(BF16)
                dp_ref[:, col + h * RDK + half: col + (h + 1) * RDK] = ((d2 * cos_ - d1 * sin_) * scale).astype(BF16)

    rev = lambda c: n - 1 - c
    v_spec = pl.BlockSpec((CHUNK, RH * RDV), lambda c: (rev(c), 0))
    pos_spec = pl.BlockSpec((CHUNK, half), lambda c: (rev(c), 0))
    ret_cols = pl.BlockSpec((CHUNK, RET_COLS), lambda c: (rev(c), 0))
    return pl.pallas_call(
        body, name="retention_bwd", grid=(n,),
        in_specs=[ret_cols, pos_spec, pos_spec, v_spec, v_spec,
                  pl.BlockSpec((None, RH, RDK, RDV), lambda c: (rev(c), 0, 0, 0)),
                  pl.BlockSpec((RH, CHUNK, CHUNK), lambda c: (0, 0, 0)),
                  pl.BlockSpec((RH, CHUNK, 128), lambda c: (0, 0, 0)), ANY_SPEC],
        out_specs=ret_cols, out_shape=_d_proj_shape(lq),
        scratch_shapes=[pltpu.VMEM((RH, RDK, RDV), F32)],
        input_output_aliases={8: 0}, compiler_params=_params("arbitrary"),
    )(proj, cos, sin, o_ret, d_gr, states, decay, coef, d_proj)


def _sb_masks_and_u():
    lane = lax.broadcasted_iota(jnp.int32, (1, 2 * SB_DH), 1)
    lo = lane < SB_DH
    row = lax.broadcasted_iota(jnp.int32, (SB_T, SB_T), 0)
    col = lax.broadcasted_iota(jnp.int32, (SB_T, SB_T), 1)
    u = (row > col).astype(BF16)
    return lo, u, row, col


def _sb_rows(j):
    start = j * SB_T
    return pl.ds(start if isinstance(j, int) else pl.multiple_of(start, SB_T), SB_T)


SB_DEAD = -104.0
SB_ROWS = SB_T
SB_CHAINS = tuple((hh, slice(r, r + SB_ROWS)) for hh in range(2) for r in range(0, SB_T, SB_ROWS))


def _sb_alive(logs):
    m = logs[0]
    for l in logs[1:]:
        m = jnp.maximum(m, l)
    return (jnp.max(m) > SB_DEAD).astype(jnp.int32)


def _sb_walk(i, run, logs_of, step):
    def cond(c):
        return (c[0] >= 1) & (c[1] > 0)

    def body(c):
        r = step(c[0], c[2])
        return c[0] - 1, _sb_alive(logs_of(r)), r

    return lax.while_loop(cond, body, (i - 1, _sb_alive(logs_of(run)), run))


def _sb_logs(z):
    ls = jnp.minimum(z, 0.0) - jnp.log(1.0 + jnp.exp(-jnp.abs(z)))
    return ls, ls - z


def sb_fwd(proj, exch):
    lq = proj.shape[0]
    t = SB_T

    def body(q_ref, k_ref, v_ref, o_ref, of_ref, acc_ref):
        i = pl.program_id(1)
        lo, u, row, col = _sb_masks_and_u()
        qs = (q_ref[...].astype(F32) * SB_DH ** -0.5).astype(BF16)
        zero = jnp.zeros_like(qs)
        qh = (jnp.where(lo, qs, zero), jnp.where(lo, zero, qs))
        acc_ref[...] = jnp.zeros_like(acc_ref)

        def block(j, run, masked):
            rows = _sb_rows(j)
            ks, vs = k_ref[rows, :], v_ref[rows, :]
            if masked:
                valid = (col + j * t < row + i * t) & (col + j * t >= META0)
            out = []
            for hh, rs in SB_CHAINS:
                ls, ln = _sb_logs(_dot_nt(qh[hh][rs], ks))
                if masked:
                    ln = jnp.where(valid[rs], ln, 0.0)
                a = jnp.exp(ls + _dot(ln.astype(BF16), u) + run[len(out)])
                if masked:
                    a = jnp.where(valid[rs], a, 0.0)
                acc_ref[hh, rs, :] += _dot(a.astype(BF16), vs)
                out.append(run[len(out)] + jnp.sum(ln, axis=1, keepdims=True))
            return tuple(out)

        zeros = jnp.zeros((SB_ROWS, 1), F32)
        run = block(i, (zeros,) * len(SB_CHAINS), True)
        _, go, run = _sb_walk(i, run, lambda r: r, lambda j, r: block(j, r, False))

        @pl.when((i > 0) & (go > 0))
        def _():
            block(0, run, True)

        o = jnp.where(lo, acc_ref[0], acc_ref[1])
        o_ref[...] = o.astype(BF16)
        of_ref[...] = o

    blk = pl.BlockSpec((t, 128), lambda p, i: (i, p))
    return _grid_call(
        body, "sb_fwd", (D // 128, lq // t), (proj, proj, proj),
        [pl.BlockSpec((t, 128), lambda p, i: (i, C_SQ // 128 + p)),
         pl.BlockSpec((lq, 128), lambda p, i: (0, C_SK // 128 + p)),
         pl.BlockSpec((lq, 128), lambda p, i: (0, C_SV // 128 + p))],
        (jax.ShapeDtypeStruct((lq, D), BF16), jax.ShapeDtypeStruct((lq, D), F32)), (blk, blk),
        [pltpu.VMEM((2, t, 128), F32)], exch)


def sb_bwd(proj, o, d_o, d_proj, exch):
    lq = proj.shape[0]
    t = SB_T
    last = lq // t - 1

    def body(q_ref, k_ref, v_ref, o_ref, do_ref, _, dq_ref, dk_out, dv_out, acc_ref, dk_ref, dv_ref):
        i = pl.program_id(1)

        @pl.when(i == 0)
        def _():
            dk_ref[...] = jnp.zeros_like(dk_ref)
            dv_ref[...] = jnp.zeros_like(dv_ref)

        lo, u, row, col = _sb_masks_and_u()
        incl = (row <= col).astype(BF16)
        qs = (q_ref[...].astype(F32) * SB_DH ** -0.5).astype(BF16)
        do = do_ref[...]
        zero = jnp.zeros_like(qs)
        qh = (jnp.where(lo, qs, zero), jnp.where(lo, zero, qs))
        doh = (jnp.where(lo, do, zero), jnp.where(lo, zero, do))
        prod = o_ref[...] * do.astype(F32)
        dsum = (jnp.sum(jnp.where(lo, prod, 0.0), axis=1, keepdims=True),
                jnp.sum(jnp.where(lo, 0.0, prod), axis=1, keepdims=True))
        acc_ref[...] = jnp.zeros_like(acc_ref)

        def block(j, run, masked):
            rows = _sb_rows(j)
            ks, vs = k_ref[rows, :], v_ref[rows, :]
            if masked:
                valid = (col + j * t < row + i * t) & (col + j * t >= META0)
            out = []
            for hh in range(2):
                run_ln, run_e = run[2 * hh], run[2 * hh + 1]
                ls, ln = _sb_logs(_dot_nt(qh[hh], ks))
                if masked:
                    ln = jnp.where(valid, ln, 0.0)
                a = jnp.exp(ls + _dot(ln.astype(BF16), u) + run_ln)
                if masked:
                    a = jnp.where(valid, a, 0.0)
                ab = a.astype(BF16)
                e = ab.astype(F32) * _dot_nt(doh[hh], vs)
                e_sum = jnp.sum(e, axis=1, keepdims=True)
                upto = (dsum[hh] - run_e - e_sum) + _dot(e.astype(BF16), incl)
                dz = e - jnp.exp(ls) * upto
                if masked:
                    dz = jnp.where(valid, dz, 0.0)
                dzb = dz.astype(BF16)
                acc_ref[hh] += _dot(dzb, ks)
                dk_ref[rows, :] += _dot_tn(dzb, qh[hh])
                dv_ref[rows, :] += _dot_tn(ab, doh[hh])
                out += [run_ln + jnp.sum(ln, axis=1, keepdims=True), run_e + e_sum]
            return tuple(out)

        zeros = jnp.zeros((t, 1), F32)
        run = block(i, (zeros,) * 4, True)
        _, go, run = _sb_walk(i, run, lambda r: (r[0], r[2]), lambda j, r: block(j, r, False))

        @pl.when((i > 0) & (go > 0))
        def _():
            block(0, run, True)

        dq_ref[...] = (jnp.where(lo, acc_ref[0], acc_ref[1]) * SB_DH ** -0.5).astype(BF16)

        @pl.when(i == last)
        def _():
            dk_out[...] = dk_ref[...].astype(BF16)
            dv_out[...] = dv_ref[...].astype(BF16)

    blk = pl.BlockSpec((t, 128), lambda p, i: (i, p))
    q_cols = pl.BlockSpec((t, 128), lambda p, i: (i, C_SQ // 128 + p))
    col_blk = pl.BlockSpec((lq, 128), lambda p, i: (0, p))
    return _grid_call(
        body, "sb_bwd", (D // 128, lq // t), (proj, proj, proj, o, d_o, d_proj),
        [q_cols, pl.BlockSpec((lq, 128), lambda p, i: (0, C_SK // 128 + p)),
         pl.BlockSpec((lq, 128), lambda p, i: (0, C_SV // 128 + p)), blk, blk, ANY_SPEC],
        (_d_proj_shape(lq), jax.ShapeDtypeStruct((lq, D), BF16), jax.ShapeDtypeStruct((lq, D), BF16)),
        (q_cols, col_blk, col_blk),
        [pltpu.VMEM((2, t, 128), F32), pltpu.VMEM((lq, 128), F32), pltpu.VMEM((lq, 128), F32)], exch, {5: 0})


def all_gather(blocks):
    nb = len(blocks)

    def body(*refs):
        x_refs, out_refs = refs[:nb], refs[nb:2 * nb]
        send_sems, recv_sems, local_sems = refs[2 * nb:]
        x, y, c = lax.axis_index("x"), lax.axis_index("y"), lax.axis_index("c")
        me, sibling = (x, y, c), (x, y, 1 - c)
        chips = [(1 - x, y), (x, 1 - y), (1 - x, 1 - y)]

        def copy(b, k, block, to, src=None):
            slot = out_refs[b].at[_device_index(*block)]
            return pltpu.make_async_remote_copy(
                src_ref=slot if src is None else src, dst_ref=slot,
                send_sem=send_sems.at[b, k], recv_sem=recv_sems.at[b, k],
                device_id=to, device_id_type=pl.DeviceIdType.MESH)

        mine = [pltpu.make_async_copy(x_refs[b], out_refs[b].at[_device_index(*me)], local_sems.at[b])
                for b in range(nb)]
        for cp in mine:
            cp.start()
        first = []
        for b in range(nb):
            first.append(copy(b, 0, me, sibling, src=x_refs[b]))
            first += [copy(b, 1 + j, me, (*chip, c), src=x_refs[b]) for j, chip in enumerate(chips)]
        for cp in first:
            cp.start()
        passed = []
        for j, chip in enumerate(chips):
            for b in range(nb):
                copy(b, 1 + j, (*chip, c), me).wait_recv()
                cp = copy(b, 4 + j, (*chip, c), sibling)
                cp.start()
                passed.append(cp)
        for b in range(nb):
            copy(b, 0, sibling, me).wait_recv()
            for j, chip in enumerate(chips):
                copy(b, 4 + j, (*chip, 1 - c), me).wait_recv()
        for cp in first + passed:
            cp.wait_send()
        for cp in mine:
            cp.wait()

    any_spec = pl.BlockSpec(memory_space=pl.ANY)
    return pl.pallas_call(
        body, name="all_gather",
        in_specs=[any_spec] * nb, out_specs=[any_spec] * nb,
        out_shape=[jax.ShapeDtypeStruct((NDEV,) + b.shape, b.dtype) for b in blocks],
        scratch_shapes=[pltpu.SemaphoreType.DMA((nb, 7)), pltpu.SemaphoreType.DMA((nb, 7)),
                        pltpu.SemaphoreType.DMA((nb,))],
    )(*blocks)


def gather_small(small):
    def body(s_ref, land_ref, send_sems, recv_sems, local_sem):
        start, wait = _direct_exchange(lambda d: s_ref, land_ref, send_sems, recv_sems, local_sem)
        start()
        wait()

    any_spec = pl.BlockSpec(memory_space=pl.ANY)
    return pl.pallas_call(
        body, name="gather_small", in_specs=[any_spec], out_specs=any_spec,
        out_shape=jax.ShapeDtypeStruct((NDEV,) + small.shape, small.dtype),
        scratch_shapes=[pltpu.SemaphoreType.DMA((NDEV - 1,)), pltpu.SemaphoreType.DMA((NDEV - 1,)),
                        pltpu.SemaphoreType.DMA(())],
    )(small)


def sum_slots(landed, name):
    _, r, c = landed.shape
    tr = _tile(r, (352, 224, 8))

    def body(l_ref, o_ref):
        acc = l_ref[0].astype(F32)
        for p in range(1, NDEV):
            acc = acc + l_ref[p].astype(F32)
        o_ref[...] = acc

    return pl.pallas_call(
        body, name=name, grid=(r // tr,),
        in_specs=[pl.BlockSpec((NDEV, tr, c), lambda i: (0, i, 0))],
        out_specs=pl.BlockSpec((tr, c), lambda i: (i, 0)),
        out_shape=jax.ShapeDtypeStruct((r, c), F32),
        compiler_params=_params("parallel"),
    )(landed)


def adamw(w, g, m, v, name):
    r, c = w.shape
    tr = _tile(r, (256, 128))

    def body(w_ref, g_ref, m_ref, v_ref, d_ref, nm_ref, nv_ref):
        g_ = g_ref[...]
        m_ = ADAM_B1 * m_ref[...] + (1.0 - ADAM_B1) * g_
        v_ = ADAM_B2 * v_ref[...] + (1.0 - ADAM_B2) * jnp.square(g_)
        m_hat = m_ / (1.0 - ADAM_B1 ** ADAM_STEP)
        v_hat = v_ / (1.0 - ADAM_B2 ** ADAM_STEP)
        d_ref[...] = -ADAM_LR * (m_hat / (jnp.sqrt(v_hat) + ADAM_EPS) + ADAM_WD * w_ref[...])
        nm_ref[...] = m_
        nv_ref[...] = v_

    spec = pl.BlockSpec((tr, c), lambda i: (i, 0))
    out = jax.ShapeDtypeStruct((r, c), F32)
    return pl.pallas_call(
        body, name=name, grid=(r // tr,), in_specs=[spec] * 4, out_specs=(spec,) * 3, out_shape=(out,) * 3,
        compiler_params=_params("parallel"),
    )(w, g, m, v)


def kernel(x, meta_tokens, w_in, w_ret_out, w_sb_out, w_out, w_ffn_in, w_ffn_out, norm_mix_pre, norm_mix_post, norm_ffn_pre, norm_ffn_post, loss_target, m_meta_tokens, m_w_in, m_w_ret_out, m_w_sb_out, m_w_out, m_w_ffn_in, m_w_ffn_out, m_norm_mix_pre, m_norm_mix_post, m_norm_ffn_pre, m_norm_ffn_post, v_meta_tokens, v_w_in, v_w_ret_out, v_w_sb_out, v_w_out, v_w_ffn_in, v_w_ffn_out, v_norm_mix_pre, v_norm_mix_post, v_norm_ffn_pre, v_norm_ffn_post):
    seq = x.shape[1]
    lq = seq + FRONT
    me = _device_index(lax.axis_index("x"), lax.axis_index("y"), lax.axis_index("c"))

    shards = {"w_in_t": w_in[0].T, "w_ffn_in_t": w_ffn_in[0].T, "w_ret_out": w_ret_out[0],
              "w_sb_out": w_sb_out[0], "w_out": w_out[0], "w_ffn_out": w_ffn_out[0]}
    rest_rows = PACK_ROWS[1:]
    pack_rest = jnp.concatenate([shards[n].astype(BF16) for n, _ in rest_rows], axis=0)
    gathered_in, meta_all = all_gather([shards["w_in_t"].astype(BF16), meta_tokens])
    full = {"w_in_t": gathered_in.reshape(PROJ, D)}
    meta_full = meta_all.transpose(1, 0, 2).reshape(N_META, D)

    pos = jnp.arange(lq, dtype=F32) - META0
    half = RDK // 2
    ang = pos[:, None] * (ROPE_BASE ** (-jnp.arange(half, dtype=F32) / half))[None, :]
    cos, sin = jnp.cos(ang), jnp.sin(ang)
    decay, coef = _retention_tables()

    hp = jnp.concatenate([jnp.zeros((META0, D), F32), meta_full, x[0]], axis=0)
    hn1 = rms_fwd(hp, norm_mix_pre, "rms_mix_pre")
    proj = mm_nt(hn1, full["w_in_t"], BF16, "proj")
    o_ret, gr, states = retention_fwd(proj, cos, sin, decay, coef)
    o_sb, o_sb_f32, gathered_rest = sb_fwd(proj, (pack_rest, "gather"))
    off = 0
    for n, r in rest_rows:
        full[n] = gathered_rest[:, off:off + r, :].reshape(NDEV * r, D)
        off += r
    y_ret = mm_nn(gr, full["w_ret_out"], F32, "y_ret")
    y_sb = mm_nn(o_sb, full["w_sb_out"], F32, "y_sb")
    merged = merge_fwd(proj, y_ret, y_sb)
    mix = mm_nn(merged, full["w_out"], F32, "mix")
    h1, hn2 = post_mix_fwd(hp, mix, norm_mix_post, norm_ffn_pre)
    ab = mm_nt(hn2, full["w_ffn_in_t"], BF16, "ffn_in")
    act = swiglu_fwd(ab)
    ff = mm_nn(act, full["w_ffn_out"], F32, "ffn_out")
    loss_blk, d_h2, d_ff, dg_ffn_post = loss_head(h1, ff, norm_ffn_post, loss_target[0])
    loss = lax.psum(loss_blk[0, 0], ("x", "y", "c"))

    grads = {}
    d_act = mm_nt(d_ff, full["w_ffn_out"], BF16, "d_act")
    grads["w_ffn_out"] = mm_tn(act, d_ff, BF16, "dw_ffn_out")
    d_ab = swiglu_bwd(ab, d_act)
    d_hn2 = mm_nn(d_ab, full["w_ffn_in_t"], F32, "d_hn2")
    grads["w_ffn_in_t"] = mm_tn(d_ab, hn2, BF16, "dw_ffn_in")
    d_h1, d_mix, dg_ffn_pre, dg_mix_post = post_mix_bwd(h1, d_hn2, norm_ffn_pre, d_h2, mix, norm_mix_post)
    d_merged = mm_nt(d_mix, full["w_out"], BF16, "d_merged")
    grads["w_out"] = mm_tn(merged, d_mix, BF16, "dw_out")
    d_y_ret, d_y_sb, d_proj = merge_bwd(proj, y_ret, y_sb, d_merged)
    d_gr = mm_nt(d_y_ret, full["w_ret_out"], BF16, "d_gr")
    grads["w_ret_out"] = mm_tn(gr, d_y_ret, BF16, "dw_ret_out")
    d_o_sb = mm_nt(d_y_sb, full["w_sb_out"], BF16, "d_o_sb")
    grads["w_sb_out"] = mm_tn(o_sb, d_y_sb, BF16, "dw_sb_out")
    d_proj = retention_bwd(proj, cos, sin, o_ret, d_gr, states, decay, coef, d_proj)
    parts_rest = jnp.concatenate([grads[n].reshape(NDEV, r, D) for n, r in rest_rows], axis=1)
    d_proj, d_sk, d_sv, landed_rest = sb_bwd(proj, o_sb_f32, d_o_sb, d_proj, (parts_rest, "scatter"))
    d_proj = lax.dynamic_update_slice(d_proj, d_sk, (0, C_SK))
    d_proj = lax.dynamic_update_slice(d_proj, d_sv, (0, C_SV))
    dw_in_t = mm_tn(d_proj, hn1, BF16, "dw_in")
    d_hn1, landed_in = mm_nn(d_proj, full["w_in_t"], F32, "d_hn1",
                             exch=(dw_in_t.reshape(NDEV, PROJ // NDEV, D), "scatter"))
    d_hp, dg_mix_pre = pre_mix_bwd(hp, d_hn1, norm_mix_pre, d_h1)
    grad_x = d_hp[FRONT:][None]
    small = jnp.concatenate([dg_mix_pre, dg_mix_post, dg_ffn_pre, dg_ffn_post, d_hp[META0:FRONT]], axis=0)
    ssum = sum_slots(gather_small(small), "sum_small")
    g = {"w_in_t": sum_slots(landed_in, "sum_grads_in")}
    gsum = sum_slots(landed_rest, "sum_grads_rest")
    off = 0
    for n, r in rest_rows:
        g[n] = gsum[off:off + r]
        off += r
    gain = lambda k: ssum[k * GAIN_ROWS:k * GAIN_ROWS + 1]
    g_w = {"meta_tokens": lax.dynamic_slice(ssum[4 * GAIN_ROWS:], (0, me * (D // NDEV)), (N_META, D // NDEV)),
           "w_in": g["w_in_t"].T[None], "w_ret_out": g["w_ret_out"][None], "w_sb_out": g["w_sb_out"][None],
           "w_out": g["w_out"][None], "w_ffn_in": g["w_ffn_in_t"].T[None], "w_ffn_out": g["w_ffn_out"][None],
           "norm_mix_pre": gain(0), "norm_mix_post": gain(1), "norm_ffn_pre": gain(2), "norm_ffn_post": gain(3)}

    names = ["meta_tokens", "w_in", "w_ret_out", "w_sb_out", "w_out", "w_ffn_in", "w_ffn_out",
             "norm_mix_pre", "norm_mix_post", "norm_ffn_pre", "norm_ffn_post"]
    w_of = dict(zip(names, (meta_tokens, w_in, w_ret_out, w_sb_out, w_out, w_ffn_in, w_ffn_out,
                            norm_mix_pre, norm_mix_post, norm_ffn_pre, norm_ffn_post)))
    m_of = dict(zip(names, (m_meta_tokens, m_w_in, m_w_ret_out, m_w_sb_out, m_w_out, m_w_ffn_in, m_w_ffn_out,
                            m_norm_mix_pre, m_norm_mix_post, m_norm_ffn_pre, m_norm_ffn_post)))
    v_of = dict(zip(names, (v_meta_tokens, v_w_in, v_w_ret_out, v_w_sb_out, v_w_out, v_w_ffn_in, v_w_ffn_out,
                            v_norm_mix_pre, v_norm_mix_post, v_norm_ffn_pre, v_norm_ffn_post)))
    delta, new_m, new_v = {}, {}, {}
    for n in names:
        shape = w_of[n].shape
        two_d = (shape[-2], shape[-1])
        d_, m_, v_ = adamw(w_of[n].reshape(two_d), g_w[n].reshape(two_d), m_of[n].reshape(two_d),
                           v_of[n].reshape(two_d), "adamw_" + n)
        delta[n], new_m[n], new_v[n] = d_.reshape(shape), m_.reshape(shape), v_.reshape(shape)

    return (loss, grad_x, *[g_w[n] for n in names], *[delta[n] for n in names],
            *[new_m[n] for n in names], *[new_v[n] for n in names])
```

```python
import jax
import jax.numpy as jnp
from jax import lax
from jax.experimental import pallas as pl
from jax.experimental.pallas import tpu as pltpu

F32 = jnp.float32
BF16 = jnp.bfloat16

D = 1024
N_META = 16
CHUNK = 128
FRONT = 256
META0 = FRONT - N_META
RH, RDK, RDV = 4, 256, 512
SB_DH = 64
DFF = 2816
NDEV = 8
ROPE_BASE = 10000.0
NORM_EPS = 1e-6
GN_EPS = 1e-5
C_RQ, C_RK, C_RV, C_RG, C_SQ, C_SK, C_SV, C_GA, C_GB = 0, 1024, 2048, 4096, 6144, 7168, 8192, 9216, 10240
PROJ = 11264
PACK_ROWS = (("w_in_t", PROJ // NDEV), ("w_ffn_in_t", 2 * DFF // NDEV), ("w_ret_out", RH * RDV // NDEV),
             ("w_sb_out", D // NDEV), ("w_out", D // NDEV), ("w_ffn_out", DFF // NDEV))
PACK = sum(r for _, r in PACK_ROWS)
GAIN_ROWS = 8

ADAM_LR = 0.001
ADAM_B1 = 0.9
ADAM_B2 = 0.999
ADAM_EPS = 1e-08
ADAM_WD = 0.01
ADAM_STEP = 10

VMEM_LIMIT = 56 * 1024 * 1024
SB_T = 256

NT = (((1,), (1,)), ((), ()))
TN = (((0,), (0,)), ((), ()))


def _dot(a, b):
    return jnp.dot(a, b, preferred_element_type=F32)


def _dot_nt(a, b):
    return lax.dot_general(a, b, NT, preferred_element_type=F32)


def _dot_tn(a, b):
    return lax.dot_general(a, b, TN, preferred_element_type=F32)


WIDE_TILES = (1024, 1408, 512, 256)


def _tile(n, candidates):
    for t in candidates:
        if n % t == 0:
            return t
    return n


def _params(*sem):
    return pltpu.CompilerParams(dimension_semantics=sem, vmem_limit_bytes=VMEM_LIMIT)


def _rms_hat(x):
    r = lax.rsqrt(jnp.mean(x * x, axis=-1, keepdims=True) + NORM_EPS)
    return x * r, r


def _rms_bwd(xhat, r, g, dy):
    u = dy * g
    return r * (u - xhat * jnp.mean(u * xhat, axis=-1, keepdims=True))


def _gn(y):
    mu = jnp.mean(y, axis=-1, keepdims=True)
    yc = y - mu
    rs = lax.rsqrt(jnp.mean(yc * yc, axis=-1, keepdims=True) + GN_EPS)
    return yc * rs, rs


def _gn_bwd(yh, rs, d):
    return rs * (d - jnp.mean(d, axis=-1, keepdims=True) - yh * jnp.mean(d * yh, axis=-1, keepdims=True))


def _sigmoid(x):
    return 1.0 / (1.0 + jnp.exp(-x))


def _device_index(px, py, pc):
    return 4 * px + 2 * py + pc


def _direct_exchange(src_for, land_ref, send_sems, recv_sems, local_sem):
    x, y, c = lax.axis_index("x"), lax.axis_index("y"), lax.axis_index("c")
    me = _device_index(x, y, c)
    peers = []
    for k in range(1, NDEV):
        pos = (1 - x if k & 4 else x, 1 - y if k & 2 else y, 1 - c if k & 1 else c)
        peers.append((k - 1, pos, _device_index(*pos)))

    def local():
        return pltpu.make_async_copy(src_for(me), land_ref.at[me], local_sem)

    def remote(k, pos, idx):
        return pltpu.make_async_remote_copy(
            src_ref=src_for(idx), dst_ref=land_ref.at[me], send_sem=send_sems.at[k], recv_sem=recv_sems.at[k],
            device_id=pos, device_id_type=pl.DeviceIdType.MESH)

    def arrival(k, idx):
        return pltpu.make_async_remote_copy(
            src_ref=src_for(idx), dst_ref=land_ref.at[idx], send_sem=send_sems.at[k], recv_sem=recv_sems.at[k],
            device_id=(x, y, c), device_id_type=pl.DeviceIdType.MESH)

    def start():
        local().start()
        for p in peers:
            remote(*p).start()

    def wait():
        for k, _, idx in peers:
            arrival(k, idx).wait_recv()
        for p in peers:
            remote(*p).wait_send()
        local().wait()

    return start, wait


def _grid_call(compute, name, grid, arrays, in_specs, out_shapes, out_specs, scratch, exch, aliases=None):
    arrays, in_specs = list(arrays), list(in_specs)
    out_shapes, out_specs, scratch = list(out_shapes), list(out_specs), list(scratch)
    n_in, n_out, steps = len(arrays), len(out_shapes), grid[0] * grid[1]
    aliases = aliases or {}
    if exch is None:
        return pl.pallas_call(
            compute, name=name, grid=grid, in_specs=in_specs, out_specs=out_specs, out_shape=out_shapes,
            scratch_shapes=scratch, input_output_aliases=aliases,
            compiler_params=_params("parallel", "arbitrary"))(*arrays)
    src, mode = exch

    def body(*refs):
        refs = list(refs)
        src_ref = refs.pop(n_in)
        land_ref = refs.pop(n_in + n_out)
        send_sems, recv_sems, local_sem = refs[-3:]
        src_for = (lambda d: src_ref) if mode == "gather" else (lambda d: src_ref.at[d])
        start, wait = _direct_exchange(src_for, land_ref, send_sems, recv_sems, local_sem)
        step = pl.program_id(0) * grid[1] + pl.program_id(1)
        pl.when(step == 0)(start)
        compute(*refs[:-3])
        pl.when(step == steps - 1)(wait)

    any_spec = pl.BlockSpec(memory_space=pl.ANY)
    land = src.shape if mode == "scatter" else (NDEV,) + src.shape
    return pl.pallas_call(
        body, name=name, grid=grid, in_specs=in_specs + [any_spec], out_specs=out_specs + [any_spec],
        out_shape=out_shapes + [jax.ShapeDtypeStruct(land, src.dtype)],
        scratch_shapes=scratch + [pltpu.SemaphoreType.DMA((NDEV - 1,)), pltpu.SemaphoreType.DMA((NDEV - 1,)),
                                  pltpu.SemaphoreType.DMA(())],
        input_output_aliases=aliases, compiler_params=_params("arbitrary", "arbitrary"))(*(arrays + [src]))


def _mm_call(compute, name, grid, arrays, in_specs, out_shape, out_spec, acc_shape, exch):
    if acc_shape:
        body, scratch = compute, [pltpu.VMEM(acc_shape, F32)]
    else:
        body, scratch = (lambda a_ref, b_ref, o_ref: compute(a_ref, b_ref, o_ref, None)), []
    out = _grid_call(body, name, grid, arrays, in_specs, [out_shape], [out_spec], scratch, exch)
    return tuple(out) if exch else out[0]


def mm_nt(a, b, out_dtype, name, exch=None):
    m, k = a.shape
    n = b.shape[0]
    tm = _tile(m, (1408, 768, 512, 256))
    tn = _tile(n, WIDE_TILES)

    def compute(a_ref, b_ref, o_ref, acc_ref):
        o_ref[...] = _dot_nt(a_ref[...], b_ref[...]).astype(o_ref.dtype)

    return _mm_call(
        compute, name, (m // tm, n // tn), (a, b),
        [pl.BlockSpec((tm, k), lambda i, j: (i, 0)), pl.BlockSpec((tn, k), lambda i, j: (j, 0))],
        jax.ShapeDtypeStruct((m, n), out_dtype), pl.BlockSpec((tm, tn), lambda i, j: (i, j)), None, exch)


def _accumulate(dot, steps):
    def compute(a_ref, b_ref, o_ref, acc_ref):
        kk = pl.program_id(1)

        @pl.when(kk == 0)
        def _():
            acc_ref[...] = jnp.zeros_like(acc_ref)

        acc_ref[...] += dot(a_ref[...], b_ref[...])

        @pl.when(kk == steps - 1)
        def _():
            o_ref[...] = acc_ref[...].astype(o_ref.dtype)

    return compute


def mm_nn(a, b, out_dtype, name, exch=None):
    m, k = a.shape
    n = b.shape[1]
    tm = _tile(m, (768, 512, 256))
    tk = _tile(k, (2816,) + WIDE_TILES)
    return _mm_call(
        _accumulate(_dot, k // tk), name, (m // tm, k // tk), (a, b),
        [pl.BlockSpec((tm, tk), lambda i, kk: (i, kk)), pl.BlockSpec((tk, n), lambda i, kk: (kk, 0))],
        jax.ShapeDtypeStruct((m, n), out_dtype), pl.BlockSpec((tm, n), lambda i, kk: (i, 0)), (tm, n), exch)


def mm_tn(a, b, out_dtype, name, exch=None):
    m, ka = a.shape
    n = b.shape[1]
    ta = _tile(ka, WIDE_TILES)
    tl = _tile(m, (1408, 768, 512, 256))
    return _mm_call(
        _accumulate(_dot_tn, m // tl), name, (ka // ta, m // tl), (a, b),
        [pl.BlockSpec((tl, ta), lambda i, ll: (ll, i)), pl.BlockSpec((tl, n), lambda i, ll: (ll, 0))],
        jax.ShapeDtypeStruct((ka, n), out_dtype), pl.BlockSpec((ta, n), lambda i, ll: (i, 0)), (ta, n), exch)


TM = 384


def _rb(arr, width=None, col_block=0, tm=TM):
    w = arr.shape[1] if width is None else width
    return pl.BlockSpec((tm, w), lambda i: (i, col_block))


def _whole(arr):
    return pl.BlockSpec(arr.shape, lambda i: (0,) * arr.ndim)


def _rows_call(body, name, lq, ins, in_specs, out_shapes, out_specs, aliases=None, tm=TM):
    return pl.pallas_call(
        body, name=name, grid=(lq // tm,), in_specs=in_specs, out_specs=out_specs, out_shape=out_shapes,
        input_output_aliases=aliases or {}, compiler_params=_params("arbitrary"),
    )(*ins)


ANY_SPEC = pl.BlockSpec(memory_space=pl.ANY)


def _d_proj_shape(lq):
    return jax.ShapeDtypeStruct((lq, PROJ), BF16)


def rms_fwd(h, g, name):
    lq = h.shape[0]

    def body(h_ref, g_ref, o_ref):
        xhat, _ = _rms_hat(h_ref[...])
        o_ref[...] = (xhat * g_ref[...]).astype(BF16)

    return _rows_call(body, name, lq, (h, g), [_rb(h), _whole(g)],
                      jax.ShapeDtypeStruct((lq, D), BF16), _rb(h))


def merge_fwd(proj, y_ret, y_sb):
    lq = proj.shape[0]

    def body(ga_ref, gb_ref, yr_ref, ys_ref, o_ref):
        o_ref[...] = (_sigmoid(ga_ref[...].astype(F32)) * yr_ref[...]
                      + _sigmoid(gb_ref[...].astype(F32)) * ys_ref[...]).astype(BF16)

    return _rows_call(body, "merge_fwd", lq, (proj, proj, y_ret, y_sb),
                      [_rb(proj, D, C_GA // D), _rb(proj, D, C_GB // D), _rb(y_ret), _rb(y_sb)],
                      jax.ShapeDtypeStruct((lq, D), BF16), _rb(y_ret))


def merge_bwd(proj, y_ret, y_sb, d_merged):
    lq = proj.shape[0]

    def body(gate_ref, yr_ref, ys_ref, d_ref, dyr_ref, dys_ref, dg_ref):
        d = d_ref[...].astype(F32)
        sg = _sigmoid(gate_ref[...].astype(F32))
        for g, y_ref, dy_ref in ((0, yr_ref, dyr_ref), (1, ys_ref, dys_ref)):
            @pl.when(pl.program_id(1) == g)
            def _():
                dy_ref[...] = (d * sg).astype(BF16)
                dg_ref[...] = (d * y_ref[...] * sg * (1.0 - sg)).astype(BF16)

    tm = 2 * TM
    row = pl.BlockSpec((tm, D), lambda i, g: (i, 0))
    o1 = jax.ShapeDtypeStruct((lq, D), BF16)
    return pl.pallas_call(
        body, name="merge_bwd", grid=(lq // tm, 2),
        in_specs=[pl.BlockSpec((tm, D), lambda i, g: (i, C_GA // D + g)), row, row, row],
        out_specs=(row, row, pl.BlockSpec((tm, D), lambda i, g: (i, C_GA // D + g))),
        out_shape=(o1, o1, _d_proj_shape(lq)),
        compiler_params=_params("arbitrary", "arbitrary"),
    )(proj, y_ret, y_sb, d_merged)


def post_mix_fwd(hp, mix, g_post, g_pre):
    lq = hp.shape[0]

    def body(h_ref, m_ref, g2_ref, g3_ref, h1_ref, hn_ref):
        mhat, _ = _rms_hat(m_ref[...])
        h1 = h_ref[...] + mhat * g2_ref[...]
        h1_ref[...] = h1
        hhat, _ = _rms_hat(h1)
        hn_ref[...] = (hhat * g3_ref[...]).astype(BF16)

    return _rows_call(body, "post_mix_fwd", lq, (hp, mix, g_post, g_pre),
                      [_rb(hp), _rb(mix), _whole(g_post), _whole(g_pre)],
                      (jax.ShapeDtypeStruct((lq, D), F32), jax.ShapeDtypeStruct((lq, D), BF16)),
                      (_rb(hp), _rb(hp)))


def ffn_in_swiglu(x, w_t):
    m, k = x.shape
    tm = _tile(m, (768, 512, 256))
    tn = _tile(DFF, WIDE_TILES)
    nj = DFF // tn

    def body(x_ref, wa_ref, wb_ref, a_ref, b_ref, act_ref):
        xv = x_ref[...]
        a = _dot_nt(xv, wa_ref[...])
        b = _dot_nt(xv, wb_ref[...])
        a_ref[...] = a.astype(BF16)
        b_ref[...] = b.astype(BF16)
        act_ref[...] = (a * _sigmoid(a) * b).astype(BF16)

    out = jax.ShapeDtypeStruct((m, DFF), BF16)
    blk = pl.BlockSpec((tm, tn), lambda i, j: (i, j))
    return pl.pallas_call(
        body, name="ffn_in_swiglu", grid=(m // tm, nj),
        in_specs=[pl.BlockSpec((tm, k), lambda i, j: (i, 0)), pl.BlockSpec((tn, k), lambda i, j: (j, 0)),
                  pl.BlockSpec((tn, k), lambda i, j: (nj + j, 0))],
        out_specs=(blk, blk, blk), out_shape=(out, out, out),
        compiler_params=_params("parallel", "arbitrary"),
    )(x, w_t, w_t)


def d_act_swiglu(d_ff, w_out, a, b):
    m, k = d_ff.shape
    tm = _tile(m, (768, 512, 256))
    tn = _tile(DFF, WIDE_TILES)

    def body(d_ref, w_ref, a_ref, b_ref, da_ref, db_ref):
        d = _dot_nt(d_ref[...], w_ref[...])
        av = a_ref[...].astype(F32)
        sg = _sigmoid(av)
        da_ref[...] = (d * b_ref[...].astype(F32) * sg * (1.0 + av * (1.0 - sg))).astype(BF16)
        db_ref[...] = (d * av * sg).astype(BF16)

    out = jax.ShapeDtypeStruct((m, DFF), BF16)
    blk = pl.BlockSpec((tm, tn), lambda i, j: (i, j))
    return pl.pallas_call(
        body, name="d_act_swiglu", grid=(m // tm, DFF // tn),
        in_specs=[pl.BlockSpec((tm, k), lambda i, j: (i, 0)), pl.BlockSpec((tn, k), lambda i, j: (j, 0)), blk, blk],
        out_specs=(blk, blk), out_shape=(out, out),
        compiler_params=_params("parallel", "arbitrary"),
    )(d_ff, w_out, a, b)


def mm_nn_halves(a0, a1, b, out_dtype, name):
    m, kh = a0.shape
    n = b.shape[1]
    tm = _tile(m, (768, 512, 256))

    def body(a0_ref, a1_ref, b_ref, o_ref, acc_ref):
        @pl.when(pl.program_id(1) == 0)
        def _():
            acc_ref[...] = _dot(a0_ref[...], b_ref[...])

        @pl.when(pl.program_id(1) == 1)
        def _():
            o_ref[...] = (acc_ref[...] + _dot(a1_ref[...], b_ref[...])).astype(o_ref.dtype)

    half = pl.BlockSpec((tm, kh), lambda i, kk: (i, 0))
    return pl.pallas_call(
        body, name=name, grid=(m // tm, 2),
        in_specs=[half, half, pl.BlockSpec((kh, n), lambda i, kk: (kk, 0))],
        out_specs=pl.BlockSpec((tm, n), lambda i, kk: (i, 0)),
        out_shape=jax.ShapeDtypeStruct((m, n), out_dtype),
        scratch_shapes=[pltpu.VMEM((tm, n), F32)],
        compiler_params=_params("parallel", "arbitrary"),
    )(a0, a1, b)


def loss_head(h1, ff, g_post, target):
    lq = h1.shape[0]
    front_blocks = 1
    rb = lambda a: _rb(a, tm=FRONT)

    def body(h_ref, f_ref, g_ref, t_ref, loss_ref, dh_ref, df_ref, dg_ref):
        i = pl.program_id(0)

        @pl.when(i == 0)
        def _():
            loss_ref[...] = jnp.zeros_like(loss_ref)
            dg_ref[...] = jnp.zeros_like(dg_ref)

        g = g_ref[...]
        fhat, r = _rms_hat(f_ref[...])
        is_x = (i >= front_blocks).astype(F32)
        diff = (h_ref[...] + fhat * g - t_ref[...]) * is_x
        loss_ref[...] += 0.5 * jnp.sum(diff * diff) / D
        dy = diff / D
        dh_ref[...] = dy
        df_ref[...] = _rms_bwd(fhat, r, g, dy).astype(BF16)
        dg_ref[...] += jnp.sum(dy * fhat, axis=0, keepdims=True)

    return _rows_call(
        body, "loss_head", lq, (h1, ff, g_post, target),
        [rb(h1), rb(ff), _whole(g_post),
         pl.BlockSpec((FRONT, D), lambda i: (jnp.maximum(i - front_blocks, 0), 0))],
        (jax.ShapeDtypeStruct((8, 128), F32), jax.ShapeDtypeStruct((lq, D), F32),
         jax.ShapeDtypeStruct((lq, D), BF16), jax.ShapeDtypeStruct((GAIN_ROWS, D), F32)),
        (pl.BlockSpec((8, 128), lambda i: (0, 0)), rb(h1), rb(h1), pl.BlockSpec((GAIN_ROWS, D), lambda i: (0, 0))),
        tm=FRONT)


def post_mix_bwd(h1, d_hn2, g_pre, d_h2, mix, g_post):
    lq = h1.shape[0]

    def body(h_ref, dn_ref, g3_ref, dh2_ref, m_ref, g2_ref, dh1_ref, dm_ref, dg3_ref, dg2_ref):
        i = pl.program_id(0)

        @pl.when(i == 0)
        def _():
            dg3_ref[...] = jnp.zeros_like(dg3_ref)
            dg2_ref[...] = jnp.zeros_like(dg2_ref)

        hhat, r = _rms_hat(h_ref[...])
        dn = dn_ref[...]
        d_h1 = dh2_ref[...] + _rms_bwd(hhat, r, g3_ref[...], dn)
        dh1_ref[...] = d_h1
        dg3_ref[...] += jnp.sum(dn * hhat, axis=0, keepdims=True)
        mhat, rm = _rms_hat(m_ref[...])
        dm_ref[...] = _rms_bwd(mhat, rm, g2_ref[...], d_h1).astype(BF16)
        dg2_ref[...] += jnp.sum(d_h1 * mhat, axis=0, keepdims=True)

    vec = jax.ShapeDtypeStruct((GAIN_ROWS, D), F32)
    vspec = pl.BlockSpec((GAIN_ROWS, D), lambda i: (0, 0))
    return _rows_call(body, "post_mix_bwd", lq, (h1, d_hn2, g_pre, d_h2, mix, g_post),
                      [_rb(h1), _rb(d_hn2), _whole(g_pre), _rb(d_h2), _rb(mix), _whole(g_post)],
                      (jax.ShapeDtypeStruct((lq, D), F32), jax.ShapeDtypeStruct((lq, D), BF16), vec, vec),
                      (_rb(h1), _rb(h1), vspec, vspec))


def pre_mix_bwd(hp, d_hn1, g_pre, d_h1):
    lq = hp.shape[0]

    def body(h_ref, dn_ref, g_ref, dh1_ref, dhp_ref, dg_ref):
        i = pl.program_id(0)

        @pl.when(i == 0)
        def _():
            dg_ref[...] = jnp.zeros_like(dg_ref)

        hhat, r = _rms_hat(h_ref[...])
        dn = dn_ref[...]
        dhp_ref[...] = dh1_ref[...] + _rms_bwd(hhat, r, g_ref[...], dn)
        dg_ref[...] += jnp.sum(dn * hhat, axis=0, keepdims=True)

    return _rows_call(body, "pre_mix_bwd", lq, (hp, d_hn1, g_pre, d_h1),
                      [_rb(hp), _rb(d_hn1), _whole(g_pre), _rb(d_h1)],
                      (jax.ShapeDtypeStruct((lq, D), F32), jax.ShapeDtypeStruct((GAIN_ROWS, D), F32)),
                      (_rb(hp), pl.BlockSpec((GAIN_ROWS, D), lambda i: (0, 0))))


def _retention_tables():
    log_g = jnp.log1p(-(2.0 ** (-5.0 - jnp.arange(RH, dtype=F32))))
    idx = jnp.arange(CHUNK, dtype=F32)
    diff = idx[:, None] - idx[None, :]
    decay = jnp.where(diff >= 0, jnp.exp(log_g[:, None, None] * jnp.maximum(diff, 0.0)), 0.0)
    zeta = jnp.exp(log_g[:, None] * (CHUNK - 1.0 - idx))
    xi = jnp.exp(log_g[:, None] * (idx + 1.0))
    g_chunk = jnp.broadcast_to(jnp.exp(log_g * CHUNK)[:, None], (RH, CHUNK))
    coef = jnp.stack([xi, zeta, g_chunk] + [jnp.zeros_like(xi)] * 125, axis=-1)
    return decay, coef


def _rotated_qk(p_ref, h, cos, sin):
    half = RDK // 2
    out = []
    for col, scale in ((C_RQ, RDK ** -0.5), (C_RK, 1.0)):
        x1 = p_ref[:, col + h * RDK: col + h * RDK + half].astype(F32)
        x2 = p_ref[:, col + h * RDK + half: col + (h + 1) * RDK].astype(F32)
        out.append(jnp.concatenate([(x1 * cos - x2 * sin) * scale, (x1 * sin + x2 * cos) * scale],
                                   axis=1).astype(BF16))
    return out


RET_COLS = C_RG + RH * RDV


def retention_fwd(proj, cos, sin, decay, coef):
    lq = proj.shape[0]
    n = lq // CHUNK

    def body(p_ref, c_ref, s_ref, dec_ref, cf_ref, o_ref, gr_ref, st_ref, state):
        @pl.when(pl.program_id(0) == 0)
        def _():
            state[...] = jnp.zeros_like(state)

        cos_, sin_ = c_ref[...], s_ref[...]
        for h in range(RH):
            vv = slice(h * RDV, (h + 1) * RDV)
            q, k = _rotated_qk(p_ref, h, cos_, sin_)
            v = p_ref[:, C_RV + h * RDV: C_RV + (h + 1) * RDV]
            xi, zeta, gch = cf_ref[h, :, 0:1], cf_ref[h, :, 1:2], cf_ref[h, 0:1, 2:3]
            st = state[h]
            stb = st.astype(BF16)
            st_ref[h] = stb
            s = _dot_nt(q, k) * dec_ref[h]
            o = _dot(s.astype(BF16), v) + _dot(q, stb) * xi
            o_ref[:, vv] = o
            yh, _ = _gn(o)
            rg = p_ref[:, C_RG + h * RDV: C_RG + (h + 1) * RDV].astype(F32)
            gr_ref[:, vv] = (rg * _sigmoid(rg) * yh).astype(BF16)
            vz = (v.astype(F32) * zeta).astype(BF16)
            state[h] = gch * st + _dot_tn(k, vz)

    v_spec = pl.BlockSpec((CHUNK, RH * RDV), lambda c: (c, 0))
    pos_spec = pl.BlockSpec((CHUNK, RDK // 2), lambda c: (c, 0))
    return pl.pallas_call(
        body, name="retention_fwd", grid=(n,),
        in_specs=[pl.BlockSpec((CHUNK, RET_COLS), lambda c: (c, 0)), pos_spec, pos_spec,
                  pl.BlockSpec((RH, CHUNK, CHUNK), lambda c: (0, 0, 0)),
                  pl.BlockSpec((RH, CHUNK, 128), lambda c: (0, 0, 0))],
        out_specs=(v_spec, v_spec, pl.BlockSpec((None, RH, RDK, RDV), lambda c: (c, 0, 0, 0))),
        out_shape=(jax.ShapeDtypeStruct((lq, RH * RDV), F32), jax.ShapeDtypeStruct((lq, RH * RDV), BF16),
                   jax.ShapeDtypeStruct((n, RH, RDK, RDV), BF16)),
        scratch_shapes=[pltpu.VMEM((RH, RDK, RDV), F32)],
        compiler_params=_params("arbitrary"),
    )(proj, cos, sin, decay, coef)


def retention_bwd(proj, cos, sin, o_ret, d_gr, states, decay, coef, d_proj):
    lq = proj.shape[0]
    n = lq // CHUNK
    half = RDK // 2

    def body(p_ref, c_ref, s_ref, y_ref, dgr_ref, st_ref, dec_ref, cf_ref, _, dp_ref, dstate):
        @pl.when(pl.program_id(0) == 0)
        def _():
            dstate[...] = jnp.zeros_like(dstate)

        cos_, sin_ = c_ref[...], s_ref[...]
        for h in range(RH):
            vv = slice(h * RDV, (h + 1) * RDV)
            q, k = _rotated_qk(p_ref, h, cos_, sin_)
            v = p_ref[:, C_RV + h * RDV: C_RV + (h + 1) * RDV]
            yh, rs = _gn(y_ref[:, vv])
            rg = p_ref[:, C_RG + h * RDV: C_RG + (h + 1) * RDV].astype(F32)
            sg = _sigmoid(rg)
            d = dgr_ref[:, vv].astype(F32)
            dp_ref[:, C_RG + h * RDV: C_RG + (h + 1) * RDV] = (d * yh * sg * (1.0 + rg * (1.0 - sg))).astype(BF16)
            dob = _gn_bwd(yh, rs, d * rg * sg).astype(BF16)
            xi, zeta, gch = cf_ref[h, :, 0:1], cf_ref[h, :, 1:2], cf_ref[h, 0:1, 2:3]
            dec = dec_ref[h]
            dsn = dstate[h]
            dsnb = dsn.astype(BF16)
            dox = (dob.astype(F32) * xi).astype(BF16)
            sb = (_dot_nt(q, k) * dec).astype(BF16)
            dsb = (_dot_nt(dob, v) * dec).astype(BF16)
            vz = (v.astype(F32) * zeta).astype(BF16)
            dq = _dot(dsb, k) + _dot_nt(dox, st_ref[h])
            dk = _dot_tn(dsb, q) + _dot_nt(vz, dsnb)
            dp_ref[:, C_RV + h * RDV: C_RV + (h + 1) * RDV] = (_dot_tn(sb, dob) + _dot(k, dsnb) * zeta).astype(BF16)
            dstate[h] = gch * dsn + _dot_tn(q, dox)
            for col, g, scale in ((C_RQ, dq, RDK ** -0.5), (C_RK, dk, 1.0)):
                d1, d2 = g[:, :half], g[:, half:]
                dp_ref[:, col + h * RDK: col + h * RDK + half] = ((d1 * cos_ + d2 * sin_) * scale).astype(BF16)
                dp_ref[:, col + h * RDK + half: col + (h + 1) * RDK] = ((d2 * cos_ - d1 * sin_) * scale).astype(BF16)

    rev = lambda c: n - 1 - c
    v_spec = pl.BlockSpec((CHUNK, RH * RDV), lambda c: (rev(c), 0))
    pos_spec = pl.BlockSpec((CHUNK, half), lambda c: (rev(c), 0))
    ret_cols = pl.BlockSpec((CHUNK, RET_COLS), lambda c: (rev(c), 0))
    return pl.pallas_call(
        body, name="retention_bwd", grid=(n,),
        in_specs=[ret_cols, pos_spec, pos_spec, v_spec, v_spec,
                  pl.BlockSpec((None, RH, RDK, RDV), lambda c: (rev(c), 0, 0, 0)),
                  pl.BlockSpec((RH, CHUNK, CHUNK), lambda c: (0, 0, 0)),
                  pl.BlockSpec((RH, CHUNK, 128), lambda c: (0, 0, 0)), ANY_SPEC],
        out_specs=ret_cols, out_shape=_d_proj_shape(lq),
        scratch_shapes=[pltpu.VMEM((RH, RDK, RDV), F32)],
        input_output_aliases={8: 0}, compiler_params=_params("arbitrary"),
    )(proj, cos, sin, o_ret, d_gr, states, decay, coef, d_proj)


def _sb_masks_and_u():
    lane = lax.broadcasted_iota(jnp.int32, (1, 2 * SB_DH), 1)
    lo = lane < SB_DH
    row = lax.broadcasted_iota(jnp.int32, (SB_T, SB_T), 0)
    col = lax.broadcasted_iota(jnp.int32, (SB_T, SB_T), 1)
    u = (row > col).astype(BF16)
    return lo, u, row, col


def _sb_rows(j):
    start = j * SB_T
    return pl.ds(start if isinstance(j, int) else pl.multiple_of(start, SB_T), SB_T)


SB_DEAD = -104.0
SB_ROWS = SB_T
SB_CHAINS = tuple((hh, slice(r, r + SB_ROWS)) for hh in range(2) for r in range(0, SB_T, SB_ROWS))


def _sb_alive(logs):
    m = logs[0]
    for l in logs[1:]:
        m = jnp.maximum(m, l)
    return (jnp.max(m) > SB_DEAD).astype(jnp.int32)


def _sb_walk(i, run, logs_of, step):
    def cond(c):
        return (c[0] >= 1) & (c[1] > 0)

    def body(c):
        r = step(c[0], c[2])
        return c[0] - 1, _sb_alive(logs_of(r)), r

    return lax.while_loop(cond, body, (i - 1, _sb_alive(logs_of(run)), run))


def _sb_logs(z):
    ls = jnp.minimum(z, 0.0) - jnp.log(1.0 + jnp.exp(-jnp.abs(z)))
    return ls, ls - z


def sb_fwd(proj, exch):
    lq = proj.shape[0]
    t = SB_T

    def body(q_ref, k_ref, v_ref, o_ref, of_ref, acc_ref):
        i = pl.program_id(1)
        lo, u, row, col = _sb_masks_and_u()
        qs = (q_ref[...].astype(F32) * SB_DH ** -0.5).astype(BF16)
        zero = jnp.zeros_like(qs)
        qh = (jnp.where(lo, qs, zero), jnp.where(lo, zero, qs))
        acc_ref[...] = jnp.zeros_like(acc_ref)

        def block(j, run, masked):
            rows = _sb_rows(j)
            ks, vs = k_ref[rows, :], v_ref[rows, :]
            if masked:
                valid = (col + j * t < row + i * t) & (col + j * t >= META0)
            out = []
            for hh, rs in SB_CHAINS:
                ls, ln = _sb_logs(_dot_nt(qh[hh][rs], ks))
                if masked:
                    ln = jnp.where(valid[rs], ln, 0.0)
                a = jnp.exp(ls + _dot(ln.astype(BF16), u) + run[len(out)])
                if masked:
                    a = jnp.where(valid[rs], a, 0.0)
                acc_ref[hh, rs, :] += _dot(a.astype(BF16), vs)
                out.append(run[len(out)] + jnp.sum(ln, axis=1, keepdims=True))
            return tuple(out)

        zeros = jnp.zeros((SB_ROWS, 1), F32)
        run = block(i, (zeros,) * len(SB_CHAINS), True)
        _, go, run = _sb_walk(i, run, lambda r: r, lambda j, r: block(j, r, False))

        @pl.when((i > 0) & (go > 0))
        def _():
            block(0, run, True)

        o = jnp.where(lo, acc_ref[0], acc_ref[1])
        o_ref[...] = o.astype(BF16)
        of_ref[...] = o

    blk = pl.BlockSpec((t, 128), lambda p, i: (i, p))
    return _grid_call(
        body, "sb_fwd", (D // 128, lq // t), (proj, proj, proj),
        [pl.BlockSpec((t, 128), lambda p, i: (i, C_SQ // 128 + p)),
         pl.BlockSpec((lq, 128), lambda p, i: (0, C_SK // 128 + p)),
         pl.BlockSpec((lq, 128), lambda p, i: (0, C_SV // 128 + p))],
        (jax.ShapeDtypeStruct((lq, D), BF16), jax.ShapeDtypeStruct((lq, D), F32)), (blk, blk),
        [pltpu.VMEM((2, t, 128), F32)], exch)


def sb_bwd(proj, o, d_o, d_proj, exch):
    lq = proj.shape[0]
    t = SB_T
    last = lq // t - 1

    def body(q_ref, k_ref, v_ref, o_ref, do_ref, _, dq_ref, dk_out, dv_out, acc_ref, dk_ref, dv_ref):
        i = pl.program_id(1)

        @pl.when(i == 0)
        def _():
            dk_ref[...] = jnp.zeros_like(dk_ref)
            dv_ref[...] = jnp.zeros_like(dv_ref)

        lo, u, row, col = _sb_masks_and_u()
        incl = (row <= col).astype(BF16)
        qs = (q_ref[...].astype(F32) * SB_DH ** -0.5).astype(BF16)
        do = do_ref[...]
        zero = jnp.zeros_like(qs)
        qh = (jnp.where(lo, qs, zero), jnp.where(lo, zero, qs))
        doh = (jnp.where(lo, do, zero), jnp.where(lo, zero, do))
        prod = o_ref[...] * do.astype(F32)
        dsum = (jnp.sum(jnp.where(lo, prod, 0.0), axis=1, keepdims=True),
                jnp.sum(jnp.where(lo, 0.0, prod), axis=1, keepdims=True))
        acc_ref[...] = jnp.zeros_like(acc_ref)

        def block(j, run, masked):
            rows = _sb_rows(j)
            ks, vs = k_ref[rows, :], v_ref[rows, :]
            if masked:
                valid = (col + j * t < row + i * t) & (col + j * t >= META0)
            out = []
            for hh in range(2):
                run_ln, run_e = run[2 * hh], run[2 * hh + 1]
                ls, ln = _sb_logs(_dot_nt(qh[hh], ks))
                if masked:
                    ln = jnp.where(valid, ln, 0.0)
                a = jnp.exp(ls + _dot(ln.astype(BF16), u) + run_ln)
                if masked:
                    a = jnp.where(valid, a, 0.0)
                ab = a.astype(BF16)
                e = ab.astype(F32) * _dot_nt(doh[hh], vs)
                e_sum = jnp.sum(e, axis=1, keepdims=True)
                upto = (dsum[hh] - run_e - e_sum) + _dot(e.astype(BF16), incl)
                dz = e - jnp.exp(ls) * upto
                if masked:
                    dz = jnp.where(valid, dz, 0.0)
                dzb = dz.astype(BF16)
                acc_ref[hh] += _dot(dzb, ks)
                dk_ref[rows, :] += _dot_tn(dzb, qh[hh])
                dv_ref[rows, :] += _dot_tn(ab, doh[hh])
                out += [run_ln + jnp.sum(ln, axis=1, keepdims=True), run_e + e_sum]
            return tuple(out)

        zeros = jnp.zeros((t, 1), F32)
        run = block(i, (zeros,) * 4, True)
        _, go, run = _sb_walk(i, run, lambda r: (r[0], r[2]), lambda j, r: block(j, r, False))

        @pl.when((i > 0) & (go > 0))
        def _():
            block(0, run, True)

        dq_ref[...] = (jnp.where(lo, acc_ref[0], acc_ref[1]) * SB_DH ** -0.5).astype(BF16)

        @pl.when(i == last)
        def _():
            dk_out[...] = dk_ref[...].astype(BF16)
            dv_out[...] = dv_ref[...].astype(BF16)

    blk = pl.BlockSpec((t, 128), lambda p, i: (i, p))
    q_cols = pl.BlockSpec((t, 128), lambda p, i: (i, C_SQ // 128 + p))
    col_blk = pl.BlockSpec((lq, 128), lambda p, i: (0, p))
    return _grid_call(
        body, "sb_bwd", (D // 128, lq // t), (proj, proj, proj, o, d_o, d_proj),
        [q_cols, pl.BlockSpec((lq, 128), lambda p, i: (0, C_SK // 128 + p)),
         pl.BlockSpec((lq, 128), lambda p, i: (0, C_SV // 128 + p)), blk, blk, ANY_SPEC],
        (_d_proj_shape(lq), jax.ShapeDtypeStruct((lq, D), BF16), jax.ShapeDtypeStruct((lq, D), BF16)),
        (q_cols, col_blk, col_blk),
        [pltpu.VMEM((2, t, 128), F32), pltpu.VMEM((lq, 128), F32), pltpu.VMEM((lq, 128), F32)], exch, {5: 0})


def all_gather(blocks):
    nb = len(blocks)

    def body(*refs):
        x_refs, out_refs = refs[:nb], refs[nb:2 * nb]
        send_sems, recv_sems, local_sems = refs[2 * nb:]
        x, y, c = lax.axis_index("x"), lax.axis_index("y"), lax.axis_index("c")
        me, sibling = (x, y, c), (x, y, 1 - c)
        chips = [(1 - x, y), (x, 1 - y), (1 - x, 1 - y)]

        def copy(b, k, block, to, src=None):
            slot = out_refs[b].at[_device_index(*block)]
            return pltpu.make_async_remote_copy(
                src_ref=slot if src is None else src, dst_ref=slot,
                send_sem=send_sems.at[b, k], recv_sem=recv_sems.at[b, k],
                device_id=to, device_id_type=pl.DeviceIdType.MESH)

        mine = [pltpu.make_async_copy(x_refs[b], out_refs[b].at[_device_index(*me)], local_sems.at[b])
                for b in range(nb)]
        for cp in mine:
            cp.start()
        first = []
        for b in range(nb):
            first.append(copy(b, 0, me, sibling, src=x_refs[b]))
            first += [copy(b, 1 + j, me, (*chip, c), src=x_refs[b]) for j, chip in enumerate(chips)]
        for cp in first:
            cp.start()
        passed = []
        for j, chip in enumerate(chips):
            for b in range(nb):
                copy(b, 1 + j, (*chip, c), me).wait_recv()
                cp = copy(b, 4 + j, (*chip, c), sibling)
                cp.start()
                passed.append(cp)
        for b in range(nb):
            copy(b, 0, sibling, me).wait_recv()
            for j, chip in enumerate(chips):
                copy(b, 4 + j, (*chip, 1 - c), me).wait_recv()
        for cp in first + passed:
            cp.wait_send()
        for cp in mine:
            cp.wait()

    any_spec = pl.BlockSpec(memory_space=pl.ANY)
    return pl.pallas_call(
        body, name="all_gather",
        in_specs=[any_spec] * nb, out_specs=[any_spec] * nb,
        out_shape=[jax.ShapeDtypeStruct((NDEV,) + b.shape, b.dtype) for b in blocks],
        scratch_shapes=[pltpu.SemaphoreType.DMA((nb, 7)), pltpu.SemaphoreType.DMA((nb, 7)),
                        pltpu.SemaphoreType.DMA((nb,))],
    )(*blocks)


def gather_small(small):
    def body(s_ref, land_ref, send_sems, recv_sems, local_sem):
        start, wait = _direct_exchange(lambda d: s_ref, land_ref, send_sems, recv_sems, local_sem)
        start()
        wait()

    any_spec = pl.BlockSpec(memory_space=pl.ANY)
    return pl.pallas_call(
        body, name="gather_small", in_specs=[any_spec], out_specs=any_spec,
        out_shape=jax.ShapeDtypeStruct((NDEV,) + small.shape, small.dtype),
        scratch_shapes=[pltpu.SemaphoreType.DMA((NDEV - 1,)), pltpu.SemaphoreType.DMA((NDEV - 1,)),
                        pltpu.SemaphoreType.DMA(())],
    )(small)


def sum_slots(landed, name):
    _, r, c = landed.shape
    tr = _tile(r, (352, 224, 8))

    def body(l_ref, o_ref):
        acc = l_ref[0].astype(F32)
        for p in range(1, NDEV):
            acc = acc + l_ref[p].astype(F32)
        o_ref[...] = acc

    return pl.pallas_call(
        body, name=name, grid=(r // tr,),
        in_specs=[pl.BlockSpec((NDEV, tr, c), lambda i: (0, i, 0))],
        out_specs=pl.BlockSpec((tr, c), lambda i: (i, 0)),
        out_shape=jax.ShapeDtypeStruct((r, c), F32),
        compiler_params=_params("parallel"),
    )(landed)


def adamw(w, g, m, v, name):
    r, c = w.shape
    tr = _tile(r, (256, 128))

    def body(w_ref, g_ref, m_ref, v_ref, d_ref, nm_ref, nv_ref):
        g_ = g_ref[...]
        m_ = ADAM_B1 * m_ref[...] + (1.0 - ADAM_B1) * g_
        v_ = ADAM_B2 * v_ref[...] + (1.0 - ADAM_B2) * jnp.square(g_)
        m_hat = m_ / (1.0 - ADAM_B1 ** ADAM_STEP)
        v_hat = v_ / (1.0 - ADAM_B2 ** ADAM_STEP)
        d_ref[...] = -ADAM_LR * (m_hat / (jnp.sqrt(v_hat) + ADAM_EPS) + ADAM_WD * w_ref[...])
        nm_ref[...] = m_
        nv_ref[...] = v_

    spec = pl.BlockSpec((tr, c), lambda i: (i, 0))
    out = jax.ShapeDtypeStruct((r, c), F32)
    return pl.pallas_call(
        body, name=name, grid=(r // tr,), in_specs=[spec] * 4, out_specs=(spec,) * 3, out_shape=(out,) * 3,
        compiler_params=_params("parallel"),
    )(w, g, m, v)


def kernel(x, meta_tokens, w_in, w_ret_out, w_sb_out, w_out, w_ffn_in, w_ffn_out, norm_mix_pre, norm_mix_post, norm_ffn_pre, norm_ffn_post, loss_target, m_meta_tokens, m_w_in, m_w_ret_out, m_w_sb_out, m_w_out, m_w_ffn_in, m_w_ffn_out, m_norm_mix_pre, m_norm_mix_post, m_norm_ffn_pre, m_norm_ffn_post, v_meta_tokens, v_w_in, v_w_ret_out, v_w_sb_out, v_w_out, v_w_ffn_in, v_w_ffn_out, v_norm_mix_pre, v_norm_mix_post, v_norm_ffn_pre, v_norm_ffn_post):
    seq = x.shape[1]
    lq = seq + FRONT
    me = _device_index(lax.axis_index("x"), lax.axis_index("y"), lax.axis_index("c"))

    shards = {"w_in_t": w_in[0].T, "w_ffn_in_t": w_ffn_in[0].T, "w_ret_out": w_ret_out[0],
              "w_sb_out": w_sb_out[0], "w_out": w_out[0], "w_ffn_out": w_ffn_out[0]}
    rest_rows = PACK_ROWS[1:]
    pack_rest = jnp.concatenate([shards[n].astype(BF16) for n, _ in rest_rows], axis=0)
    gathered_in, meta_all = all_gather([shards["w_in_t"].astype(BF16), meta_tokens])
    full = {"w_in_t": gathered_in.reshape(PROJ, D)}
    meta_full = meta_all.transpose(1, 0, 2).reshape(N_META, D)

    pos = jnp.arange(lq, dtype=F32) - META0
    half = RDK // 2
    ang = pos[:, None] * (ROPE_BASE ** (-jnp.arange(half, dtype=F32) / half))[None, :]
    cos, sin = jnp.cos(ang), jnp.sin(ang)
    decay, coef = _retention_tables()

    hp = jnp.concatenate([jnp.zeros((META0, D), F32), meta_full, x[0]], axis=0)
    hn1 = rms_fwd(hp, norm_mix_pre, "rms_mix_pre")
    proj = mm_nt(hn1, full["w_in_t"], BF16, "proj")
    o_ret, gr, states = retention_fwd(proj, cos, sin, decay, coef)
    o_sb, o_sb_f32, gathered_rest = sb_fwd(proj, (pack_rest, "gather"))
    off = 0
    for n, r in rest_rows:
        full[n] = gathered_rest[:, off:off + r, :].reshape(NDEV * r, D)
        off += r
    y_ret = mm_nn(gr, full["w_ret_out"], F32, "y_ret")
    y_sb = mm_nn(o_sb, full["w_sb_out"], F32, "y_sb")
    merged = merge_fwd(proj, y_ret, y_sb)
    mix = mm_nn(merged, full["w_out"], F32, "mix")
    h1, hn2 = post_mix_fwd(hp, mix, norm_mix_post, norm_ffn_pre)
    ffn_a, ffn_b, act = ffn_in_swiglu(hn2, full["w_ffn_in_t"])
    ff = mm_nn(act, full["w_ffn_out"], F32, "ffn_out")
    loss_blk, d_h2, d_ff, dg_ffn_post = loss_head(h1, ff, norm_ffn_post, loss_target[0])
    loss = lax.psum(loss_blk[0, 0], ("x", "y", "c"))

    grads = {}
    d_ffn_a, d_ffn_b = d_act_swiglu(d_ff, full["w_ffn_out"], ffn_a, ffn_b)
    grads["w_ffn_out"] = mm_tn(act, d_ff, BF16, "dw_ffn_out")
    d_hn2 = mm_nn_halves(d_ffn_a, d_ffn_b, full["w_ffn_in_t"], F32, "d_hn2")
    grads["w_ffn_in_t"] = jnp.concatenate([mm_tn(d_ffn_a, hn2, BF16, "dw_ffn_in_gate"),
                                           mm_tn(d_ffn_b, hn2, BF16, "dw_ffn_in_up")], axis=0)
    d_h1, d_mix, dg_ffn_pre, dg_mix_post = post_mix_bwd(h1, d_hn2, norm_ffn_pre, d_h2, mix, norm_mix_post)
    d_merged = mm_nt(d_mix, full["w_out"], BF16, "d_merged")
    grads["w_out"] = mm_tn(merged, d_mix, BF16, "dw_out")
    d_y_ret, d_y_sb, d_proj = merge_bwd(proj, y_ret, y_sb, d_merged)
    d_gr = mm_nt(d_y_ret, full["w_ret_out"], BF16, "d_gr")
    grads["w_ret_out"] = mm_tn(gr, d_y_ret, BF16, "dw_ret_out")
    d_o_sb = mm_nt(d_y_sb, full["w_sb_out"], BF16, "d_o_sb")
    grads["w_sb_out"] = mm_tn(o_sb, d_y_sb, BF16, "dw_sb_out")
    d_proj = retention_bwd(proj, cos, sin, o_ret, d_gr, states, decay, coef, d_proj)
    parts_rest = jnp.concatenate([grads[n].reshape(NDEV, r, D) for n, r in rest_rows], axis=1)
    d_proj, d_sk, d_sv, landed_rest = sb_bwd(proj, o_sb_f32, d_o_sb, d_proj, (parts_rest, "scatter"))
    d_proj = lax.dynamic_update_slice(d_proj, d_sk, (0, C_SK))
    d_proj = lax.dynamic_update_slice(d_proj, d_sv, (0, C_SV))
    dw_in_t = mm_tn(d_proj, hn1, BF16, "dw_in")
    d_hn1, landed_in = mm_nn(d_proj, full["w_in_t"], F32, "d_hn1",
                             exch=(dw_in_t.reshape(NDEV, PROJ // NDEV, D), "scatter"))
    d_hp, dg_mix_pre = pre_mix_bwd(hp, d_hn1, norm_mix_pre, d_h1)
    grad_x = d_hp[FRONT:][None]
    small = jnp.concatenate([dg_mix_pre, dg_mix_post, dg_ffn_pre, dg_ffn_post, d_hp[META0:FRONT]], axis=0)
    ssum = sum_slots(gather_small(small), "sum_small")
    g = {"w_in_t": sum_slots(landed_in, "sum_grads_in")}
    gsum = sum_slots(landed_rest, "sum_grads_rest")
    off = 0
    for n, r in rest_rows:
        g[n] = gsum[off:off + r]
        off += r
    gain = lambda k: ssum[k * GAIN_ROWS:k * GAIN_ROWS + 1]
    g_w = {"meta_tokens": lax.dynamic_slice(ssum[4 * GAIN_ROWS:], (0, me * (D // NDEV)), (N_META, D // NDEV)),
           "w_in": g["w_in_t"].T[None], "w_ret_out": g["w_ret_out"][None], "w_sb_out": g["w_sb_out"][None],
           "w_out": g["w_out"][None], "w_ffn_in": g["w_ffn_in_t"].T[None], "w_ffn_out": g["w_ffn_out"][None],
           "norm_mix_pre": gain(0), "norm_mix_post": gain(1), "norm_ffn_pre": gain(2), "norm_ffn_post": gain(3)}

    names = ["meta_tokens", "w_in", "w_ret_out", "w_sb_out", "w_out", "w_ffn_in", "w_ffn_out",
             "norm_mix_pre", "norm_mix_post", "norm_ffn_pre", "norm_ffn_post"]
    w_of = dict(zip(names, (meta_tokens, w_in, w_ret_out, w_sb_out, w_out, w_ffn_in, w_ffn_out,
                            norm_mix_pre, norm_mix_post, norm_ffn_pre, norm_ffn_post)))
    m_of = dict(zip(names, (m_meta_tokens, m_w_in, m_w_ret_out, m_w_sb_out, m_w_out, m_w_ffn_in, m_w_ffn_out,
                            m_norm_mix_pre, m_norm_mix_post, m_norm_ffn_pre, m_norm_ffn_post)))
    v_of = dict(zip(names, (v_meta_tokens, v_w_in, v_w_ret_out, v_w_sb_out, v_w_out, v_w_ffn_in, v_w_ffn_out,
                            v_norm_mix_pre, v_norm_mix_post, v_norm_ffn_pre, v_norm_ffn_post)))
    delta, new_m, new_v = {}, {}, {}
    for n in names:
        shape = w_of[n].shape
        two_d = (shape[-2], shape[-1])
        d_, m_, v_ = adamw(w_of[n].reshape(two_d), g_w[n].reshape(two_d), m_of[n].reshape(two_d),
                           v_of[n].reshape(two_d), "adamw_" + n)
        delta[n], new_m[n], new_v[n] = d_.reshape(shape), m_.reshape(shape), v_.reshape(shape)

    return (loss, grad_x, *[g_w[n] for n in names], *[delta[n] for n in names],
            *[new_m[n] for n in names], *[new_v[n] for n in names])
```

```python
import jax
import jax.numpy as jnp
from jax import lax
from jax.experimental import pallas as pl
from jax.experimental.pallas import tpu as pltpu

F32 = jnp.float32
BF16 = jnp.bfloat16

D = 1024
N_META = 16
CHUNK = 128
FRONT = 256
META0 = FRONT - N_META
RH, RDK, RDV = 4, 256, 512
SB_DH = 64
DFF = 2816
NDEV = 8
ROPE_BASE = 10000.0
NORM_EPS = 1e-6
GN_EPS = 1e-5
C_RQ, C_RK, C_RV, C_RG, C_SQ, C_SK, C_SV, C_GA, C_GB = 0, 1024, 2048, 4096, 6144, 7168, 8192, 9216, 10240
PROJ = 11264
PACK_ROWS = (("w_in_t", PROJ // NDEV), ("w_ffn_in_t", 2 * DFF // NDEV), ("w_ret_out", RH * RDV // NDEV),
             ("w_sb_out", D // NDEV), ("w_out", D // NDEV), ("w_ffn_out", DFF // NDEV))
PACK = sum(r for _, r in PACK_ROWS)
GAIN_ROWS = 8

ADAM_LR = 0.001
ADAM_B1 = 0.9
ADAM_B2 = 0.999
ADAM_EPS = 1e-08
ADAM_WD = 0.01
ADAM_STEP = 10

VMEM_LIMIT = 56 * 1024 * 1024
SB_T = 256

NT = (((1,), (1,)), ((), ()))
TN = (((0,), (0,)), ((), ()))


def _dot(a, b):
    return jnp.dot(a, b, preferred_element_type=F32)


def _dot_nt(a, b):
    return lax.dot_general(a, b, NT, preferred_element_type=F32)


def _dot_tn(a, b):
    return lax.dot_general(a, b, TN, preferred_element_type=F32)


WIDE_TILES = (1024, 1408, 512, 256)


def _tile(n, candidates):
    for t in candidates:
        if n % t == 0:
            return t
    return n


def _params(*sem):
    return pltpu.CompilerParams(dimension_semantics=sem, vmem_limit_bytes=VMEM_LIMIT)


def _rms_hat(x):
    r = lax.rsqrt(jnp.mean(x * x, axis=-1, keepdims=True) + NORM_EPS)
    return x * r, r


def _rms_bwd(xhat, r, g, dy):
    u = dy * g
    return r * (u - xhat * jnp.mean(u * xhat, axis=-1, keepdims=True))


def _gn(y):
    mu = jnp.mean(y, axis=-1, keepdims=True)
    yc = y - mu
    rs = lax.rsqrt(jnp.mean(yc * yc, axis=-1, keepdims=True) + GN_EPS)
    return yc * rs, rs


def _gn_bwd(yh, rs, d):
    return rs * (d - jnp.mean(d, axis=-1, keepdims=True) - yh * jnp.mean(d * yh, axis=-1, keepdims=True))


def _sigmoid(x):
    return 1.0 / (1.0 + jnp.exp(-x))


def _device_index(px, py, pc):
    return 4 * px + 2 * py + pc


def _direct_exchange(src_for, land_ref, send_sems, recv_sems, local_sem):
    x, y, c = lax.axis_index("x"), lax.axis_index("y"), lax.axis_index("c")
    me = _device_index(x, y, c)
    peers = []
    for k in range(1, NDEV):
        pos = (1 - x if k & 4 else x, 1 - y if k & 2 else y, 1 - c if k & 1 else c)
        peers.append((k - 1, pos, _device_index(*pos)))

    def local():
        return pltpu.make_async_copy(src_for(me), land_ref.at[me], local_sem)

    def remote(k, pos, idx):
        return pltpu.make_async_remote_copy(
            src_ref=src_for(idx), dst_ref=land_ref.at[me], send_sem=send_sems.at[k], recv_sem=recv_sems.at[k],
            device_id=pos, device_id_type=pl.DeviceIdType.MESH)

    def arrival(k, idx):
        return pltpu.make_async_remote_copy(
            src_ref=src_for(idx), dst_ref=land_ref.at[idx], send_sem=send_sems.at[k], recv_sem=recv_sems.at[k],
            device_id=(x, y, c), device_id_type=pl.DeviceIdType.MESH)

    def start():
        local().start()
        for p in peers:
            remote(*p).start()

    def wait():
        for k, _, idx in peers:
            arrival(k, idx).wait_recv()
        for p in peers:
            remote(*p).wait_send()
        local().wait()

    return start, wait


def _grid_call(compute, name, grid, arrays, in_specs, out_shapes, out_specs, scratch, exch, aliases=None):
    arrays, in_specs = list(arrays), list(in_specs)
    out_shapes, out_specs, scratch = list(out_shapes), list(out_specs), list(scratch)
    n_in, n_out, steps = len(arrays), len(out_shapes), grid[0] * grid[1]
    aliases = aliases or {}
    if exch is None:
        return pl.pallas_call(
            compute, name=name, grid=grid, in_specs=in_specs, out_specs=out_specs, out_shape=out_shapes,
            scratch_shapes=scratch, input_output_aliases=aliases,
            compiler_params=_params("parallel", "arbitrary"))(*arrays)
    src, mode = exch

    def body(*refs):
        refs = list(refs)
        src_ref = refs.pop(n_in)
        land_ref = refs.pop(n_in + n_out)
        send_sems, recv_sems, local_sem = refs[-3:]
        src_for = (lambda d: src_ref) if mode == "gather" else (lambda d: src_ref.at[d])
        start, wait = _direct_exchange(src_for, land_ref, send_sems, recv_sems, local_sem)
        step = pl.program_id(0) * grid[1] + pl.program_id(1)
        pl.when(step == 0)(start)
        compute(*refs[:-3])
        pl.when(step == steps - 1)(wait)

    any_spec = pl.BlockSpec(memory_space=pl.ANY)
    land = src.shape if mode == "scatter" else (NDEV,) + src.shape
    return pl.pallas_call(
        body, name=name, grid=grid, in_specs=in_specs + [any_spec], out_specs=out_specs + [any_spec],
        out_shape=out_shapes + [jax.ShapeDtypeStruct(land, src.dtype)],
        scratch_shapes=scratch + [pltpu.SemaphoreType.DMA((NDEV - 1,)), pltpu.SemaphoreType.DMA((NDEV - 1,)),
                                  pltpu.SemaphoreType.DMA(())],
        input_output_aliases=aliases, compiler_params=_params("arbitrary", "arbitrary"))(*(arrays + [src]))


def _mm_call(compute, name, grid, arrays, in_specs, out_shape, out_spec, acc_shape, exch):
    if acc_shape:
        body, scratch = compute, [pltpu.VMEM(acc_shape, F32)]
    else:
        body, scratch = (lambda a_ref, b_ref, o_ref: compute(a_ref, b_ref, o_ref, None)), []
    out = _grid_call(body, name, grid, arrays, in_specs, [out_shape], [out_spec], scratch, exch)
    return tuple(out) if exch else out[0]


def mm_nt(a, b, out_dtype, name, exch=None):
    m, k = a.shape
    n = b.shape[0]
    tm = _tile(m, (1408, 768, 512, 256))
    tn = _tile(n, WIDE_TILES)

    def compute(a_ref, b_ref, o_ref, acc_ref):
        o_ref[...] = _dot_nt(a_ref[...], b_ref[...]).astype(o_ref.dtype)

    return _mm_call(
        compute, name, (m // tm, n // tn), (a, b),
        [pl.BlockSpec((tm, k), lambda i, j: (i, 0)), pl.BlockSpec((tn, k), lambda i, j: (j, 0))],
        jax.ShapeDtypeStruct((m, n), out_dtype), pl.BlockSpec((tm, tn), lambda i, j: (i, j)), None, exch)


def _accumulate(dot, steps):
    def compute(a_ref, b_ref, o_ref, acc_ref):
        kk = pl.program_id(1)

        @pl.when(kk == 0)
        def _():
            acc_ref[...] = jnp.zeros_like(acc_ref)

        acc_ref[...] += dot(a_ref[...], b_ref[...])

        @pl.when(kk == steps - 1)
        def _():
            o_ref[...] = acc_ref[...].astype(o_ref.dtype)

    return compute


def mm_nn(a, b, out_dtype, name, exch=None):
    m, k = a.shape
    n = b.shape[1]
    tm = _tile(m, (768, 512, 256))
    tk = _tile(k, (2816,) + WIDE_TILES)
    return _mm_call(
        _accumulate(_dot, k // tk), name, (m // tm, k // tk), (a, b),
        [pl.BlockSpec((tm, tk), lambda i, kk: (i, kk)), pl.BlockSpec((tk, n), lambda i, kk: (kk, 0))],
        jax.ShapeDtypeStruct((m, n), out_dtype), pl.BlockSpec((tm, n), lambda i, kk: (i, 0)), (tm, n), exch)


def mm_tn(a, b, out_dtype, name, exch=None):
    m, ka = a.shape
    n = b.shape[1]
    ta = _tile(ka, WIDE_TILES)
    tl = _tile(m, (1408, 768, 512, 256))
    return _mm_call(
        _accumulate(_dot_tn, m // tl), name, (ka // ta, m // tl), (a, b),
        [pl.BlockSpec((tl, ta), lambda i, ll: (ll, i)), pl.BlockSpec((tl, n), lambda i, ll: (ll, 0))],
        jax.ShapeDtypeStruct((ka, n), out_dtype), pl.BlockSpec((ta, n), lambda i, ll: (i, 0)), (ta, n), exch)


TM = 384


def _rb(arr, width=None, col_block=0, tm=TM):
    w = arr.shape[1] if width is None else width
    return pl.BlockSpec((tm, w), lambda i: (i, col_block))


def _whole(arr):
    return pl.BlockSpec(arr.shape, lambda i: (0,) * arr.ndim)


def _rows_call(body, name, lq, ins, in_specs, out_shapes, out_specs, aliases=None, tm=TM):
    return pl.pallas_call(
        body, name=name, grid=(lq // tm,), in_specs=in_specs, out_specs=out_specs, out_shape=out_shapes,
        input_output_aliases=aliases or {}, compiler_params=_params("arbitrary"),
    )(*ins)


ANY_SPEC = pl.BlockSpec(memory_space=pl.ANY)


def _d_proj_shape(lq):
    return jax.ShapeDtypeStruct((lq, PROJ), BF16)


def rms_fwd(h, g, name):
    lq = h.shape[0]

    def body(h_ref, g_ref, o_ref):
        xhat, _ = _rms_hat(h_ref[...])
        o_ref[...] = (xhat * g_ref[...]).astype(BF16)

    return _rows_call(body, name, lq, (h, g), [_rb(h), _whole(g)],
                      jax.ShapeDtypeStruct((lq, D), BF16), _rb(h))


def merge_fwd(proj, y_ret, y_sb):
    lq = proj.shape[0]

    def body(ga_ref, gb_ref, yr_ref, ys_ref, o_ref):
        o_ref[...] = (_sigmoid(ga_ref[...].astype(F32)) * yr_ref[...].astype(F32)
                      + _sigmoid(gb_ref[...].astype(F32)) * ys_ref[...].astype(F32)).astype(BF16)

    return _rows_call(body, "merge_fwd", lq, (proj, proj, y_ret, y_sb),
                      [_rb(proj, D, C_GA // D), _rb(proj, D, C_GB // D), _rb(y_ret), _rb(y_sb)],
                      jax.ShapeDtypeStruct((lq, D), BF16), _rb(y_ret))


def merge_bwd(proj, y_ret, y_sb, d_merged):
    lq = proj.shape[0]

    def body(gate_ref, yr_ref, ys_ref, d_ref, dyr_ref, dys_ref, dg_ref):
        d = d_ref[...].astype(F32)
        sg = _sigmoid(gate_ref[...].astype(F32))
        for g, y_ref, dy_ref in ((0, yr_ref, dyr_ref), (1, ys_ref, dys_ref)):
            @pl.when(pl.program_id(1) == g)
            def _():
                dy_ref[...] = (d * sg).astype(BF16)
                dg_ref[...] = (d * y_ref[...].astype(F32) * sg * (1.0 - sg)).astype(BF16)

    tm = 2 * TM
    row = pl.BlockSpec((tm, D), lambda i, g: (i, 0))
    o1 = jax.ShapeDtypeStruct((lq, D), BF16)
    return pl.pallas_call(
        body, name="merge_bwd", grid=(lq // tm, 2),
        in_specs=[pl.BlockSpec((tm, D), lambda i, g: (i, C_GA // D + g)), row, row, row],
        out_specs=(row, row, pl.BlockSpec((tm, D), lambda i, g: (i, C_GA // D + g))),
        out_shape=(o1, o1, _d_proj_shape(lq)),
        compiler_params=_params("arbitrary", "arbitrary"),
    )(proj, y_ret, y_sb, d_merged)


def post_mix_fwd(hp, mix, g_post, g_pre):
    lq = hp.shape[0]

    def body(h_ref, m_ref, g2_ref, g3_ref, h1_ref, hn_ref):
        mhat, _ = _rms_hat(m_ref[...].astype(F32))
        h1 = h_ref[...] + mhat * g2_ref[...]
        h1_ref[...] = h1
        hhat, _ = _rms_hat(h1)
        hn_ref[...] = (hhat * g3_ref[...]).astype(BF16)

    return _rows_call(body, "post_mix_fwd", lq, (hp, mix, g_post, g_pre),
                      [_rb(hp), _rb(mix), _whole(g_post), _whole(g_pre)],
                      (jax.ShapeDtypeStruct((lq, D), F32), jax.ShapeDtypeStruct((lq, D), BF16)),
                      (_rb(hp), _rb(hp)))


def ffn_in_swiglu(x, w_t):
    m, k = x.shape
    tm = _tile(m, (768, 512, 256))
    tn = _tile(DFF, WIDE_TILES)
    nj = DFF // tn

    def body(x_ref, wa_ref, wb_ref, a_ref, b_ref, act_ref):
        xv = x_ref[...]
        a = _dot_nt(xv, wa_ref[...])
        b = _dot_nt(xv, wb_ref[...])
        a_ref[...] = a.astype(BF16)
        b_ref[...] = b.astype(BF16)
        act_ref[...] = (a * _sigmoid(a) * b).astype(BF16)

    out = jax.ShapeDtypeStruct((m, DFF), BF16)
    blk = pl.BlockSpec((tm, tn), lambda i, j: (i, j))
    return pl.pallas_call(
        body, name="ffn_in_swiglu", grid=(m // tm, nj),
        in_specs=[pl.BlockSpec((tm, k), lambda i, j: (i, 0)), pl.BlockSpec((tn, k), lambda i, j: (j, 0)),
                  pl.BlockSpec((tn, k), lambda i, j: (nj + j, 0))],
        out_specs=(blk, blk, blk), out_shape=(out, out, out),
        compiler_params=_params("parallel", "arbitrary"),
    )(x, w_t, w_t)


def d_act_swiglu(d_ff, w_out, a, b):
    m, k = d_ff.shape
    tm = _tile(m, (768, 512, 256))
    tn = _tile(DFF, WIDE_TILES)

    def body(d_ref, w_ref, a_ref, b_ref, da_ref, db_ref):
        d = _dot_nt(d_ref[...], w_ref[...])
        av = a_ref[...].astype(F32)
        sg = _sigmoid(av)
        da_ref[...] = (d * b_ref[...].astype(F32) * sg * (1.0 + av * (1.0 - sg))).astype(BF16)
        db_ref[...] = (d * av * sg).astype(BF16)

    out = jax.ShapeDtypeStruct((m, DFF), BF16)
    blk = pl.BlockSpec((tm, tn), lambda i, j: (i, j))
    return pl.pallas_call(
        body, name="d_act_swiglu", grid=(m // tm, DFF // tn),
        in_specs=[pl.BlockSpec((tm, k), lambda i, j: (i, 0)), pl.BlockSpec((tn, k), lambda i, j: (j, 0)), blk, blk],
        out_specs=(blk, blk), out_shape=(out, out),
        compiler_params=_params("parallel", "arbitrary"),
    )(d_ff, w_out, a, b)


def mm_nn_halves(a0, a1, b, out_dtype, name):
    m, kh = a0.shape
    n = b.shape[1]
    tm = _tile(m, (768, 512, 256))

    def body(a0_ref, a1_ref, b_ref, o_ref, acc_ref):
        @pl.when(pl.program_id(1) == 0)
        def _():
            acc_ref[...] = _dot(a0_ref[...], b_ref[...])

        @pl.when(pl.program_id(1) == 1)
        def _():
            o_ref[...] = (acc_ref[...] + _dot(a1_ref[...], b_ref[...])).astype(o_ref.dtype)

    half = pl.BlockSpec((tm, kh), lambda i, kk: (i, 0))
    return pl.pallas_call(
        body, name=name, grid=(m // tm, 2),
        in_specs=[half, half, pl.BlockSpec((kh, n), lambda i, kk: (kk, 0))],
        out_specs=pl.BlockSpec((tm, n), lambda i, kk: (i, 0)),
        out_shape=jax.ShapeDtypeStruct((m, n), out_dtype),
        scratch_shapes=[pltpu.VMEM((tm, n), F32)],
        compiler_params=_params("parallel", "arbitrary"),
    )(a0, a1, b)


def loss_head(h1, ff, g_post, target):
    lq = h1.shape[0]
    front_blocks = 1
    rb = lambda a: _rb(a, tm=FRONT)

    def body(h_ref, f_ref, g_ref, t_ref, loss_ref, dh_ref, df_ref, dg_ref):
        i = pl.program_id(0)

        @pl.when(i == 0)
        def _():
            loss_ref[...] = jnp.zeros_like(loss_ref)
            dg_ref[...] = jnp.zeros_like(dg_ref)

        g = g_ref[...]
        fhat, r = _rms_hat(f_ref[...].astype(F32))
        is_x = (i >= front_blocks).astype(F32)
        diff = (h_ref[...] + fhat * g - t_ref[...]) * is_x
        loss_ref[...] += 0.5 * jnp.sum(diff * diff) / D
        dy = diff / D
        dh_ref[...] = dy
        df_ref[...] = _rms_bwd(fhat, r, g, dy).astype(BF16)
        dg_ref[...] += jnp.sum(dy * fhat, axis=0, keepdims=True)

    return _rows_call(
        body, "loss_head", lq, (h1, ff, g_post, target),
        [rb(h1), rb(ff), _whole(g_post),
         pl.BlockSpec((FRONT, D), lambda i: (jnp.maximum(i - front_blocks, 0), 0))],
        (jax.ShapeDtypeStruct((8, 128), F32), jax.ShapeDtypeStruct((lq, D), F32),
         jax.ShapeDtypeStruct((lq, D), BF16), jax.ShapeDtypeStruct((GAIN_ROWS, D), F32)),
        (pl.BlockSpec((8, 128), lambda i: (0, 0)), rb(h1), rb(h1), pl.BlockSpec((GAIN_ROWS, D), lambda i: (0, 0))),
        tm=FRONT)


def post_mix_bwd(h1, d_hn2, g_pre, d_h2, mix, g_post):
    lq = h1.shape[0]

    def body(h_ref, dn_ref, g3_ref, dh2_ref, m_ref, g2_ref, dh1_ref, dm_ref, dg3_ref, dg2_ref):
        i = pl.program_id(0)

        @pl.when(i == 0)
        def _():
            dg3_ref[...] = jnp.zeros_like(dg3_ref)
            dg2_ref[...] = jnp.zeros_like(dg2_ref)

        hhat, r = _rms_hat(h_ref[...])
        dn = dn_ref[...].astype(F32)
        d_h1 = dh2_ref[...] + _rms_bwd(hhat, r, g3_ref[...], dn)
        dh1_ref[...] = d_h1
        dg3_ref[...] += jnp.sum(dn * hhat, axis=0, keepdims=True)
        mhat, rm = _rms_hat(m_ref[...].astype(F32))
        dm_ref[...] = _rms_bwd(mhat, rm, g2_ref[...], d_h1).astype(BF16)
        dg2_ref[...] += jnp.sum(d_h1 * mhat, axis=0, keepdims=True)

    vec = jax.ShapeDtypeStruct((GAIN_ROWS, D), F32)
    vspec = pl.BlockSpec((GAIN_ROWS, D), lambda i: (0, 0))
    return _rows_call(body, "post_mix_bwd", lq, (h1, d_hn2, g_pre, d_h2, mix, g_post),
                      [_rb(h1), _rb(d_hn2), _whole(g_pre), _rb(d_h2), _rb(mix), _whole(g_post)],
                      (jax.ShapeDtypeStruct((lq, D), F32), jax.ShapeDtypeStruct((lq, D), BF16), vec, vec),
                      (_rb(h1), _rb(h1), vspec, vspec))


def pre_mix_bwd(hp, d_hn1, g_pre, d_h1):
    lq = hp.shape[0]

    def body(h_ref, dn_ref, g_ref, dh1_ref, dhp_ref, dg_ref):
        i = pl.program_id(0)

        @pl.when(i == 0)
        def _():
            dg_ref[...] = jnp.zeros_like(dg_ref)

        hhat, r = _rms_hat(h_ref[...])
        dn = dn_ref[...].astype(F32)
        dhp_ref[...] = dh1_ref[...] + _rms_bwd(hhat, r, g_ref[...], dn)
        dg_ref[...] += jnp.sum(dn * hhat, axis=0, keepdims=True)

    return _rows_call(body, "pre_mix_bwd", lq, (hp, d_hn1, g_pre, d_h1),
                      [_rb(hp), _rb(d_hn1), _whole(g_pre), _rb(d_h1)],
                      (jax.ShapeDtypeStruct((lq, D), F32), jax.ShapeDtypeStruct((GAIN_ROWS, D), F32)),
                      (_rb(hp), pl.BlockSpec((GAIN_ROWS, D), lambda i: (0, 0))))


def _retention_tables():
    log_g = jnp.log1p(-(2.0 ** (-5.0 - jnp.arange(RH, dtype=F32))))
    idx = jnp.arange(CHUNK, dtype=F32)
    diff = idx[:, None] - idx[None, :]
    decay = jnp.where(diff >= 0, jnp.exp(log_g[:, None, None] * jnp.maximum(diff, 0.0)), 0.0)
    zeta = jnp.exp(log_g[:, None] * (CHUNK - 1.0 - idx))
    xi = jnp.exp(log_g[:, None] * (idx + 1.0))
    g_chunk = jnp.broadcast_to(jnp.exp(log_g * CHUNK)[:, None], (RH, CHUNK))
    coef = jnp.stack([xi, zeta, g_chunk] + [jnp.zeros_like(xi)] * 125, axis=-1)
    return decay, coef


def _rotated_qk(p_ref, h, cos, sin):
    half = RDK // 2
    out = []
    for col, scale in ((C_RQ, RDK ** -0.5), (C_RK, 1.0)):
        x1 = p_ref[:, col + h * RDK: col + h * RDK + half].astype(F32)
        x2 = p_ref[:, col + h * RDK + half: col + (h + 1) * RDK].astype(F32)
        out.append(jnp.concatenate([(x1 * cos - x2 * sin) * scale, (x1 * sin + x2 * cos) * scale],
                                   axis=1).astype(BF16))
    return out


RET_COLS = C_RG + RH * RDV


def retention_fwd(proj, cos, sin, decay, coef):
    lq = proj.shape[0]
    n = lq // CHUNK

    def body(p_ref, c_ref, s_ref, dec_ref, cf_ref, o_ref, gr_ref, st_ref, state):
        @pl.when(pl.program_id(0) == 0)
        def _():
            state[...] = jnp.zeros_like(state)

        cos_, sin_ = c_ref[...], s_ref[...]
        for h in range(RH):
            vv = slice(h * RDV, (h + 1) * RDV)
            q, k = _rotated_qk(p_ref, h, cos_, sin_)
            v = p_ref[:, C_RV + h * RDV: C_RV + (h + 1) * RDV]
            xi, zeta, gch = cf_ref[h, :, 0:1], cf_ref[h, :, 1:2], cf_ref[h, 0:1, 2:3]
            st = state[h]
            stb = st.astype(BF16)
            st_ref[h] = stb
            s = _dot_nt(q, k) * dec_ref[h]
            o = _dot(s.astype(BF16), v) + _dot(q, stb) * xi
            o_ref[:, vv] = o
            yh, _ = _gn(o)
            rg = p_ref[:, C_RG + h * RDV: C_RG + (h + 1) * RDV].astype(F32)
            gr_ref[:, vv] = (rg * _sigmoid(rg) * yh).astype(BF16)
            vz = (v.astype(F32) * zeta).astype(BF16)
            state[h] = gch * st + _dot_tn(k, vz)

    v_spec = pl.BlockSpec((CHUNK, RH * RDV), lambda c: (c, 0))
    pos_spec = pl.BlockSpec((CHUNK, RDK // 2), lambda c: (c, 0))
    return pl.pallas_call(
        body, name="retention_fwd", grid=(n,),
        in_specs=[pl.BlockSpec((CHUNK, RET_COLS), lambda c: (c, 0)), pos_spec, pos_spec,
                  pl.BlockSpec((RH, CHUNK, CHUNK), lambda c: (0, 0, 0)),
                  pl.BlockSpec((RH, CHUNK, 128), lambda c: (0, 0, 0))],
        out_specs=(v_spec, v_spec, pl.BlockSpec((None, RH, RDK, RDV), lambda c: (c, 0, 0, 0))),
        out_shape=(jax.ShapeDtypeStruct((lq, RH * RDV), F32), jax.ShapeDtypeStruct((lq, RH * RDV), BF16),
                   jax.ShapeDtypeStruct((n, RH, RDK, RDV), BF16)),
        scratch_shapes=[pltpu.VMEM((RH, RDK, RDV), F32)],
        compiler_params=_params("arbitrary"),
    )(proj, cos, sin, decay, coef)


def retention_bwd(proj, cos, sin, o_ret, d_gr, states, decay, coef, d_proj):
    lq = proj.shape[0]
    n = lq // CHUNK
    half = RDK // 2

    def body(p_ref, c_ref, s_ref, y_ref, dgr_ref, st_ref, dec_ref, cf_ref, _, dp_ref, dstate):
        @pl.when(pl.program_id(0) == 0)
        def _():
            dstate[...] = jnp.zeros_like(dstate)

        cos_, sin_ = c_ref[...], s_ref[...]
        for h in range(RH):
            vv = slice(h * RDV, (h + 1) * RDV)
            q, k = _rotated_qk(p_ref, h, cos_, sin_)
            v = p_ref[:, C_RV + h * RDV: C_RV + (h + 1) * RDV]
            yh, rs = _gn(y_ref[:, vv])
            rg = p_ref[:, C_RG + h * RDV: C_RG + (h + 1) * RDV].astype(F32)
            sg = _sigmoid(rg)
            d = dgr_ref[:, vv].astype(F32)
            dp_ref[:, C_RG + h * RDV: C_RG + (h + 1) * RDV] = (d * yh * sg * (1.0 + rg * (1.0 - sg))).astype(BF16)
            dob = _gn_bwd(yh, rs, d * rg * sg).astype(BF16)
            xi, zeta, gch = cf_ref[h, :, 0:1], cf_ref[h, :, 1:2], cf_ref[h, 0:1, 2:3]
            dec = dec_ref[h]
            dsn = dstate[h]
            dsnb = dsn.astype(BF16)
            dox = (dob.astype(F32) * xi).astype(BF16)
            sb = (_dot_nt(q, k) * dec).astype(BF16)
            dsb = (_dot_nt(dob, v) * dec).astype(BF16)
            vz = (v.astype(F32) * zeta).astype(BF16)
            dq = _dot(dsb, k) + _dot_nt(dox, st_ref[h])
            dk = _dot_tn(dsb, q) + _dot_nt(vz, dsnb)
            dp_ref[:, C_RV + h * RDV: C_RV + (h + 1) * RDV] = (_dot_tn(sb, dob) + _dot(k, dsnb) * zeta).astype(BF16)
            dstate[h] = gch * dsn + _dot_tn(q, dox)
            for col, g, scale in ((C_RQ, dq, RDK ** -0.5), (C_RK, dk, 1.0)):
                d1, d2 = g[:, :half], g[:, half:]
                dp_ref[:, col + h * RDK: col + h * RDK + half] = ((d1 * cos_ + d2 * sin_) * scale).astype(BF16)
                dp_ref[:, col + h * RDK + half: col + (h + 1) * RDK] = ((d2 * cos_ - d1 * sin_) * scale).astype(BF16)

    rev = lambda c: n - 1 - c
    v_spec = pl.BlockSpec((CHUNK, RH * RDV), lambda c: (rev(c), 0))
    pos_spec = pl.BlockSpec((CHUNK, half), lambda c: (rev(c), 0))
    ret_cols = pl.BlockSpec((CHUNK, RET_COLS), lambda c: (rev(c), 0))
    return pl.pallas_call(
        body, name="retention_bwd", grid=(n,),
        in_specs=[ret_cols, pos_spec, pos_spec, v_spec, v_spec,
                  pl.BlockSpec((None, RH, RDK, RDV), lambda c: (rev(c), 0, 0, 0)),
                  pl.BlockSpec((RH, CHUNK, CHUNK), lambda c: (0, 0, 0)),
                  pl.BlockSpec((RH, CHUNK, 128), lambda c: (0, 0, 0)), ANY_SPEC],
        out_specs=ret_cols, out_shape=_d_proj_shape(lq),
        scratch_shapes=[pltpu.VMEM((RH, RDK, RDV), F32)],
        input_output_aliases={8: 0}, compiler_params=_params("arbitrary"),
    )(proj, cos, sin, o_ret, d_gr, states, decay, coef, d_proj)


def _sb_masks_and_u():
    lane = lax.broadcasted_iota(jnp.int32, (1, 2 * SB_DH), 1)
    lo = lane < SB_DH
    row = lax.broadcasted_iota(jnp.int32, (SB_T, SB_T), 0)
    col = lax.broadcasted_iota(jnp.int32, (SB_T, SB_T), 1)
    u = (row > col).astype(BF16)
    return lo, u, row, col


def _sb_rows(j):
    start = j * SB_T
    return pl.ds(start if isinstance(j, int) else pl.multiple_of(start, SB_T), SB_T)


SB_DEAD = -104.0
SB_ROWS = SB_T
SB_CHAINS = tuple((hh, slice(r, r + SB_ROWS)) for hh in range(2) for r in range(0, SB_T, SB_ROWS))


def _sb_alive(logs):
    m = logs[0]
    for l in logs[1:]:
        m = jnp.maximum(m, l)
    return (jnp.max(m) > SB_DEAD).astype(jnp.int32)


def _sb_walk(i, run, logs_of, step):
    def cond(c):
        return (c[0] >= 1) & (c[1] > 0)

    def body(c):
        r = step(c[0], c[2])
        return c[0] - 1, _sb_alive(logs_of(r)), r

    return lax.while_loop(cond, body, (i - 1, _sb_alive(logs_of(run)), run))


def _sb_logs(z):
    ls = jnp.minimum(z, 0.0) - jnp.log(1.0 + jnp.exp(-jnp.abs(z)))
    return ls, ls - z


def sb_fwd(proj, exch):
    lq = proj.shape[0]
    t = SB_T

    def body(q_ref, k_ref, v_ref, o_ref, of_ref, acc_ref):
        i = pl.program_id(1)
        lo, u, row, col = _sb_masks_and_u()
        qs = (q_ref[...].astype(F32) * SB_DH ** -0.5).astype(BF16)
        zero = jnp.zeros_like(qs)
        qh = (jnp.where(lo, qs, zero), jnp.where(lo, zero, qs))
        acc_ref[...] = jnp.zeros_like(acc_ref)

        def block(j, run, masked):
            rows = _sb_rows(j)
            ks, vs = k_ref[rows, :], v_ref[rows, :]
            if masked:
                valid = (col + j * t < row + i * t) & (col + j * t >= META0)
            out = []
            for hh, rs in SB_CHAINS:
                ls, ln = _sb_logs(_dot_nt(qh[hh][rs], ks))
                if masked:
                    ln = jnp.where(valid[rs], ln, 0.0)
                a = jnp.exp(ls + _dot(ln.astype(BF16), u) + run[len(out)])
                if masked:
                    a = jnp.where(valid[rs], a, 0.0)
                acc_ref[hh, rs, :] += _dot(a.astype(BF16), vs)
                out.append(run[len(out)] + jnp.sum(ln, axis=1, keepdims=True))
            return tuple(out)

        zeros = jnp.zeros((SB_ROWS, 1), F32)
        run = block(i, (zeros,) * len(SB_CHAINS), True)
        _, go, run = _sb_walk(i, run, lambda r: r, lambda j, r: block(j, r, False))

        @pl.when((i > 0) & (go > 0))
        def _():
            block(0, run, True)

        o = jnp.where(lo, acc_ref[0], acc_ref[1])
        o_ref[...] = o.astype(BF16)
        of_ref[...] = o

    blk = pl.BlockSpec((t, 128), lambda p, i: (i, p))
    return _grid_call(
        body, "sb_fwd", (D // 128, lq // t), (proj, proj, proj),
        [pl.BlockSpec((t, 128), lambda p, i: (i, C_SQ // 128 + p)),
         pl.BlockSpec((lq, 128), lambda p, i: (0, C_SK // 128 + p)),
         pl.BlockSpec((lq, 128), lambda p, i: (0, C_SV // 128 + p))],
        (jax.ShapeDtypeStruct((lq, D), BF16), jax.ShapeDtypeStruct((lq, D), F32)), (blk, blk),
        [pltpu.VMEM((2, t, 128), F32)], exch)


def sb_bwd(proj, o, d_o, d_proj, exch):
    lq = proj.shape[0]
    t = SB_T
    last = lq // t - 1

    def body(q_ref, k_ref, v_ref, o_ref, do_ref, _, dq_ref, dk_out, dv_out, acc_ref, dk_ref, dv_ref):
        i = pl.program_id(1)

        @pl.when(i == 0)
        def _():
            dk_ref[...] = jnp.zeros_like(dk_ref)
            dv_ref[...] = jnp.zeros_like(dv_ref)

        lo, u, row, col = _sb_masks_and_u()
        incl = (row <= col).astype(BF16)
        qs = (q_ref[...].astype(F32) * SB_DH ** -0.5).astype(BF16)
        do = do_ref[...]
        zero = jnp.zeros_like(qs)
        qh = (jnp.where(lo, qs, zero), jnp.where(lo, zero, qs))
        doh = (jnp.where(lo, do, zero), jnp.where(lo, zero, do))
        prod = o_ref[...] * do.astype(F32)
        dsum = (jnp.sum(jnp.where(lo, prod, 0.0), axis=1, keepdims=True),
                jnp.sum(jnp.where(lo, 0.0, prod), axis=1, keepdims=True))
        acc_ref[...] = jnp.zeros_like(acc_ref)

        def block(j, run, masked):
            rows = _sb_rows(j)
            ks, vs = k_ref[rows, :], v_ref[rows, :]
            if masked:
                valid = (col + j * t < row + i * t) & (col + j * t >= META0)
            out = []
            for hh in range(2):
                run_ln, run_e = run[2 * hh], run[2 * hh + 1]
                ls, ln = _sb_logs(_dot_nt(qh[hh], ks))
                if masked:
                    ln = jnp.where(valid, ln, 0.0)
                a = jnp.exp(ls + _dot(ln.astype(BF16), u) + run_ln)
                if masked:
                    a = jnp.where(valid, a, 0.0)
                ab = a.astype(BF16)
                e = ab.astype(F32) * _dot_nt(doh[hh], vs)
                e_sum = jnp.sum(e, axis=1, keepdims=True)
                upto = (dsum[hh] - run_e - e_sum) + _dot(e.astype(BF16), incl)
                dz = e - jnp.exp(ls) * upto
                if masked:
                    dz = jnp.where(valid, dz, 0.0)
                dzb = dz.astype(BF16)
                acc_ref[hh] += _dot(dzb, ks)
                dk_ref[rows, :] += _dot_tn(dzb, qh[hh])
                dv_ref[rows, :] += _dot_tn(ab, doh[hh])
                out += [run_ln + jnp.sum(ln, axis=1, keepdims=True), run_e + e_sum]
            return tuple(out)

        zeros = jnp.zeros((t, 1), F32)
        run = block(i, (zeros,) * 4, True)
        _, go, run = _sb_walk(i, run, lambda r: (r[0], r[2]), lambda j, r: block(j, r, False))

        @pl.when((i > 0) & (go > 0))
        def _():
            block(0, run, True)

        dq_ref[...] = (jnp.where(lo, acc_ref[0], acc_ref[1]) * SB_DH ** -0.5).astype(BF16)

        @pl.when(i == last)
        def _():
            dk_out[...] = dk_ref[...].astype(BF16)
            dv_out[...] = dv_ref[...].astype(BF16)

    blk = pl.BlockSpec((t, 128), lambda p, i: (i, p))
    q_cols = pl.BlockSpec((t, 128), lambda p, i: (i, C_SQ // 128 + p))
    col_blk = pl.BlockSpec((lq, 128), lambda p, i: (0, p))
    return _grid_call(
        body, "sb_bwd", (D // 128, lq // t), (proj, proj, proj, o, d_o, d_proj),
        [q_cols, pl.BlockSpec((lq, 128), lambda p, i: (0, C_SK // 128 + p)),
         pl.BlockSpec((lq, 128), lambda p, i: (0, C_SV // 128 + p)), blk, blk, ANY_SPEC],
        (_d_proj_shape(lq), jax.ShapeDtypeStruct((lq, D), BF16), jax.ShapeDtypeStruct((lq, D), BF16)),
        (q_cols, col_blk, col_blk),
        [pltpu.VMEM((2, t, 128), F32), pltpu.VMEM((lq, 128), F32), pltpu.VMEM((lq, 128), F32)], exch, {5: 0})


def all_gather(blocks):
    nb = len(blocks)

    def body(*refs):
        x_refs, out_refs = refs[:nb], refs[nb:2 * nb]
        send_sems, recv_sems, local_sems = refs[2 * nb:]
        x, y, c = lax.axis_index("x"), lax.axis_index("y"), lax.axis_index("c")
        me, sibling = (x, y, c), (x, y, 1 - c)
        chips = [(1 - x, y), (x, 1 - y), (1 - x, 1 - y)]

        def copy(b, k, block, to, src=None):
            slot = out_refs[b].at[_device_index(*block)]
            return pltpu.make_async_remote_copy(
                src_ref=slot if src is None else src, dst_ref=slot,
                send_sem=send_sems.at[b, k], recv_sem=recv_sems.at[b, k],
                device_id=to, device_id_type=pl.DeviceIdType.MESH)

        mine = [pltpu.make_async_copy(x_refs[b], out_refs[b].at[_device_index(*me)], local_sems.at[b])
                for b in range(nb)]
        for cp in mine:
            cp.start()
        first = []
        for b in range(nb):
            first.append(copy(b, 0, me, sibling, src=x_refs[b]))
            first += [copy(b, 1 + j, me, (*chip, c), src=x_refs[b]) for j, chip in enumerate(chips)]
        for cp in first:
            cp.start()
        passed = []
        for j, chip in enumerate(chips):
            for b in range(nb):
                copy(b, 1 + j, (*chip, c), me).wait_recv()
                cp = copy(b, 4 + j, (*chip, c), sibling)
                cp.start()
                passed.append(cp)
        for b in range(nb):
            copy(b, 0, sibling, me).wait_recv()
            for j, chip in enumerate(chips):
                copy(b, 4 + j, (*chip, 1 - c), me).wait_recv()
        for cp in first + passed:
            cp.wait_send()
        for cp in mine:
            cp.wait()

    any_spec = pl.BlockSpec(memory_space=pl.ANY)
    return pl.pallas_call(
        body, name="all_gather",
        in_specs=[any_spec] * nb, out_specs=[any_spec] * nb,
        out_shape=[jax.ShapeDtypeStruct((NDEV,) + b.shape, b.dtype) for b in blocks],
        scratch_shapes=[pltpu.SemaphoreType.DMA((nb, 7)), pltpu.SemaphoreType.DMA((nb, 7)),
                        pltpu.SemaphoreType.DMA((nb,))],
    )(*blocks)


def gather_small(small):
    def body(s_ref, land_ref, send_sems, recv_sems, local_sem):
        start, wait = _direct_exchange(lambda d: s_ref, land_ref, send_sems, recv_sems, local_sem)
        start()
        wait()

    any_spec = pl.BlockSpec(memory_space=pl.ANY)
    return pl.pallas_call(
        body, name="gather_small", in_specs=[any_spec], out_specs=any_spec,
        out_shape=jax.ShapeDtypeStruct((NDEV,) + small.shape, small.dtype),
        scratch_shapes=[pltpu.SemaphoreType.DMA((NDEV - 1,)), pltpu.SemaphoreType.DMA((NDEV - 1,)),
                        pltpu.SemaphoreType.DMA(())],
    )(small)


def sum_slots(landed, name):
    _, r, c = landed.shape
    tr = _tile(r, (352, 224, 8))

    def body(l_ref, o_ref):
        acc = l_ref[0].astype(F32)
        for p in range(1, NDEV):
            acc = acc + l_ref[p].astype(F32)
        o_ref[...] = acc

    return pl.pallas_call(
        body, name=name, grid=(r // tr,),
        in_specs=[pl.BlockSpec((NDEV, tr, c), lambda i: (0, i, 0))],
        out_specs=pl.BlockSpec((tr, c), lambda i: (i, 0)),
        out_shape=jax.ShapeDtypeStruct((r, c), F32),
        compiler_params=_params("parallel"),
    )(landed)


def adamw(w, g, m, v, name):
    r, c = w.shape
    tr = _tile(r, (256, 128))

    def body(w_ref, g_ref, m_ref, v_ref, d_ref, nm_ref, nv_ref):
        g_ = g_ref[...]
        m_ = ADAM_B1 * m_ref[...] + (1.0 - ADAM_B1) * g_
        v_ = ADAM_B2 * v_ref[...] + (1.0 - ADAM_B2) * jnp.square(g_)
        m_hat = m_ / (1.0 - ADAM_B1 ** ADAM_STEP)
        v_hat = v_ / (1.0 - ADAM_B2 ** ADAM_STEP)
        d_ref[...] = -ADAM_LR * (m_hat / (jnp.sqrt(v_hat) + ADAM_EPS) + ADAM_WD * w_ref[...])
        nm_ref[...] = m_
        nv_ref[...] = v_

    spec = pl.BlockSpec((tr, c), lambda i: (i, 0))
    out = jax.ShapeDtypeStruct((r, c), F32)
    return pl.pallas_call(
        body, name=name, grid=(r // tr,), in_specs=[spec] * 4, out_specs=(spec,) * 3, out_shape=(out,) * 3,
        compiler_params=_params("parallel"),
    )(w, g, m, v)


def kernel(x, meta_tokens, w_in, w_ret_out, w_sb_out, w_out, w_ffn_in, w_ffn_out, norm_mix_pre, norm_mix_post, norm_ffn_pre, norm_ffn_post, loss_target, m_meta_tokens, m_w_in, m_w_ret_out, m_w_sb_out, m_w_out, m_w_ffn_in, m_w_ffn_out, m_norm_mix_pre, m_norm_mix_post, m_norm_ffn_pre, m_norm_ffn_post, v_meta_tokens, v_w_in, v_w_ret_out, v_w_sb_out, v_w_out, v_w_ffn_in, v_w_ffn_out, v_norm_mix_pre, v_norm_mix_post, v_norm_ffn_pre, v_norm_ffn_post):
    seq = x.shape[1]
    lq = seq + FRONT
    me = _device_index(lax.axis_index("x"), lax.axis_index("y"), lax.axis_index("c"))

    shards = {"w_in_t": w_in[0].T, "w_ffn_in_t": w_ffn_in[0].T, "w_ret_out": w_ret_out[0],
              "w_sb_out": w_sb_out[0], "w_out": w_out[0], "w_ffn_out": w_ffn_out[0]}
    rest_rows = PACK_ROWS[1:]
    pack_rest = jnp.concatenate([shards[n].astype(BF16) for n, _ in rest_rows], axis=0)
    gathered_in, meta_all = all_gather([shards["w_in_t"].astype(BF16), meta_tokens])
    full = {"w_in_t": gathered_in.reshape(PROJ, D)}
    meta_full = meta_all.transpose(1, 0, 2).reshape(N_META, D)

    pos = jnp.arange(lq, dtype=F32) - META0
    half = RDK // 2
    ang = pos[:, None] * (ROPE_BASE ** (-jnp.arange(half, dtype=F32) / half))[None, :]
    cos, sin = jnp.cos(ang), jnp.sin(ang)
    decay, coef = _retention_tables()

    hp = jnp.concatenate([jnp.zeros((META0, D), F32), meta_full, x[0]], axis=0)
    hn1 = rms_fwd(hp, norm_mix_pre, "rms_mix_pre")
    proj = mm_nt(hn1, full["w_in_t"], BF16, "proj")
    o_ret, gr, states = retention_fwd(proj, cos, sin, decay, coef)
    o_sb, o_sb_f32, gathered_rest = sb_fwd(proj, (pack_rest, "gather"))
    off = 0
    for n, r in rest_rows:
        full[n] = gathered_rest[:, off:off + r, :].reshape(NDEV * r, D)
        off += r
    y_ret = mm_nn(gr, full["w_ret_out"], BF16, "y_ret")
    y_sb = mm_nn(o_sb, full["w_sb_out"], BF16, "y_sb")
    merged = merge_fwd(proj, y_ret, y_sb)
    mix = mm_nn(merged, full["w_out"], BF16, "mix")
    h1, hn2 = post_mix_fwd(hp, mix, norm_mix_post, norm_ffn_pre)
    ffn_a, ffn_b, act = ffn_in_swiglu(hn2, full["w_ffn_in_t"])
    ff = mm_nn(act, full["w_ffn_out"], BF16, "ffn_out")
    loss_blk, d_h2, d_ff, dg_ffn_post = loss_head(h1, ff, norm_ffn_post, loss_target[0])
    loss = lax.psum(loss_blk[0, 0], ("x", "y", "c"))

    grads = {}
    d_ffn_a, d_ffn_b = d_act_swiglu(d_ff, full["w_ffn_out"], ffn_a, ffn_b)
    grads["w_ffn_out"] = mm_tn(act, d_ff, BF16, "dw_ffn_out")
    d_hn2 = mm_nn_halves(d_ffn_a, d_ffn_b, full["w_ffn_in_t"], BF16, "d_hn2")
    grads["w_ffn_in_t"] = jnp.concatenate([mm_tn(d_ffn_a, hn2, BF16, "dw_ffn_in_gate"),
                                           mm_tn(d_ffn_b, hn2, BF16, "dw_ffn_in_up")], axis=0)
    d_h1, d_mix, dg_ffn_pre, dg_mix_post = post_mix_bwd(h1, d_hn2, norm_ffn_pre, d_h2, mix, norm_mix_post)
    d_merged = mm_nt(d_mix, full["w_out"], BF16, "d_merged")
    grads["w_out"] = mm_tn(merged, d_mix, BF16, "dw_out")
    d_y_ret, d_y_sb, d_proj = merge_bwd(proj, y_ret, y_sb, d_merged)
    d_gr = mm_nt(d_y_ret, full["w_ret_out"], BF16, "d_gr")
    grads["w_ret_out"] = mm_tn(gr, d_y_ret, BF16, "dw_ret_out")
    d_o_sb = mm_nt(d_y_sb, full["w_sb_out"], BF16, "d_o_sb")
    grads["w_sb_out"] = mm_tn(o_sb, d_y_sb, BF16, "dw_sb_out")
    d_proj = retention_bwd(proj, cos, sin, o_ret, d_gr, states, decay, coef, d_proj)
    parts_rest = jnp.concatenate([grads[n].reshape(NDEV, r, D) for n, r in rest_rows], axis=1)
    d_proj, d_sk, d_sv, landed_rest = sb_bwd(proj, o_sb_f32, d_o_sb, d_proj, (parts_rest, "scatter"))
    d_proj = lax.dynamic_update_slice(d_proj, d_sk, (0, C_SK))
    d_proj = lax.dynamic_update_slice(d_proj, d_sv, (0, C_SV))
    dw_in_t = mm_tn(d_proj, hn1, BF16, "dw_in")
    d_hn1, landed_in = mm_nn(d_proj, full["w_in_t"], BF16, "d_hn1",
                             exch=(dw_in_t.reshape(NDEV, PROJ // NDEV, D), "scatter"))
    d_hp, dg_mix_pre = pre_mix_bwd(hp, d_hn1, norm_mix_pre, d_h1)
    grad_x = d_hp[FRONT:][None]
    small = jnp.concatenate([dg_mix_pre, dg_mix_post, dg_ffn_pre, dg_ffn_post, d_hp[META0:FRONT]], axis=0)
    ssum = sum_slots(gather_small(small), "sum_small")
    g = {"w_in_t": sum_slots(landed_in, "sum_grads_in")}
    gsum = sum_slots(landed_rest, "sum_grads_rest")
    off = 0
    for n, r in rest_rows:
        g[n] = gsum[off:off + r]
        off += r
    gain = lambda k: ssum[k * GAIN_ROWS:k * GAIN_ROWS + 1]
    g_w = {"meta_tokens": lax.dynamic_slice(ssum[4 * GAIN_ROWS:], (0, me * (D // NDEV)), (N_META, D // NDEV)),
           "w_in": g["w_in_t"].T[None], "w_ret_out": g["w_ret_out"][None], "w_sb_out": g["w_sb_out"][None],
           "w_out": g["w_out"][None], "w_ffn_in": g["w_ffn_in_t"].T[None], "w_ffn_out": g["w_ffn_out"][None],
           "norm_mix_pre": gain(0), "norm_mix_post": gain(1), "norm_ffn_pre": gain(2), "norm_ffn_post": gain(3)}

    names = ["meta_tokens", "w_in", "w_ret_out", "w_sb_out", "w_out", "w_ffn_in", "w_ffn_out",
             "norm_mix_pre", "norm_mix_post", "norm_ffn_pre", "norm_ffn_post"]
    w_of = dict(zip(names, (meta_tokens, w_in, w_ret_out, w_sb_out, w_out, w_ffn_in, w_ffn_out,
                            norm_mix_pre, norm_mix_post, norm_ffn_pre, norm_ffn_post)))
    m_of = dict(zip(names, (m_meta_tokens, m_w_in, m_w_ret_out, m_w_sb_out, m_w_out, m_w_ffn_in, m_w_ffn_out,
                            m_norm_mix_pre, m_norm_mix_post, m_norm_ffn_pre, m_norm_ffn_post)))
    v_of = dict(zip(names, (v_meta_tokens, v_w_in, v_w_ret_out, v_w_sb_out, v_w_out, v_w_ffn_in, v_w_ffn_out,
                            v_norm_mix_pre, v_norm_mix_post, v_norm_ffn_pre, v_norm_ffn_post)))
    delta, new_m, new_v = {}, {}, {}
    for n in names:
        shape = w_of[n].shape
        two_d = (shape[-2], shape[-1])
        d_, m_, v_ = adamw(w_of[n].reshape(two_d), g_w[n].reshape(two_d), m_of[n].reshape(two_d),
                           v_of[n].reshape(two_d), "adamw_" + n)
        delta[n], new_m[n], new_v[n] = d_.reshape(shape), m_.reshape(shape), v_.reshape(shape)

    return (loss, grad_x, *[g_w[n] for n in names], *[delta[n] for n in names],
            *[new_m[n] for n in names], *[new_v[n] for n in names])
```

```python
import jax
import jax.numpy as jnp
from jax import lax
from jax.experimental import pallas as pl
from jax.experimental.pallas import tpu as pltpu

F32 = jnp.float32
BF16 = jnp.bfloat16

D = 1024
N_META = 16
CHUNK = 128
FRONT = 256
META0 = FRONT - N_META
RH, RDK, RDV = 4, 256, 512
SB_DH = 64
DFF = 2816
NDEV = 8
ROPE_BASE = 10000.0
NORM_EPS = 1e-6
GN_EPS = 1e-5
C_RQ, C_RK, C_RV, C_RG, C_SQ, C_SK, C_SV, C_GA, C_GB = 0, 1024, 2048, 4096, 6144, 7168, 8192, 9216, 10240
PROJ = 11264
PACK_ROWS = (("w_in_t", PROJ // NDEV), ("w_ffn_in_t", 2 * DFF // NDEV), ("w_ret_out", RH * RDV // NDEV),
             ("w_sb_out", D // NDEV), ("w_out", D // NDEV), ("w_ffn_out", DFF // NDEV))
PACK = sum(r for _, r in PACK_ROWS)
GAIN_ROWS = 8

ADAM_LR = 0.001
ADAM_B1 = 0.9
ADAM_B2 = 0.999
ADAM_EPS = 1e-08
ADAM_WD = 0.01
ADAM_STEP = 10

VMEM_LIMIT = 56 * 1024 * 1024
SB_T = 256

NT = (((1,), (1,)), ((), ()))
TN = (((0,), (0,)), ((), ()))


def _dot(a, b):
    return jnp.dot(a, b, preferred_element_type=F32)


def _dot_nt(a, b):
    return lax.dot_general(a, b, NT, preferred_element_type=F32)


def _dot_tn(a, b):
    return lax.dot_general(a, b, TN, preferred_element_type=F32)


WIDE_TILES = (1024, 1408, 512, 256)


def _tile(n, candidates):
    for t in candidates:
        if n % t == 0:
            return t
    return n


def _params(*sem):
    return pltpu.CompilerParams(dimension_semantics=sem, vmem_limit_bytes=VMEM_LIMIT)


def _rms_hat(x):
    r = lax.rsqrt(jnp.mean(x * x, axis=-1, keepdims=True) + NORM_EPS)
    return x * r, r


def _rms_bwd(xhat, r, g, dy):
    u = dy * g
    return r * (u - xhat * jnp.mean(u * xhat, axis=-1, keepdims=True))


def _gn(y):
    mu = jnp.mean(y, axis=-1, keepdims=True)
    yc = y - mu
    rs = lax.rsqrt(jnp.mean(yc * yc, axis=-1, keepdims=True) + GN_EPS)
    return yc * rs, rs


def _gn_bwd(yh, rs, d):
    return rs * (d - jnp.mean(d, axis=-1, keepdims=True) - yh * jnp.mean(d * yh, axis=-1, keepdims=True))


def _sigmoid(x):
    return 1.0 / (1.0 + jnp.exp(-x))


def _device_index(px, py, pc):
    return 4 * px + 2 * py + pc


def _direct_exchange(src_for, land_ref, send_sems, recv_sems, local_sem):
    x, y, c = lax.axis_index("x"), lax.axis_index("y"), lax.axis_index("c")
    me = _device_index(x, y, c)
    peers = []
    for k in range(1, NDEV):
        pos = (1 - x if k & 4 else x, 1 - y if k & 2 else y, 1 - c if k & 1 else c)
        peers.append((k - 1, pos, _device_index(*pos)))

    def local():
        return pltpu.make_async_copy(src_for(me), land_ref.at[me], local_sem)

    def remote(k, pos, idx):
        return pltpu.make_async_remote_copy(
            src_ref=src_for(idx), dst_ref=land_ref.at[me], send_sem=send_sems.at[k], recv_sem=recv_sems.at[k],
            device_id=pos, device_id_type=pl.DeviceIdType.MESH)

    def arrival(k, idx):
        return pltpu.make_async_remote_copy(
            src_ref=src_for(idx), dst_ref=land_ref.at[idx], send_sem=send_sems.at[k], recv_sem=recv_sems.at[k],
            device_id=(x, y, c), device_id_type=pl.DeviceIdType.MESH)

    def start():
        local().start()
        for p in peers:
            remote(*p).start()

    def wait():
        for k, _, idx in peers:
            arrival(k, idx).wait_recv()
        for p in peers:
            remote(*p).wait_send()
        local().wait()

    return start, wait


def _grid_call(compute, name, grid, arrays, in_specs, out_shapes, out_specs, scratch, exch, aliases=None):
    arrays, in_specs = list(arrays), list(in_specs)
    out_shapes, out_specs, scratch = list(out_shapes), list(out_specs), list(scratch)
    n_in, n_out, steps = len(arrays), len(out_shapes), grid[0] * grid[1]
    aliases = aliases or {}
    if exch is None:
        return pl.pallas_call(
            compute, name=name, grid=grid, in_specs=in_specs, out_specs=out_specs, out_shape=out_shapes,
            scratch_shapes=scratch, input_output_aliases=aliases,
            compiler_params=_params("parallel", "arbitrary"))(*arrays)
    src, mode = exch

    def body(*refs):
        refs = list(refs)
        src_ref = refs.pop(n_in)
        land_ref = refs.pop(n_in + n_out)
        send_sems, recv_sems, local_sem = refs[-3:]
        src_for = (lambda d: src_ref) if mode == "gather" else (lambda d: src_ref.at[d])
        start, wait = _direct_exchange(src_for, land_ref, send_sems, recv_sems, local_sem)
        step = pl.program_id(0) * grid[1] + pl.program_id(1)
        pl.when(step == 0)(start)
        compute(*refs[:-3])
        pl.when(step == steps - 1)(wait)

    any_spec = pl.BlockSpec(memory_space=pl.ANY)
    land = src.shape if mode == "scatter" else (NDEV,) + src.shape
    return pl.pallas_call(
        body, name=name, grid=grid, in_specs=in_specs + [any_spec], out_specs=out_specs + [any_spec],
        out_shape=out_shapes + [jax.ShapeDtypeStruct(land, src.dtype)],
        scratch_shapes=scratch + [pltpu.SemaphoreType.DMA((NDEV - 1,)), pltpu.SemaphoreType.DMA((NDEV - 1,)),
                                  pltpu.SemaphoreType.DMA(())],
        input_output_aliases=aliases, compiler_params=_params("arbitrary", "arbitrary"))(*(arrays + [src]))


def _mm_call(compute, name, grid, arrays, in_specs, out_shape, out_spec, acc_shape, exch):
    if acc_shape:
        body, scratch = compute, [pltpu.VMEM(acc_shape, F32)]
    else:
        body, scratch = (lambda a_ref, b_ref, o_ref: compute(a_ref, b_ref, o_ref, None)), []
    out = _grid_call(body, name, grid, arrays, in_specs, [out_shape], [out_spec], scratch, exch)
    return tuple(out) if exch else out[0]


def mm_nt(a, b, out_dtype, name, exch=None):
    m, k = a.shape
    n = b.shape[0]
    tm = _tile(m, (1408, 768, 512, 256))
    tn = _tile(n, WIDE_TILES)

    def compute(a_ref, b_ref, o_ref, acc_ref):
        o_ref[...] = _dot_nt(a_ref[...], b_ref[...]).astype(o_ref.dtype)

    return _mm_call(
        compute, name, (m // tm, n // tn), (a, b),
        [pl.BlockSpec((tm, k), lambda i, j: (i, 0)), pl.BlockSpec((tn, k), lambda i, j: (j, 0))],
        jax.ShapeDtypeStruct((m, n), out_dtype), pl.BlockSpec((tm, tn), lambda i, j: (i, j)), None, exch)


def _accumulate(dot, steps):
    def compute(a_ref, b_ref, o_ref, acc_ref):
        if steps == 1:
            o_ref[...] = dot(a_ref[...], b_ref[...]).astype(o_ref.dtype)
            return
        kk = pl.program_id(1)

        @pl.when(kk == 0)
        def _():
            acc_ref[...] = dot(a_ref[...], b_ref[...])

        @pl.when((kk > 0) & (kk < steps - 1))
        def _():
            acc_ref[...] += dot(a_ref[...], b_ref[...])

        @pl.when(kk == steps - 1)
        def _():
            o_ref[...] = (acc_ref[...] + dot(a_ref[...], b_ref[...])).astype(o_ref.dtype)

    return compute


def mm_nn(a, b, out_dtype, name, exch=None):
    m, k = a.shape
    n = b.shape[1]
    tm = _tile(m, (768, 512, 256))
    tk = _tile(k, (2816,) + WIDE_TILES)
    return _mm_call(
        _accumulate(_dot, k // tk), name, (m // tm, k // tk), (a, b),
        [pl.BlockSpec((tm, tk), lambda i, kk: (i, kk)), pl.BlockSpec((tk, n), lambda i, kk: (kk, 0))],
        jax.ShapeDtypeStruct((m, n), out_dtype), pl.BlockSpec((tm, n), lambda i, kk: (i, 0)), (tm, n), exch)


def mm_tn(a, b, out_dtype, name, exch=None):
    m, ka = a.shape
    n = b.shape[1]
    ta = _tile(ka, WIDE_TILES)
    tl = _tile(m, (1408, 768, 512, 256))
    return _mm_call(
        _accumulate(_dot_tn, m // tl), name, (ka // ta, m // tl), (a, b),
        [pl.BlockSpec((tl, ta), lambda i, ll: (ll, i)), pl.BlockSpec((tl, n), lambda i, ll: (ll, 0))],
        jax.ShapeDtypeStruct((ka, n), out_dtype), pl.BlockSpec((ta, n), lambda i, ll: (i, 0)), (ta, n), exch)


TM = 384


def _rb(arr, width=None, col_block=0, tm=TM):
    w = arr.shape[1] if width is None else width
    return pl.BlockSpec((tm, w), lambda i: (i, col_block))


def _whole(arr):
    return pl.BlockSpec(arr.shape, lambda i: (0,) * arr.ndim)


def _rows_call(body, name, lq, ins, in_specs, out_shapes, out_specs, aliases=None, tm=TM):
    return pl.pallas_call(
        body, name=name, grid=(lq // tm,), in_specs=in_specs, out_specs=out_specs, out_shape=out_shapes,
        input_output_aliases=aliases or {}, compiler_params=_params("arbitrary"),
    )(*ins)


ANY_SPEC = pl.BlockSpec(memory_space=pl.ANY)


def _d_proj_shape(lq):
    return jax.ShapeDtypeStruct((lq, PROJ), BF16)


def rms_fwd(h, g, name):
    lq = h.shape[0]

    def body(h_ref, g_ref, o_ref):
        xhat, _ = _rms_hat(h_ref[...])
        o_ref[...] = (xhat * g_ref[...]).astype(BF16)

    return _rows_call(body, name, lq, (h, g), [_rb(h), _whole(g)],
                      jax.ShapeDtypeStruct((lq, D), BF16), _rb(h))


def merge_fwd(proj, y_ret, y_sb):
    lq = proj.shape[0]

    def body(ga_ref, gb_ref, yr_ref, ys_ref, o_ref):
        o_ref[...] = (_sigmoid(ga_ref[...].astype(F32)) * yr_ref[...].astype(F32)
                      + _sigmoid(gb_ref[...].astype(F32)) * ys_ref[...].astype(F32)).astype(BF16)

    return _rows_call(body, "merge_fwd", lq, (proj, proj, y_ret, y_sb),
                      [_rb(proj, D, C_GA // D), _rb(proj, D, C_GB // D), _rb(y_ret), _rb(y_sb)],
                      jax.ShapeDtypeStruct((lq, D), BF16), _rb(y_ret))


def merge_bwd(proj, y_ret, y_sb, d_merged):
    lq = proj.shape[0]

    def body(gate_ref, yr_ref, ys_ref, d_ref, dyr_ref, dys_ref, dg_ref):
        d = d_ref[...].astype(F32)
        sg = _sigmoid(gate_ref[...].astype(F32))
        for g, y_ref, dy_ref in ((0, yr_ref, dyr_ref), (1, ys_ref, dys_ref)):
            @pl.when(pl.program_id(1) == g)
            def _():
                dy_ref[...] = (d * sg).astype(BF16)
                dg_ref[...] = (d * y_ref[...].astype(F32) * sg * (1.0 - sg)).astype(BF16)

    tm = 2 * TM
    row = pl.BlockSpec((tm, D), lambda i, g: (i, 0))
    o1 = jax.ShapeDtypeStruct((lq, D), BF16)
    return pl.pallas_call(
        body, name="merge_bwd", grid=(lq // tm, 2),
        in_specs=[pl.BlockSpec((tm, D), lambda i, g: (i, C_GA // D + g)), row, row, row],
        out_specs=(row, row, pl.BlockSpec((tm, D), lambda i, g: (i, C_GA // D + g))),
        out_shape=(o1, o1, _d_proj_shape(lq)),
        compiler_params=_params("arbitrary", "arbitrary"),
    )(proj, y_ret, y_sb, d_merged)


def post_mix_fwd(hp, mix, g_post, g_pre):
    lq = hp.shape[0]

    def body(h_ref, m_ref, g2_ref, g3_ref, h1_ref, hn_ref):
        mhat, _ = _rms_hat(m_ref[...].astype(F32))
        h1 = h_ref[...] + mhat * g2_ref[...]
        h1_ref[...] = h1
        hhat, _ = _rms_hat(h1)
        hn_ref[...] = (hhat * g3_ref[...]).astype(BF16)

    return _rows_call(body, "post_mix_fwd", lq, (hp, mix, g_post, g_pre),
                      [_rb(hp), _rb(mix), _whole(g_post), _whole(g_pre)],
                      (jax.ShapeDtypeStruct((lq, D), F32), jax.ShapeDtypeStruct((lq, D), BF16)),
                      (_rb(hp), _rb(hp)))


def ffn_in_swiglu(x, w_t):
    m, k = x.shape
    tm = _tile(m, (768, 512, 256))
    tn = _tile(DFF, WIDE_TILES)
    nj = DFF // tn

    def body(x_ref, wa_ref, wb_ref, a_ref, b_ref, act_ref):
        wa, wb = wa_ref[...], wb_ref[...]
        for r in range(0, tm, 256):
            rs = slice(r, r + 256)
            xv = x_ref[rs, :]
            a = _dot_nt(xv, wa)
            b = _dot_nt(xv, wb)
            a_ref[rs, :] = a.astype(BF16)
            b_ref[rs, :] = b.astype(BF16)
            act_ref[rs, :] = (a * _sigmoid(a) * b).astype(BF16)

    out = jax.ShapeDtypeStruct((m, DFF), BF16)
    blk = pl.BlockSpec((tm, tn), lambda i, j: (i, j))
    return pl.pallas_call(
        body, name="ffn_in_swiglu", grid=(m // tm, nj),
        in_specs=[pl.BlockSpec((tm, k), lambda i, j: (i, 0)), pl.BlockSpec((tn, k), lambda i, j: (j, 0)),
                  pl.BlockSpec((tn, k), lambda i, j: (nj + j, 0))],
        out_specs=(blk, blk, blk), out_shape=(out, out, out),
        compiler_params=_params("parallel", "arbitrary"),
    )(x, w_t, w_t)


def d_act_swiglu(d_ff, w_out, a, b):
    m, k = d_ff.shape
    tm = _tile(m, (768, 512, 256))
    tn = _tile(DFF, WIDE_TILES)

    def body(d_ref, w_ref, a_ref, b_ref, da_ref, db_ref):
        w = w_ref[...]
        for r in range(0, tm, 256):
            rs = slice(r, r + 256)
            d = _dot_nt(d_ref[rs, :], w)
            av = a_ref[rs, :].astype(F32)
            sg = _sigmoid(av)
            da_ref[rs, :] = (d * b_ref[rs, :].astype(F32) * sg * (1.0 + av * (1.0 - sg))).astype(BF16)
            db_ref[rs, :] = (d * av * sg).astype(BF16)

    out = jax.ShapeDtypeStruct((m, DFF), BF16)
    blk = pl.BlockSpec((tm, tn), lambda i, j: (i, j))
    return pl.pallas_call(
        body, name="d_act_swiglu", grid=(m // tm, DFF // tn),
        in_specs=[pl.BlockSpec((tm, k), lambda i, j: (i, 0)), pl.BlockSpec((tn, k), lambda i, j: (j, 0)), blk, blk],
        out_specs=(blk, blk), out_shape=(out, out),
        compiler_params=_params("parallel", "arbitrary"),
    )(d_ff, w_out, a, b)


def mm_nn_halves(a0, a1, b, out_dtype, name):
    m, kh = a0.shape
    n = b.shape[1]
    tm = _tile(m, (768, 512, 256))

    def body(a0_ref, a1_ref, b_ref, o_ref, acc_ref):
        @pl.when(pl.program_id(1) == 0)
        def _():
            acc_ref[...] = _dot(a0_ref[...], b_ref[...])

        @pl.when(pl.program_id(1) == 1)
        def _():
            o_ref[...] = (acc_ref[...] + _dot(a1_ref[...], b_ref[...])).astype(o_ref.dtype)

    half = pl.BlockSpec((tm, kh), lambda i, kk: (i, 0))
    return pl.pallas_call(
        body, name=name, grid=(m // tm, 2),
        in_specs=[half, half, pl.BlockSpec((kh, n), lambda i, kk: (kk, 0))],
        out_specs=pl.BlockSpec((tm, n), lambda i, kk: (i, 0)),
        out_shape=jax.ShapeDtypeStruct((m, n), out_dtype),
        scratch_shapes=[pltpu.VMEM((tm, n), F32)],
        compiler_params=_params("parallel", "arbitrary"),
    )(a0, a1, b)


def loss_head(h1, ff, g_post, target):
    lq = h1.shape[0]
    front_blocks = 1
    rb = lambda a: _rb(a, tm=FRONT)

    def body(h_ref, f_ref, g_ref, t_ref, loss_ref, dh_ref, df_ref, dg_ref):
        i = pl.program_id(0)

        @pl.when(i == 0)
        def _():
            loss_ref[...] = jnp.zeros_like(loss_ref)
            dg_ref[...] = jnp.zeros_like(dg_ref)

        g = g_ref[...]
        fhat, r = _rms_hat(f_ref[...].astype(F32))
        is_x = (i >= front_blocks).astype(F32)
        diff = (h_ref[...] + fhat * g - t_ref[...]) * is_x
        loss_ref[...] += 0.5 * jnp.sum(diff * diff) / D
        dy = diff / D
        dh_ref[...] = dy
        df_ref[...] = _rms_bwd(fhat, r, g, dy).astype(BF16)
        dg_ref[...] += jnp.sum(dy * fhat, axis=0, keepdims=True)

    return _rows_call(
        body, "loss_head", lq, (h1, ff, g_post, target),
        [rb(h1), rb(ff), _whole(g_post),
         pl.BlockSpec((FRONT, D), lambda i: (jnp.maximum(i - front_blocks, 0), 0))],
        (jax.ShapeDtypeStruct((8, 128), F32), jax.ShapeDtypeStruct((lq, D), F32),
         jax.ShapeDtypeStruct((lq, D), BF16), jax.ShapeDtypeStruct((GAIN_ROWS, D), F32)),
        (pl.BlockSpec((8, 128), lambda i: (0, 0)), rb(h1), rb(h1), pl.BlockSpec((GAIN_ROWS, D), lambda i: (0, 0))),
        tm=FRONT)


def post_mix_bwd(h1, d_hn2, g_pre, d_h2, mix, g_post):
    lq = h1.shape[0]

    def body(h_ref, dn_ref, g3_ref, dh2_ref, m_ref, g2_ref, dh1_ref, dm_ref, dg3_ref, dg2_ref):
        i = pl.program_id(0)

        @pl.when(i == 0)
        def _():
            dg3_ref[...] = jnp.zeros_like(dg3_ref)
            dg2_ref[...] = jnp.zeros_like(dg2_ref)

        hhat, r = _rms_hat(h_ref[...])
        dn = dn_ref[...].astype(F32)
        d_h1 = dh2_ref[...] + _rms_bwd(hhat, r, g3_ref[...], dn)
        dh1_ref[...] = d_h1
        dg3_ref[...] += jnp.sum(dn * hhat, axis=0, keepdims=True)
        mhat, rm = _rms_hat(m_ref[...].astype(F32))
        dm_ref[...] = _rms_bwd(mhat, rm, g2_ref[...], d_h1).astype(BF16)
        dg2_ref[...] += jnp.sum(d_h1 * mhat, axis=0, keepdims=True)

    vec = jax.ShapeDtypeStruct((GAIN_ROWS, D), F32)
    vspec = pl.BlockSpec((GAIN_ROWS, D), lambda i: (0, 0))
    return _rows_call(body, "post_mix_bwd", lq, (h1, d_hn2, g_pre, d_h2, mix, g_post),
                      [_rb(h1), _rb(d_hn2), _whole(g_pre), _rb(d_h2), _rb(mix), _whole(g_post)],
                      (jax.ShapeDtypeStruct((lq, D), F32), jax.ShapeDtypeStruct((lq, D), BF16), vec, vec),
                      (_rb(h1), _rb(h1), vspec, vspec))


def pre_mix_bwd(hp, d_hn1, g_pre, d_h1):
    lq = hp.shape[0]

    def body(h_ref, dn_ref, g_ref, dh1_ref, dhp_ref, dg_ref):
        i = pl.program_id(0)

        @pl.when(i == 0)
        def _():
            dg_ref[...] = jnp.zeros_like(dg_ref)

        hhat, r = _rms_hat(h_ref[...])
        dn = dn_ref[...].astype(F32)
        dhp_ref[...] = dh1_ref[...] + _rms_bwd(hhat, r, g_ref[...], dn)
        dg_ref[...] += jnp.sum(dn * hhat, axis=0, keepdims=True)

    return _rows_call(body, "pre_mix_bwd", lq, (hp, d_hn1, g_pre, d_h1),
                      [_rb(hp), _rb(d_hn1), _whole(g_pre), _rb(d_h1)],
                      (jax.ShapeDtypeStruct((lq, D), F32), jax.ShapeDtypeStruct((GAIN_ROWS, D), F32)),
                      (_rb(hp), pl.BlockSpec((GAIN_ROWS, D), lambda i: (0, 0))))


def _retention_tables():
    log_g = jnp.log1p(-(2.0 ** (-5.0 - jnp.arange(RH, dtype=F32))))
    idx = jnp.arange(CHUNK, dtype=F32)
    diff = idx[:, None] - idx[None, :]
    decay = jnp.where(diff >= 0, jnp.exp(log_g[:, None, None] * jnp.maximum(diff, 0.0)), 0.0)
    zeta = jnp.exp(log_g[:, None] * (CHUNK - 1.0 - idx))
    xi = jnp.exp(log_g[:, None] * (idx + 1.0))
    g_chunk = jnp.broadcast_to(jnp.exp(log_g * CHUNK)[:, None], (RH, CHUNK))
    coef = jnp.stack([xi, zeta, g_chunk] + [jnp.zeros_like(xi)] * 125, axis=-1)
    return decay, coef


def _rotated_qk(p_ref, h, cos, sin):
    half = RDK // 2
    out = []
    for col, scale in ((C_RQ, RDK ** -0.5), (C_RK, 1.0)):
        x1 = p_ref[:, col + h * RDK: col + h * RDK + half].astype(F32)
        x2 = p_ref[:, col + h * RDK + half: col + (h + 1) * RDK].astype(F32)
        out.append(jnp.concatenate([(x1 * cos - x2 * sin) * scale, (x1 * sin + x2 * cos) * scale],
                                   axis=1).astype(BF16))
    return out


RET_COLS = C_RG + RH * RDV


def retention_fwd(proj, cos, sin, decay, coef):
    lq = proj.shape[0]
    n = lq // CHUNK

    def body(p_ref, c_ref, s_ref, dec_ref, cf_ref, o_ref, gr_ref, st_ref, state):
        @pl.when(pl.program_id(0) == 0)
        def _():
            state[...] = jnp.zeros_like(state)

        cos_, sin_ = c_ref[...], s_ref[...]
        for h in range(RH):
            vv = slice(h * RDV, (h + 1) * RDV)
            q, k = _rotated_qk(p_ref, h, cos_, sin_)
            v = p_ref[:, C_RV + h * RDV: C_RV + (h + 1) * RDV]
            xi, zeta, gch = cf_ref[h, :, 0:1], cf_ref[h, :, 1:2], cf_ref[h, 0:1, 2:3]
            st = state[h]
            stb = st.astype(BF16)
            st_ref[h] = stb
            s = _dot_nt(q, k) * dec_ref[h]
            o = _dot(s.astype(BF16), v) + _dot(q, stb) * xi
            o_ref[:, vv] = o
            yh, _ = _gn(o)
            rg = p_ref[:, C_RG + h * RDV: C_RG + (h + 1) * RDV].astype(F32)
            gr_ref[:, vv] = (rg * _sigmoid(rg) * yh).astype(BF16)
            vz = (v.astype(F32) * zeta).astype(BF16)
            state[h] = gch * st + _dot_tn(k, vz)

    v_spec = pl.BlockSpec((CHUNK, RH * RDV), lambda c: (c, 0))
    pos_spec = pl.BlockSpec((CHUNK, RDK // 2), lambda c: (c, 0))
    return pl.pallas_call(
        body, name="retention_fwd", grid=(n,),
        in_specs=[pl.BlockSpec((CHUNK, RET_COLS), lambda c: (c, 0)), pos_spec, pos_spec,
                  pl.BlockSpec((RH, CHUNK, CHUNK), lambda c: (0, 0, 0)),
                  pl.BlockSpec((RH, CHUNK, 128), lambda c: (0, 0, 0))],
        out_specs=(v_spec, v_spec, pl.BlockSpec((None, RH, RDK, RDV), lambda c: (c, 0, 0, 0))),
        out_shape=(jax.ShapeDtypeStruct((lq, RH * RDV), F32), jax.ShapeDtypeStruct((lq, RH * RDV), BF16),
                   jax.ShapeDtypeStruct((n, RH, RDK, RDV), BF16)),
        scratch_shapes=[pltpu.VMEM((RH, RDK, RDV), F32)],
        compiler_params=_params("arbitrary"),
    )(proj, cos, sin, decay, coef)


def retention_bwd(proj, cos, sin, o_ret, d_gr, states, decay, coef, d_proj):
    lq = proj.shape[0]
    n = lq // CHUNK
    half = RDK // 2

    def body(p_ref, c_ref, s_ref, y_ref, dgr_ref, st_ref, dec_ref, cf_ref, _, dp_ref, dstate):
        @pl.when(pl.program_id(0) == 0)
        def _():
            dstate[...] = jnp.zeros_like(dstate)

        cos_, sin_ = c_ref[...], s_ref[...]
        for h in range(RH):
            vv = slice(h * RDV, (h + 1) * RDV)
            q, k = _rotated_qk(p_ref, h, cos_, sin_)
            v = p_ref[:, C_RV + h * RDV: C_RV + (h + 1) * RDV]
            yh, rs = _gn(y_ref[:, vv])
            rg = p_ref[:, C_RG + h * RDV: C_RG + (h + 1) * RDV].astype(F32)
            sg = _sigmoid(rg)
            d = dgr_ref[:, vv].astype(F32)
            dp_ref[:, C_RG + h * RDV: C_RG + (h + 1) * RDV] = (d * yh * sg * (1.0 + rg * (1.0 - sg))).astype(BF16)
            dob = _gn_bwd(yh, rs, d * rg * sg).astype(BF16)
            xi, zeta, gch = cf_ref[h, :, 0:1], cf_ref[h, :, 1:2], cf_ref[h, 0:1, 2:3]
            dec = dec_ref[h]
            dsn = dstate[h]
            dsnb = dsn.astype(BF16)
            dox = (dob.astype(F32) * xi).astype(BF16)
            sb = (_dot_nt(q, k) * dec).astype(BF16)
            dsb = (_dot_nt(dob, v) * dec).astype(BF16)
            vz = (v.astype(F32) * zeta).astype(BF16)
            dq = _dot(dsb, k) + _dot_nt(dox, st_ref[h])
            dk = _dot_tn(dsb, q) + _dot_nt(vz, dsnb)
            dp_ref[:, C_RV + h * RDV: C_RV + (h + 1) * RDV] = (_dot_tn(sb, dob) + _dot(k, dsnb) * zeta).astype(BF16)
            dstate[h] = gch * dsn + _dot_tn(q, dox)
            for col, g, scale in ((C_RQ, dq, RDK ** -0.5), (C_RK, dk, 1.0)):
                d1, d2 = g[:, :half], g[:, half:]
                dp_ref[:, col + h * RDK: col + h * RDK + half] = ((d1 * cos_ + d2 * sin_) * scale).astype(BF16)
                dp_ref[:, col + h * RDK + half: col + (h + 1) * RDK] = ((d2 * cos_ - d1 * sin_) * scale).astype(BF16)

    rev = lambda c: n - 1 - c
    v_spec = pl.BlockSpec((CHUNK, RH * RDV), lambda c: (rev(c), 0))
    pos_spec = pl.BlockSpec((CHUNK, half), lambda c: (rev(c), 0))
    ret_cols = pl.BlockSpec((CHUNK, RET_COLS), lambda c: (rev(c), 0))
    return pl.pallas_call(
        body, name="retention_bwd", grid=(n,),
        in_specs=[ret_cols, pos_spec, pos_spec, v_spec, v_spec,
                  pl.BlockSpec((None, RH, RDK, RDV), lambda c: (rev(c), 0, 0, 0)),
                  pl.BlockSpec((RH, CHUNK, CHUNK), lambda c: (0, 0, 0)),
                  pl.BlockSpec((RH, CHUNK, 128), lambda c: (0, 0, 0)), ANY_SPEC],
        out_specs=ret_cols, out_shape=_d_proj_shape(lq),
        scratch_shapes=[pltpu.VMEM((RH, RDK, RDV), F32)],
        input_output_aliases={8: 0}, compiler_params=_params("arbitrary"),
    )(proj, cos, sin, o_ret, d_gr, states, decay, coef, d_proj)


def _sb_masks_and_u():
    lane = lax.broadcasted_iota(jnp.int32, (1, 2 * SB_DH), 1)
    lo = lane < SB_DH
    row = lax.broadcasted_iota(jnp.int32, (SB_T, SB_T), 0)
    col = lax.broadcasted_iota(jnp.int32, (SB_T, SB_T), 1)
    u = (row > col).astype(BF16)
    return lo, u, row, col


def _sb_rows(j):
    start = j * SB_T
    return pl.ds(start if isinstance(j, int) else pl.multiple_of(start, SB_T), SB_T)


SB_DEAD = -104.0
SB_ROWS = SB_T
SB_CHAINS = tuple((hh, slice(r, r + SB_ROWS)) for hh in range(2) for r in range(0, SB_T, SB_ROWS))


def _sb_alive(logs):
    m = logs[0]
    for l in logs[1:]:
        m = jnp.maximum(m, l)
    return (jnp.max(m) > SB_DEAD).astype(jnp.int32)


def _sb_walk(i, run, logs_of, step):
    def cond(c):
        return (c[0] >= 1) & (c[1] > 0)

    def body(c):
        r = step(c[0], c[2])
        return c[0] - 1, _sb_alive(logs_of(r)), r

    return lax.while_loop(cond, body, (i - 1, _sb_alive(logs_of(run)), run))


def _sb_logs(z):
    ls = jnp.minimum(z, 0.0) - jnp.log(1.0 + jnp.exp(-jnp.abs(z)))
    return ls, ls - z


def sb_fwd(proj, exch):
    lq = proj.shape[0]
    t = SB_T

    def body(q_ref, k_ref, v_ref, o_ref, of_ref, acc_ref):
        i = pl.program_id(1)
        lo, u, row, col = _sb_masks_and_u()
        qs = (q_ref[...].astype(F32) * SB_DH ** -0.5).astype(BF16)
        zero = jnp.zeros_like(qs)
        qh = (jnp.where(lo, qs, zero), jnp.where(lo, zero, qs))
        acc_ref[...] = jnp.zeros_like(acc_ref)

        def block(j, run, masked):
            rows = _sb_rows(j)
            ks, vs = k_ref[rows, :], v_ref[rows, :]
            if masked:
                valid = (col + j * t < row + i * t) & (col + j * t >= META0)
            out = []
            for hh, rs in SB_CHAINS:
                ls, ln = _sb_logs(_dot_nt(qh[hh][rs], ks))
                if masked:
                    ln = jnp.where(valid[rs], ln, 0.0)
                a = jnp.exp(ls + _dot(ln.astype(BF16), u) + run[len(out)])
                if masked:
                    a = jnp.where(valid[rs], a, 0.0)
                acc_ref[hh, rs, :] += _dot(a.astype(BF16), vs)
                out.append(run[len(out)] + jnp.sum(ln, axis=1, keepdims=True))
            return tuple(out)

        zeros = jnp.zeros((SB_ROWS, 1), F32)
        run = block(i, (zeros,) * len(SB_CHAINS), True)
        _, go, run = _sb_walk(i, run, lambda r: r, lambda j, r: block(j, r, False))

        @pl.when((i > 0) & (go > 0))
        def _():
            block(0, run, True)

        o = jnp.where(lo, acc_ref[0], acc_ref[1])
        o_ref[...] = o.astype(BF16)
        of_ref[...] = o

    blk = pl.BlockSpec((t, 128), lambda p, i: (i, p))
    return _grid_call(
        body, "sb_fwd", (D // 128, lq // t), (proj, proj, proj),
        [pl.BlockSpec((t, 128), lambda p, i: (i, C_SQ // 128 + p)),
         pl.BlockSpec((lq, 128), lambda p, i: (0, C_SK // 128 + p)),
         pl.BlockSpec((lq, 128), lambda p, i: (0, C_SV // 128 + p))],
        (jax.ShapeDtypeStruct((lq, D), BF16), jax.ShapeDtypeStruct((lq, D), F32)), (blk, blk),
        [pltpu.VMEM((2, t, 128), F32)], exch)


def sb_bwd(proj, o, d_o, d_proj, exch):
    lq = proj.shape[0]
    t = SB_T
    last = lq // t - 1

    def body(q_ref, k_ref, v_ref, o_ref, do_ref, _, dq_ref, dk_out, dv_out, acc_ref, dk_ref, dv_ref):
        i = pl.program_id(1)

        @pl.when(i == 0)
        def _():
            dk_ref[...] = jnp.zeros_like(dk_ref)
            dv_ref[...] = jnp.zeros_like(dv_ref)

        lo, u, row, col = _sb_masks_and_u()
        incl = (row <= col).astype(BF16)
        qs = (q_ref[...].astype(F32) * SB_DH ** -0.5).astype(BF16)
        do = do_ref[...]
        zero = jnp.zeros_like(qs)
        qh = (jnp.where(lo, qs, zero), jnp.where(lo, zero, qs))
        doh = (jnp.where(lo, do, zero), jnp.where(lo, zero, do))
        prod = o_ref[...] * do.astype(F32)
        dsum = (jnp.sum(jnp.where(lo, prod, 0.0), axis=1, keepdims=True),
                jnp.sum(jnp.where(lo, 0.0, prod), axis=1, keepdims=True))
        acc_ref[...] = jnp.zeros_like(acc_ref)

        def block(j, run, masked):
            rows = _sb_rows(j)
            ks, vs = k_ref[rows, :], v_ref[rows, :]
            if masked:
                valid = (col + j * t < row + i * t) & (col + j * t >= META0)
            out = []
            for hh in range(2):
                run_ln, run_e = run[2 * hh], run[2 * hh + 1]
                ls, ln = _sb_logs(_dot_nt(qh[hh], ks))
                if masked:
                    ln = jnp.where(valid, ln, 0.0)
                a = jnp.exp(ls + _dot(ln.astype(BF16), u) + run_ln)
                if masked:
                    a = jnp.where(valid, a, 0.0)
                ab = a.astype(BF16)
                e = ab.astype(F32) * _dot_nt(doh[hh], vs)
                e_sum = jnp.sum(e, axis=1, keepdims=True)
                upto = (dsum[hh] - run_e - e_sum) + _dot(e.astype(BF16), incl)
                dz = e - jnp.exp(ls) * upto
                if masked:
                    dz = jnp.where(valid, dz, 0.0)
                dzb = dz.astype(BF16)
                acc_ref[hh] += _dot(dzb, ks)
                dk_ref[rows, :] += _dot_tn(dzb, qh[hh])
                dv_ref[rows, :] += _dot_tn(ab, doh[hh])
                out += [run_ln + jnp.sum(ln, axis=1, keepdims=True), run_e + e_sum]
            return tuple(out)

        zeros = jnp.zeros((t, 1), F32)
        run = block(i, (zeros,) * 4, True)
        _, go, run = _sb_walk(i, run, lambda r: (r[0], r[2]), lambda j, r: block(j, r, False))

        @pl.when((i > 0) & (go > 0))
        def _():
            block(0, run, True)

        dq_ref[...] = (jnp.where(lo, acc_ref[0], acc_ref[1]) * SB_DH ** -0.5).astype(BF16)

        @pl.when(i == last)
        def _():
            dk_out[...] = dk_ref[...].astype(BF16)
            dv_out[...] = dv_ref[...].astype(BF16)

    blk = pl.BlockSpec((t, 128), lambda p, i: (i, p))
    q_cols = pl.BlockSpec((t, 128), lambda p, i: (i, C_SQ // 128 + p))
    col_blk = pl.BlockSpec((lq, 128), lambda p, i: (0, p))
    return _grid_call(
        body, "sb_bwd", (D // 128, lq // t), (proj, proj, proj, o, d_o, d_proj),
        [q_cols, pl.BlockSpec((lq, 128), lambda p, i: (0, C_SK // 128 + p)),
         pl.BlockSpec((lq, 128), lambda p, i: (0, C_SV // 128 + p)), blk, blk, ANY_SPEC],
        (_d_proj_shape(lq), jax.ShapeDtypeStruct((lq, D), BF16), jax.ShapeDtypeStruct((lq, D), BF16)),
        (q_cols, col_blk, col_blk),
        [pltpu.VMEM((2, t, 128), F32), pltpu.VMEM((lq, 128), F32), pltpu.VMEM((lq, 128), F32)], exch, {5: 0})


def all_gather(blocks):
    nb = len(blocks)

    def body(*refs):
        x_refs, out_refs = refs[:nb], refs[nb:2 * nb]
        send_sems, recv_sems, local_sems = refs[2 * nb:]
        x, y, c = lax.axis_index("x"), lax.axis_index("y"), lax.axis_index("c")
        me, sibling = (x, y, c), (x, y, 1 - c)
        chips = [(1 - x, y), (x, 1 - y), (1 - x, 1 - y)]

        def copy(b, k, block, to, src=None):
            slot = out_refs[b].at[_device_index(*block)]
            return pltpu.make_async_remote_copy(
                src_ref=slot if src is None else src, dst_ref=slot,
                send_sem=send_sems.at[b, k], recv_sem=recv_sems.at[b, k],
                device_id=to, device_id_type=pl.DeviceIdType.MESH)

        mine = [pltpu.make_async_copy(x_refs[b], out_refs[b].at[_device_index(*me)], local_sems.at[b])
                for b in range(nb)]
        for cp in mine:
            cp.start()
        first = []
        for b in range(nb):
            first.append(copy(b, 0, me, sibling, src=x_refs[b]))
            first += [copy(b, 1 + j, me, (*chip, c), src=x_refs[b]) for j, chip in enumerate(chips)]
        for cp in first:
            cp.start()
        passed = []
        for j, chip in enumerate(chips):
            for b in range(nb):
                copy(b, 1 + j, (*chip, c), me).wait_recv()
                cp = copy(b, 4 + j, (*chip, c), sibling)
                cp.start()
                passed.append(cp)
        for b in range(nb):
            copy(b, 0, sibling, me).wait_recv()
            for j, chip in enumerate(chips):
                copy(b, 4 + j, (*chip, 1 - c), me).wait_recv()
        for cp in first + passed:
            cp.wait_send()
        for cp in mine:
            cp.wait()

    any_spec = pl.BlockSpec(memory_space=pl.ANY)
    return pl.pallas_call(
        body, name="all_gather",
        in_specs=[any_spec] * nb, out_specs=[any_spec] * nb,
        out_shape=[jax.ShapeDtypeStruct((NDEV,) + b.shape, b.dtype) for b in blocks],
        scratch_shapes=[pltpu.SemaphoreType.DMA((nb, 7)), pltpu.SemaphoreType.DMA((nb, 7)),
                        pltpu.SemaphoreType.DMA((nb,))],
    )(*blocks)


def gather_small(small):
    def body(s_ref, land_ref, send_sems, recv_sems, local_sem):
        start, wait = _direct_exchange(lambda d: s_ref, land_ref, send_sems, recv_sems, local_sem)
        start()
        wait()

    any_spec = pl.BlockSpec(memory_space=pl.ANY)
    return pl.pallas_call(
        body, name="gather_small", in_specs=[any_spec], out_specs=any_spec,
        out_shape=jax.ShapeDtypeStruct((NDEV,) + small.shape, small.dtype),
        scratch_shapes=[pltpu.SemaphoreType.DMA((NDEV - 1,)), pltpu.SemaphoreType.DMA((NDEV - 1,)),
                        pltpu.SemaphoreType.DMA(())],
    )(small)


def sum_slots(landed, name):
    _, r, c = landed.shape
    tr = _tile(r, (352, 224, 8))

    def body(l_ref, o_ref):
        acc = l_ref[0].astype(F32)
        for p in range(1, NDEV):
            acc = acc + l_ref[p].astype(F32)
        o_ref[...] = acc

    return pl.pallas_call(
        body, name=name, grid=(r // tr,),
        in_specs=[pl.BlockSpec((NDEV, tr, c), lambda i: (0, i, 0))],
        out_specs=pl.BlockSpec((tr, c), lambda i: (i, 0)),
        out_shape=jax.ShapeDtypeStruct((r, c), F32),
        compiler_params=_params("parallel"),
    )(landed)


def adamw(w, g, m, v, name):
    r, c = w.shape
    tr = _tile(r, (256, 128))

    def body(w_ref, g_ref, m_ref, v_ref, d_ref, nm_ref, nv_ref):
        g_ = g_ref[...]
        m_ = ADAM_B1 * m_ref[...] + (1.0 - ADAM_B1) * g_
        v_ = ADAM_B2 * v_ref[...] + (1.0 - ADAM_B2) * jnp.square(g_)
        m_hat = m_ / (1.0 - ADAM_B1 ** ADAM_STEP)
        v_hat = v_ / (1.0 - ADAM_B2 ** ADAM_STEP)
        d_ref[...] = -ADAM_LR * (m_hat / (jnp.sqrt(v_hat) + ADAM_EPS) + ADAM_WD * w_ref[...])
        nm_ref[...] = m_
        nv_ref[...] = v_

    spec = pl.BlockSpec((tr, c), lambda i: (i, 0))
    out = jax.ShapeDtypeStruct((r, c), F32)
    return pl.pallas_call(
        body, name=name, grid=(r // tr,), in_specs=[spec] * 4, out_specs=(spec,) * 3, out_shape=(out,) * 3,
        compiler_params=_params("parallel"),
    )(w, g, m, v)


def kernel(x, meta_tokens, w_in, w_ret_out, w_sb_out, w_out, w_ffn_in, w_ffn_out, norm_mix_pre, norm_mix_post, norm_ffn_pre, norm_ffn_post, loss_target, m_meta_tokens, m_w_in, m_w_ret_out, m_w_sb_out, m_w_out, m_w_ffn_in, m_w_ffn_out, m_norm_mix_pre, m_norm_mix_post, m_norm_ffn_pre, m_norm_ffn_post, v_meta_tokens, v_w_in, v_w_ret_out, v_w_sb_out, v_w_out, v_w_ffn_in, v_w_ffn_out, v_norm_mix_pre, v_norm_mix_post, v_norm_ffn_pre, v_norm_ffn_post):
    seq = x.shape[1]
    lq = seq + FRONT
    me = _device_index(lax.axis_index("x"), lax.axis_index("y"), lax.axis_index("c"))

    shards = {"w_in_t": w_in[0].T, "w_ffn_in_t": w_ffn_in[0].T, "w_ret_out": w_ret_out[0],
              "w_sb_out": w_sb_out[0], "w_out": w_out[0], "w_ffn_out": w_ffn_out[0]}
    rest_rows = PACK_ROWS[1:]
    pack_rest = jnp.concatenate([shards[n].astype(BF16) for n, _ in rest_rows], axis=0)
    gathered_in, meta_all = all_gather([shards["w_in_t"].astype(BF16), meta_tokens])
    full = {"w_in_t": gathered_in.reshape(PROJ, D)}
    meta_full = meta_all.transpose(1, 0, 2).reshape(N_META, D)

    pos = jnp.arange(lq, dtype=F32) - META0
    half = RDK // 2
    ang = pos[:, None] * (ROPE_BASE ** (-jnp.arange(half, dtype=F32) / half))[None, :]
    cos, sin = jnp.cos(ang), jnp.sin(ang)
    decay, coef = _retention_tables()

    hp = jnp.concatenate([jnp.zeros((META0, D), F32), meta_full, x[0]], axis=0)
    hn1 = rms_fwd(hp, norm_mix_pre, "rms_mix_pre")
    proj = mm_nt(hn1, full["w_in_t"], BF16, "proj")
    o_ret, gr, states = retention_fwd(proj, cos, sin, decay, coef)
    o_sb, o_sb_f32, gathered_rest = sb_fwd(proj, (pack_rest, "gather"))
    off = 0
    for n, r in rest_rows:
        full[n] = gathered_rest[:, off:off + r, :].reshape(NDEV * r, D)
        off += r
    y_ret = mm_nn(gr, full["w_ret_out"], BF16, "y_ret")
    y_sb = mm_nn(o_sb, full["w_sb_out"], BF16, "y_sb")
    merged = merge_fwd(proj, y_ret, y_sb)
    mix = mm_nn(merged, full["w_out"], BF16, "mix")
    h1, hn2 = post_mix_fwd(hp, mix, norm_mix_post, norm_ffn_pre)
    ffn_a, ffn_b, act = ffn_in_swiglu(hn2, full["w_ffn_in_t"])
    ff = mm_nn(act, full["w_ffn_out"], BF16, "ffn_out")
    loss_blk, d_h2, d_ff, dg_ffn_post = loss_head(h1, ff, norm_ffn_post, loss_target[0])
    loss = lax.psum(loss_blk[0, 0], ("x", "y", "c"))

    grads = {}
    d_ffn_a, d_ffn_b = d_act_swiglu(d_ff, full["w_ffn_out"], ffn_a, ffn_b)
    grads["w_ffn_out"] = mm_tn(act, d_ff, BF16, "dw_ffn_out")
    d_hn2 = mm_nn_halves(d_ffn_a, d_ffn_b, full["w_ffn_in_t"], BF16, "d_hn2")
    grads["w_ffn_in_t"] = jnp.concatenate([mm_tn(d_ffn_a, hn2, BF16, "dw_ffn_in_gate"),
                                           mm_tn(d_ffn_b, hn2, BF16, "dw_ffn_in_up")], axis=0)
    d_h1, d_mix, dg_ffn_pre, dg_mix_post = post_mix_bwd(h1, d_hn2, norm_ffn_pre, d_h2, mix, norm_mix_post)
    d_merged = mm_nt(d_mix, full["w_out"], BF16, "d_merged")
    grads["w_out"] = mm_tn(merged, d_mix, BF16, "dw_out")
    d_y_ret, d_y_sb, d_proj = merge_bwd(proj, y_ret, y_sb, d_merged)
    d_gr = mm_nt(d_y_ret, full["w_ret_out"], BF16, "d_gr")
    grads["w_ret_out"] = mm_tn(gr, d_y_ret, BF16, "dw_ret_out")
    d_o_sb = mm_nt(d_y_sb, full["w_sb_out"], BF16, "d_o_sb")
    grads["w_sb_out"] = mm_tn(o_sb, d_y_sb, BF16, "dw_sb_out")
    d_proj = retention_bwd(proj, cos, sin, o_ret, d_gr, states, decay, coef, d_proj)
    parts_rest = jnp.concatenate([grads[n].reshape(NDEV, r, D) for n, r in rest_rows], axis=1)
    d_proj, d_sk, d_sv, landed_rest = sb_bwd(proj, o_sb_f32, d_o_sb, d_proj, (parts_rest, "scatter"))
    d_proj = lax.dynamic_update_slice(d_proj, d_sk, (0, C_SK))
    d_proj = lax.dynamic_update_slice(d_proj, d_sv, (0, C_SV))
    dw_in_t = mm_tn(d_proj, hn1, BF16, "dw_in")
    d_hn1, landed_in = mm_nn(d_proj, full["w_in_t"], BF16, "d_hn1",
                             exch=(dw_in_t.reshape(NDEV, PROJ // NDEV, D), "scatter"))
    d_hp, dg_mix_pre = pre_mix_bwd(hp, d_hn1, norm_mix_pre, d_h1)
    grad_x = d_hp[FRONT:][None]
    small = jnp.concatenate([dg_mix_pre, dg_mix_post, dg_ffn_pre, dg_ffn_post, d_hp[META0:FRONT]], axis=0)
    ssum = sum_slots(gather_small(small), "sum_small")
    g = {"w_in_t": sum_slots(landed_in, "sum_grads_in")}
    gsum = sum_slots(landed_rest, "sum_grads_rest")
    off = 0
    for n, r in rest_rows:
        g[n] = gsum[off:off + r]
        off += r
    gain = lambda k: ssum[k * GAIN_ROWS:k * GAIN_ROWS + 1]
    g_w = {"meta_tokens": lax.dynamic_slice(ssum[4 * GAIN_ROWS:], (0, me * (D // NDEV)), (N_META, D // NDEV)),
           "w_in": g["w_in_t"].T[None], "w_ret_out": g["w_ret_out"][None], "w_sb_out": g["w_sb_out"][None],
           "w_out": g["w_out"][None], "w_ffn_in": g["w_ffn_in_t"].T[None], "w_ffn_out": g["w_ffn_out"][None],
           "norm_mix_pre": gain(0), "norm_mix_post": gain(1), "norm_ffn_pre": gain(2), "norm_ffn_post": gain(3)}

    names = ["meta_tokens", "w_in", "w_ret_out", "w_sb_out", "w_out", "w_ffn_in", "w_ffn_out",
             "norm_mix_pre", "norm_mix_post", "norm_ffn_pre", "norm_ffn_post"]
    w_of = dict(zip(names, (meta_tokens, w_in, w_ret_out, w_sb_out, w_out, w_ffn_in, w_ffn_out,
                            norm_mix_pre, norm_mix_post, norm_ffn_pre, norm_ffn_post)))
    m_of = dict(zip(names, (m_meta_tokens, m_w_in, m_w_ret_out, m_w_sb_out, m_w_out, m_w_ffn_in, m_w_ffn_out,
                            m_norm_mix_pre, m_norm_mix_post, m_norm_ffn_pre, m_norm_ffn_post)))
    v_of = dict(zip(names, (v_meta_tokens, v_w_in, v_w_ret_out, v_w_sb_out, v_w_out, v_w_ffn_in, v_w_ffn_out,
                            v_norm_mix_pre, v_norm_mix_post, v_norm_ffn_pre, v_norm_ffn_post)))
    delta, new_m, new_v = {}, {}, {}
    for n in names:
        shape = w_of[n].shape
        two_d = (shape[-2], shape[-1])
        d_, m_, v_ = adamw(w_of[n].reshape(two_d), g_w[n].reshape(two_d), m_of[n].reshape(two_d),
                           v_of[n].reshape(two_d), "adamw_" + n)
        delta[n], new_m[n], new_v[n] = d_.reshape(shape), m_.reshape(shape), v_.reshape(shape)

    return (loss, grad_x, *[g_w[n] for n in names], *[delta[n] for n in names],
            *[new_m[n] for n in names], *[new_v[n] for n in names])
```

```python
import jax
import jax.numpy as jnp
from jax import lax
from jax.experimental import pallas as pl
from jax.experimental.pallas import tpu as pltpu

F32 = jnp.float32
BF16 = jnp.bfloat16

D = 1024
N_META = 16
CHUNK = 128
FRONT = 256
META0 = FRONT - N_META
RH, RDK, RDV = 4, 256, 512
SB_DH = 64
DFF = 2816
NDEV = 8
ROPE_BASE = 10000.0
NORM_EPS = 1e-6
GN_EPS = 1e-5
C_RQ, C_RK, C_RV, C_RG, C_SQ, C_SK, C_SV, C_GA, C_GB = 0, 1024, 2048, 4096, 6144, 7168, 8192, 9216, 10240
PROJ = 11264
PACK_ROWS = (("w_in_t", PROJ // NDEV), ("w_ffn_in_t", 2 * DFF // NDEV), ("w_ret_out", RH * RDV // NDEV),
             ("w_sb_out", D // NDEV), ("w_out", D // NDEV), ("w_ffn_out", DFF // NDEV))
PACK = sum(r for _, r in PACK_ROWS)
GAIN_ROWS = 8

ADAM_LR = 0.001
ADAM_B1 = 0.9
ADAM_B2 = 0.999
ADAM_EPS = 1e-08
ADAM_WD = 0.01
ADAM_STEP = 10

VMEM_LIMIT = 56 * 1024 * 1024
SB_T = 256

NT = (((1,), (1,)), ((), ()))
TN = (((0,), (0,)), ((), ()))


def _dot(a, b):
    return jnp.dot(a, b, preferred_element_type=F32)


def _dot_nt(a, b):
    return lax.dot_general(a, b, NT, preferred_element_type=F32)


def _dot_tn(a, b):
    return lax.dot_general(a, b, TN, preferred_element_type=F32)


WIDE_TILES = (1024, 1408, 512, 256)


def _tile(n, candidates):
    for t in candidates:
        if n % t == 0:
            return t
    return n


def _params(*sem):
    return pltpu.CompilerParams(dimension_semantics=sem, vmem_limit_bytes=VMEM_LIMIT)


def _rms_hat(x):
    r = lax.rsqrt(jnp.mean(x * x, axis=-1, keepdims=True) + NORM_EPS)
    return x * r, r


def _rms_bwd(xhat, r, g, dy):
    u = dy * g
    return r * (u - xhat * jnp.mean(u * xhat, axis=-1, keepdims=True))


def _gn(y):
    mu = jnp.mean(y, axis=-1, keepdims=True)
    yc = y - mu
    rs = lax.rsqrt(jnp.mean(yc * yc, axis=-1, keepdims=True) + GN_EPS)
    return yc * rs, rs


def _gn_bwd(yh, rs, d):
    return rs * (d - jnp.mean(d, axis=-1, keepdims=True) - yh * jnp.mean(d * yh, axis=-1, keepdims=True))


def _sigmoid(x):
    return 1.0 / (1.0 + jnp.exp(-x))


def _device_index(px, py, pc):
    return 4 * px + 2 * py + pc


def _direct_exchange(src_for, land_ref, send_sems, recv_sems, local_sem):
    x, y, c = lax.axis_index("x"), lax.axis_index("y"), lax.axis_index("c")
    me = _device_index(x, y, c)
    peers = []
    for k in range(1, NDEV):
        pos = (1 - x if k & 4 else x, 1 - y if k & 2 else y, 1 - c if k & 1 else c)
        peers.append((k - 1, pos, _device_index(*pos)))

    def local():
        return pltpu.make_async_copy(src_for(me), land_ref.at[me], local_sem)

    def remote(k, pos, idx):
        return pltpu.make_async_remote_copy(
            src_ref=src_for(idx), dst_ref=land_ref.at[me], send_sem=send_sems.at[k], recv_sem=recv_sems.at[k],
            device_id=pos, device_id_type=pl.DeviceIdType.MESH)

    def arrival(k, idx):
        return pltpu.make_async_remote_copy(
            src_ref=src_for(idx), dst_ref=land_ref.at[idx], send_sem=send_sems.at[k], recv_sem=recv_sems.at[k],
            device_id=(x, y, c), device_id_type=pl.DeviceIdType.MESH)

    def start():
        local().start()
        for p in peers:
            remote(*p).start()

    def wait():
        for k, _, idx in peers:
            arrival(k, idx).wait_recv()
        for p in peers:
            remote(*p).wait_send()
        local().wait()

    return start, wait


def _grid_call(compute, name, grid, arrays, in_specs, out_shapes, out_specs, scratch, exch, aliases=None):
    arrays, in_specs = list(arrays), list(in_specs)
    out_shapes, out_specs, scratch = list(out_shapes), list(out_specs), list(scratch)
    n_in, n_out, steps = len(arrays), len(out_shapes), grid[0] * grid[1]
    aliases = aliases or {}
    if exch is None:
        return pl.pallas_call(
            compute, name=name, grid=grid, in_specs=in_specs, out_specs=out_specs, out_shape=out_shapes,
            scratch_shapes=scratch, input_output_aliases=aliases,
            compiler_params=_params("parallel", "arbitrary"))(*arrays)
    src, mode = exch

    def body(*refs):
        refs = list(refs)
        src_ref = refs.pop(n_in)
        land_ref = refs.pop(n_in + n_out)
        send_sems, recv_sems, local_sem = refs[-3:]
        src_for = (lambda d: src_ref) if mode == "gather" else (lambda d: src_ref.at[d])
        start, wait = _direct_exchange(src_for, land_ref, send_sems, recv_sems, local_sem)
        step = pl.program_id(0) * grid[1] + pl.program_id(1)
        pl.when(step == 0)(start)
        compute(*refs[:-3])
        pl.when(step == steps - 1)(wait)

    any_spec = pl.BlockSpec(memory_space=pl.ANY)
    land = src.shape if mode == "scatter" else (NDEV,) + src.shape
    return pl.pallas_call(
        body, name=name, grid=grid, in_specs=in_specs + [any_spec], out_specs=out_specs + [any_spec],
        out_shape=out_shapes + [jax.ShapeDtypeStruct(land, src.dtype)],
        scratch_shapes=scratch + [pltpu.SemaphoreType.DMA((NDEV - 1,)), pltpu.SemaphoreType.DMA((NDEV - 1,)),
                                  pltpu.SemaphoreType.DMA(())],
        input_output_aliases=aliases, compiler_params=_params("arbitrary", "arbitrary"))(*(arrays + [src]))


def _mm_call(compute, name, grid, arrays, in_specs, out_shape, out_spec, acc_shape, exch):
    if acc_shape:
        body, scratch = compute, [pltpu.VMEM(acc_shape, F32)]
    else:
        body, scratch = (lambda a_ref, b_ref, o_ref: compute(a_ref, b_ref, o_ref, None)), []
    out = _grid_call(body, name, grid, arrays, in_specs, [out_shape], [out_spec], scratch, exch)
    return tuple(out) if exch else out[0]


def mm_nt(a, b, out_dtype, name, exch=None):
    m, k = a.shape
    n = b.shape[0]
    tm = _tile(m, (1408, 768, 512, 256))
    tn = _tile(n, WIDE_TILES)

    def compute(a_ref, b_ref, o_ref, acc_ref):
        o_ref[...] = _dot_nt(a_ref[...], b_ref[...]).astype(o_ref.dtype)

    return _mm_call(
        compute, name, (m // tm, n // tn), (a, b),
        [pl.BlockSpec((tm, k), lambda i, j: (i, 0)), pl.BlockSpec((tn, k), lambda i, j: (j, 0))],
        jax.ShapeDtypeStruct((m, n), out_dtype), pl.BlockSpec((tm, tn), lambda i, j: (i, j)), None, exch)


def _accumulate(dot, steps):
    def compute(a_ref, b_ref, o_ref, acc_ref):
        if steps == 1:
            o_ref[...] = dot(a_ref[...], b_ref[...]).astype(o_ref.dtype)
            return
        kk = pl.program_id(1)

        @pl.when(kk == 0)
        def _():
            acc_ref[...] = dot(a_ref[...], b_ref[...])

        @pl.when((kk > 0) & (kk < steps - 1))
        def _():
            acc_ref[...] += dot(a_ref[...], b_ref[...])

        @pl.when(kk == steps - 1)
        def _():
            o_ref[...] = (acc_ref[...] + dot(a_ref[...], b_ref[...])).astype(o_ref.dtype)

    return compute


def mm_nn(a, b, out_dtype, name, exch=None):
    m, k = a.shape
    n = b.shape[1]
    tm = _tile(m, (768, 512, 256))
    tk = _tile(k, (2816,) + WIDE_TILES)
    return _mm_call(
        _accumulate(_dot, k // tk), name, (m // tm, k // tk), (a, b),
        [pl.BlockSpec((tm, tk), lambda i, kk: (i, kk)), pl.BlockSpec((tk, n), lambda i, kk: (kk, 0))],
        jax.ShapeDtypeStruct((m, n), out_dtype), pl.BlockSpec((tm, n), lambda i, kk: (i, 0)), (tm, n), exch)


def mm_tn(a, b, out_dtype, name, exch=None):
    m, ka = a.shape
    n = b.shape[1]
    ta = _tile(ka, WIDE_TILES)
    tl = _tile(m, (1408, 768, 512, 256))
    return _mm_call(
        _accumulate(_dot_tn, m // tl), name, (ka // ta, m // tl), (a, b),
        [pl.BlockSpec((tl, ta), lambda i, ll: (ll, i)), pl.BlockSpec((tl, n), lambda i, ll: (ll, 0))],
        jax.ShapeDtypeStruct((ka, n), out_dtype), pl.BlockSpec((ta, n), lambda i, ll: (i, 0)), (ta, n), exch)


TM = 384


def _rb(arr, width=None, col_block=0, tm=TM):
    w = arr.shape[1] if width is None else width
    return pl.BlockSpec((tm, w), lambda i: (i, col_block))


def _whole(arr):
    return pl.BlockSpec(arr.shape, lambda i: (0,) * arr.ndim)


def _rows_call(body, name, lq, ins, in_specs, out_shapes, out_specs, aliases=None, tm=TM):
    return pl.pallas_call(
        body, name=name, grid=(lq // tm,), in_specs=in_specs, out_specs=out_specs, out_shape=out_shapes,
        input_output_aliases=aliases or {}, compiler_params=_params("arbitrary"),
    )(*ins)


ANY_SPEC = pl.BlockSpec(memory_space=pl.ANY)


def _d_proj_shape(lq):
    return jax.ShapeDtypeStruct((lq, PROJ), BF16)


def rms_fwd(h, g, name):
    lq = h.shape[0]

    def body(h_ref, g_ref, o_ref):
        xhat, _ = _rms_hat(h_ref[...])
        o_ref[...] = (xhat * g_ref[...]).astype(BF16)

    return _rows_call(body, name, lq, (h, g), [_rb(h), _whole(g)],
                      jax.ShapeDtypeStruct((lq, D), BF16), _rb(h))


def merge_fwd(proj, y_ret, y_sb):
    lq = proj.shape[0]

    def body(ga_ref, gb_ref, yr_ref, ys_ref, o_ref):
        o_ref[...] = (_sigmoid(ga_ref[...].astype(F32)) * yr_ref[...].astype(F32)
                      + _sigmoid(gb_ref[...].astype(F32)) * ys_ref[...].astype(F32)).astype(BF16)

    return _rows_call(body, "merge_fwd", lq, (proj, proj, y_ret, y_sb),
                      [_rb(proj, D, C_GA // D), _rb(proj, D, C_GB // D), _rb(y_ret), _rb(y_sb)],
                      jax.ShapeDtypeStruct((lq, D), BF16), _rb(y_ret))


def merge_bwd(proj, y_ret, y_sb, d_merged):
    lq = proj.shape[0]

    def body(gate_ref, yr_ref, ys_ref, d_ref, dyr_ref, dys_ref, dg_ref):
        d = d_ref[...].astype(F32)
        sg = _sigmoid(gate_ref[...].astype(F32))
        for g, y_ref, dy_ref in ((0, yr_ref, dyr_ref), (1, ys_ref, dys_ref)):
            @pl.when(pl.program_id(1) == g)
            def _():
                dy_ref[...] = (d * sg).astype(BF16)
                dg_ref[...] = (d * y_ref[...].astype(F32) * sg * (1.0 - sg)).astype(BF16)

    tm = 2 * TM
    row = pl.BlockSpec((tm, D), lambda i, g: (i, 0))
    o1 = jax.ShapeDtypeStruct((lq, D), BF16)
    return pl.pallas_call(
        body, name="merge_bwd", grid=(lq // tm, 2),
        in_specs=[pl.BlockSpec((tm, D), lambda i, g: (i, C_GA // D + g)), row, row, row],
        out_specs=(row, row, pl.BlockSpec((tm, D), lambda i, g: (i, C_GA // D + g))),
        out_shape=(o1, o1, _d_proj_shape(lq)),
        compiler_params=_params("arbitrary", "arbitrary"),
    )(proj, y_ret, y_sb, d_merged)


def post_mix_fwd(hp, mix, g_post, g_pre):
    lq = hp.shape[0]

    def body(h_ref, m_ref, g2_ref, g3_ref, h1_ref, hn_ref):
        mhat, _ = _rms_hat(m_ref[...].astype(F32))
        h1 = h_ref[...] + mhat * g2_ref[...]
        h1_ref[...] = h1
        hhat, _ = _rms_hat(h1)
        hn_ref[...] = (hhat * g3_ref[...]).astype(BF16)

    return _rows_call(body, "post_mix_fwd", lq, (hp, mix, g_post, g_pre),
                      [_rb(hp), _rb(mix), _whole(g_post), _whole(g_pre)],
                      (jax.ShapeDtypeStruct((lq, D), F32), jax.ShapeDtypeStruct((lq, D), BF16)),
                      (_rb(hp), _rb(hp)))


def ffn_in_swiglu(x, w_t):
    m, k = x.shape
    tm = _tile(m, (768, 512, 256))
    tn = _tile(DFF, WIDE_TILES)
    nj = DFF // tn

    def body(x_ref, wa_ref, wb_ref, a_ref, b_ref, act_ref):
        wa, wb = wa_ref[...], wb_ref[...]
        for r in range(0, tm, 256):
            rs = slice(r, r + 256)
            xv = x_ref[rs, :]
            a = _dot_nt(xv, wa)
            b = _dot_nt(xv, wb)
            a_ref[rs, :] = a.astype(BF16)
            b_ref[rs, :] = b.astype(BF16)
            act_ref[rs, :] = (a * _sigmoid(a) * b).astype(BF16)

    out = jax.ShapeDtypeStruct((m, DFF), BF16)
    blk = pl.BlockSpec((tm, tn), lambda i, j: (i, j))
    return pl.pallas_call(
        body, name="ffn_in_swiglu", grid=(m // tm, nj),
        in_specs=[pl.BlockSpec((tm, k), lambda i, j: (i, 0)), pl.BlockSpec((tn, k), lambda i, j: (j, 0)),
                  pl.BlockSpec((tn, k), lambda i, j: (nj + j, 0))],
        out_specs=(blk, blk, blk), out_shape=(out, out, out),
        compiler_params=_params("parallel", "arbitrary"),
    )(x, w_t, w_t)


def d_act_swiglu(d_ff, w_out, a, b):
    m, k = d_ff.shape
    tm = _tile(m, (768, 512, 256))
    tn = _tile(DFF, WIDE_TILES)

    def body(d_ref, w_ref, a_ref, b_ref, da_ref, db_ref):
        w = w_ref[...]
        for r in range(0, tm, 256):
            rs = slice(r, r + 256)
            d = _dot_nt(d_ref[rs, :], w)
            av = a_ref[rs, :].astype(F32)
            sg = _sigmoid(av)
            da_ref[rs, :] = (d * b_ref[rs, :].astype(F32) * sg * (1.0 + av * (1.0 - sg))).astype(BF16)
            db_ref[rs, :] = (d * av * sg).astype(BF16)

    out = jax.ShapeDtypeStruct((m, DFF), BF16)
    blk = pl.BlockSpec((tm, tn), lambda i, j: (i, j))
    return pl.pallas_call(
        body, name="d_act_swiglu", grid=(m // tm, DFF // tn),
        in_specs=[pl.BlockSpec((tm, k), lambda i, j: (i, 0)), pl.BlockSpec((tn, k), lambda i, j: (j, 0)), blk, blk],
        out_specs=(blk, blk), out_shape=(out, out),
        compiler_params=_params("parallel", "arbitrary"),
    )(d_ff, w_out, a, b)


def mm_nn_halves(a0, a1, b, out_dtype, name):
    m, kh = a0.shape
    n = b.shape[1]
    tm = _tile(m, (768, 512, 256))

    def body(a0_ref, a1_ref, b_ref, o_ref, acc_ref):
        @pl.when(pl.program_id(1) == 0)
        def _():
            acc_ref[...] = _dot(a0_ref[...], b_ref[...])

        @pl.when(pl.program_id(1) == 1)
        def _():
            o_ref[...] = (acc_ref[...] + _dot(a1_ref[...], b_ref[...])).astype(o_ref.dtype)

    half = pl.BlockSpec((tm, kh), lambda i, kk: (i, 0))
    return pl.pallas_call(
        body, name=name, grid=(m // tm, 2),
        in_specs=[half, half, pl.BlockSpec((kh, n), lambda i, kk: (kk, 0))],
        out_specs=pl.BlockSpec((tm, n), lambda i, kk: (i, 0)),
        out_shape=jax.ShapeDtypeStruct((m, n), out_dtype),
        scratch_shapes=[pltpu.VMEM((tm, n), F32)],
        compiler_params=_params("parallel", "arbitrary"),
    )(a0, a1, b)


def loss_head(h1, ff, g_post, target):
    lq = h1.shape[0]
    front_blocks = 1
    rb = lambda a: _rb(a, tm=FRONT)

    def body(h_ref, f_ref, g_ref, t_ref, loss_ref, dh_ref, df_ref, dg_ref):
        i = pl.program_id(0)

        @pl.when(i == 0)
        def _():
            loss_ref[...] = jnp.zeros_like(loss_ref)
            dg_ref[...] = jnp.zeros_like(dg_ref)

        g = g_ref[...]
        fhat, r = _rms_hat(f_ref[...].astype(F32))
        is_x = (i >= front_blocks).astype(F32)
        diff = (h_ref[...] + fhat * g - t_ref[...]) * is_x
        loss_ref[...] += 0.5 * jnp.sum(diff * diff) / D
        dy = diff / D
        dh_ref[...] = dy
        df_ref[...] = _rms_bwd(fhat, r, g, dy).astype(BF16)
        dg_ref[...] += jnp.sum(dy * fhat, axis=0, keepdims=True)

    return _rows_call(
        body, "loss_head", lq, (h1, ff, g_post, target),
        [rb(h1), rb(ff), _whole(g_post),
         pl.BlockSpec((FRONT, D), lambda i: (jnp.maximum(i - front_blocks, 0), 0))],
        (jax.ShapeDtypeStruct((8, 128), F32), jax.ShapeDtypeStruct((lq, D), F32),
         jax.ShapeDtypeStruct((lq, D), BF16), jax.ShapeDtypeStruct((GAIN_ROWS, D), F32)),
        (pl.BlockSpec((8, 128), lambda i: (0, 0)), rb(h1), rb(h1), pl.BlockSpec((GAIN_ROWS, D), lambda i: (0, 0))),
        tm=FRONT)


def post_mix_bwd(h1, d_hn2, g_pre, d_h2, mix, g_post):
    lq = h1.shape[0]

    def body(h_ref, dn_ref, g3_ref, dh2_ref, m_ref, g2_ref, dh1_ref, dm_ref, dg3_ref, dg2_ref):
        i = pl.program_id(0)

        @pl.when(i == 0)
        def _():
            dg3_ref[...] = jnp.zeros_like(dg3_ref)
            dg2_ref[...] = jnp.zeros_like(dg2_ref)

        hhat, r = _rms_hat(h_ref[...])
        dn = dn_ref[...].astype(F32)
        d_h1 = dh2_ref[...] + _rms_bwd(hhat, r, g3_ref[...], dn)
        dh1_ref[...] = d_h1
        dg3_ref[...] += jnp.sum(dn * hhat, axis=0, keepdims=True)
        mhat, rm = _rms_hat(m_ref[...].astype(F32))
        dm_ref[...] = _rms_bwd(mhat, rm, g2_ref[...], d_h1).astype(BF16)
        dg2_ref[...] += jnp.sum(d_h1 * mhat, axis=0, keepdims=True)

    vec = jax.ShapeDtypeStruct((GAIN_ROWS, D), F32)
    vspec = pl.BlockSpec((GAIN_ROWS, D), lambda i: (0, 0))
    return _rows_call(body, "post_mix_bwd", lq, (h1, d_hn2, g_pre, d_h2, mix, g_post),
                      [_rb(h1), _rb(d_hn2), _whole(g_pre), _rb(d_h2), _rb(mix), _whole(g_post)],
                      (jax.ShapeDtypeStruct((lq, D), F32), jax.ShapeDtypeStruct((lq, D), BF16), vec, vec),
                      (_rb(h1), _rb(h1), vspec, vspec))


def pre_mix_bwd(hp, d_hn1, g_pre, d_h1):
    lq = hp.shape[0]

    def body(h_ref, dn_ref, g_ref, dh1_ref, dhp_ref, dg_ref):
        i = pl.program_id(0)

        @pl.when(i == 0)
        def _():
            dg_ref[...] = jnp.zeros_like(dg_ref)

        hhat, r = _rms_hat(h_ref[...])
        dn = dn_ref[...].astype(F32)
        dhp_ref[...] = dh1_ref[...] + _rms_bwd(hhat, r, g_ref[...], dn)
        dg_ref[...] += jnp.sum(dn * hhat, axis=0, keepdims=True)

    return _rows_call(body, "pre_mix_bwd", lq, (hp, d_hn1, g_pre, d_h1),
                      [_rb(hp), _rb(d_hn1), _whole(g_pre), _rb(d_h1)],
                      (jax.ShapeDtypeStruct((lq, D), F32), jax.ShapeDtypeStruct((GAIN_ROWS, D), F32)),
                      (_rb(hp), pl.BlockSpec((GAIN_ROWS, D), lambda i: (0, 0))))


def _retention_tables():
    log_g = jnp.log1p(-(2.0 ** (-5.0 - jnp.arange(RH, dtype=F32))))
    idx = jnp.arange(CHUNK, dtype=F32)
    diff = idx[:, None] - idx[None, :]
    decay = jnp.where(diff >= 0, jnp.exp(log_g[:, None, None] * jnp.maximum(diff, 0.0)), 0.0)
    zeta = jnp.exp(log_g[:, None] * (CHUNK - 1.0 - idx))
    xi = jnp.exp(log_g[:, None] * (idx + 1.0))
    g_chunk = jnp.broadcast_to(jnp.exp(log_g * CHUNK)[:, None], (RH, CHUNK))
    coef = jnp.stack([xi, zeta, g_chunk] + [jnp.zeros_like(xi)] * 125, axis=-1)
    return decay, coef


def _rotated_qk(p_ref, h, cos, sin):
    half = RDK // 2
    out = []
    for col, scale in ((C_RQ, RDK ** -0.5), (C_RK, 1.0)):
        x1 = p_ref[:, col + h * RDK: col + h * RDK + half].astype(F32)
        x2 = p_ref[:, col + h * RDK + half: col + (h + 1) * RDK].astype(F32)
        out.append(jnp.concatenate([(x1 * cos - x2 * sin) * scale, (x1 * sin + x2 * cos) * scale],
                                   axis=1).astype(BF16))
    return out


RET_COLS = C_RG + RH * RDV


def retention_fwd(proj, cos, sin, decay, coef):
    lq = proj.shape[0]
    n = lq // CHUNK

    def body(p_ref, c_ref, s_ref, dec_ref, cf_ref, o_ref, gr_ref, st_ref, state):
        @pl.when(pl.program_id(0) == 0)
        def _():
            state[...] = jnp.zeros_like(state)

        cos_, sin_ = c_ref[...], s_ref[...]
        for h in range(RH):
            vv = slice(h * RDV, (h + 1) * RDV)
            q, k = _rotated_qk(p_ref, h, cos_, sin_)
            v = p_ref[:, C_RV + h * RDV: C_RV + (h + 1) * RDV]
            xi, zeta, gch = cf_ref[h, :, 0:1], cf_ref[h, :, 1:2], cf_ref[h, 0:1, 2:3]
            st = state[h]
            stb = st.astype(BF16)
            st_ref[h] = stb
            s = _dot_nt(q, k) * dec_ref[h]
            o = _dot(s.astype(BF16), v) + _dot(q, stb) * xi
            o_ref[:, vv] = o
            yh, _ = _gn(o)
            rg = p_ref[:, C_RG + h * RDV: C_RG + (h + 1) * RDV].astype(F32)
            gr_ref[:, vv] = (rg * _sigmoid(rg) * yh).astype(BF16)
            vz = (v.astype(F32) * zeta).astype(BF16)
            state[h] = gch * st + _dot_tn(k, vz)

    v_spec = pl.BlockSpec((CHUNK, RH * RDV), lambda c: (c, 0))
    pos_spec = pl.BlockSpec((CHUNK, RDK // 2), lambda c: (c, 0))
    return pl.pallas_call(
        body, name="retention_fwd", grid=(n,),
        in_specs=[pl.BlockSpec((CHUNK, RET_COLS), lambda c: (c, 0)), pos_spec, pos_spec,
                  pl.BlockSpec((RH, CHUNK, CHUNK), lambda c: (0, 0, 0)),
                  pl.BlockSpec((RH, CHUNK, 128), lambda c: (0, 0, 0))],
        out_specs=(v_spec, v_spec, pl.BlockSpec((None, RH, RDK, RDV), lambda c: (c, 0, 0, 0))),
        out_shape=(jax.ShapeDtypeStruct((lq, RH * RDV), F32), jax.ShapeDtypeStruct((lq, RH * RDV), BF16),
                   jax.ShapeDtypeStruct((n, RH, RDK, RDV), BF16)),
        scratch_shapes=[pltpu.VMEM((RH, RDK, RDV), F32)],
        compiler_params=_params("arbitrary"),
    )(proj, cos, sin, decay, coef)


def retention_bwd(proj, cos, sin, o_ret, d_gr, states, decay, coef, d_proj):
    lq = proj.shape[0]
    n = lq // CHUNK
    half = RDK // 2

    def body(p_ref, c_ref, s_ref, y_ref, dgr_ref, st_ref, dec_ref, cf_ref, _, dp_ref, dstate):
        @pl.when(pl.program_id(0) == 0)
        def _():
            dstate[...] = jnp.zeros_like(dstate)

        cos_, sin_ = c_ref[...], s_ref[...]
        for h in range(RH):
            vv = slice(h * RDV, (h + 1) * RDV)
            q, k = _rotated_qk(p_ref, h, cos_, sin_)
            v = p_ref[:, C_RV + h * RDV: C_RV + (h + 1) * RDV]
            yh, rs = _gn(y_ref[:, vv])
            rg = p_ref[:, C_RG + h * RDV: C_RG + (h + 1) * RDV].astype(F32)
            sg = _sigmoid(rg)
            d = dgr_ref[:, vv].astype(F32)
            dp_ref[:, C_RG + h * RDV: C_RG + (h + 1) * RDV] = (d * yh * sg * (1.0 + rg * (1.0 - sg))).astype(BF16)
            dob = _gn_bwd(yh, rs, d * rg * sg).astype(BF16)
            xi, zeta, gch = cf_ref[h, :, 0:1], cf_ref[h, :, 1:2], cf_ref[h, 0:1, 2:3]
            dec = dec_ref[h]
            dsn = dstate[h]
            dsnb = dsn.astype(BF16)
            dox = (dob.astype(F32) * xi).astype(BF16)
            sb = (_dot_nt(q, k) * dec).astype(BF16)
            dsb = (_dot_nt(dob, v) * dec).astype(BF16)
            vz = (v.astype(F32) * zeta).astype(BF16)
            dq = _dot(dsb, k) + _dot_nt(dox, st_ref[h])
            dk = _dot_tn(dsb, q) + _dot_nt(vz, dsnb)
            dp_ref[:, C_RV + h * RDV: C_RV + (h + 1) * RDV] = (_dot_tn(sb, dob) + _dot(k, dsnb) * zeta).astype(BF16)
            dstate[h] = gch * dsn + _dot_tn(q, dox)
            for col, g, scale in ((C_RQ, dq, RDK ** -0.5), (C_RK, dk, 1.0)):
                d1, d2 = g[:, :half], g[:, half:]
                dp_ref[:, col + h * RDK: col + h * RDK + half] = ((d1 * cos_ + d2 * sin_) * scale).astype(BF16)
                dp_ref[:, col + h * RDK + half: col + (h + 1) * RDK] = ((d2 * cos_ - d1 * sin_) * scale).astype(BF16)

    rev = lambda c: n - 1 - c
    v_spec = pl.BlockSpec((CHUNK, RH * RDV), lambda c: (rev(c), 0))
    pos_spec = pl.BlockSpec((CHUNK, half), lambda c: (rev(c), 0))
    ret_cols = pl.BlockSpec((CHUNK, RET_COLS), lambda c: (rev(c), 0))
    return pl.pallas_call(
        body, name="retention_bwd", grid=(n,),
        in_specs=[ret_cols, pos_spec, pos_spec, v_spec, v_spec,
                  pl.BlockSpec((None, RH, RDK, RDV), lambda c: (rev(c), 0, 0, 0)),
                  pl.BlockSpec((RH, CHUNK, CHUNK), lambda c: (0, 0, 0)),
                  pl.BlockSpec((RH, CHUNK, 128), lambda c: (0, 0, 0)), ANY_SPEC],
        out_specs=ret_cols, out_shape=_d_proj_shape(lq),
        scratch_shapes=[pltpu.VMEM((RH, RDK, RDV), F32)],
        input_output_aliases={8: 0}, compiler_params=_params("arbitrary"),
    )(proj, cos, sin, o_ret, d_gr, states, decay, coef, d_proj)


def _sb_masks_and_u():
    lane = lax.broadcasted_iota(jnp.int32, (1, 2 * SB_DH), 1)
    lo = lane < SB_DH
    row = lax.broadcasted_iota(jnp.int32, (SB_T, SB_T), 0)
    col = lax.broadcasted_iota(jnp.int32, (SB_T, SB_T), 1)
    u = (row > col).astype(BF16)
    return lo, u, row, col


def _sb_rows(j):
    start = j * SB_T
    return pl.ds(start if isinstance(j, int) else pl.multiple_of(start, SB_T), SB_T)


SB_DEAD = -104.0
SB_ROWS = SB_T
SB_CHAINS = tuple((hh, slice(r, r + SB_ROWS)) for hh in range(2) for r in range(0, SB_T, SB_ROWS))


def _sb_alive(logs):
    m = logs[0]
    for l in logs[1:]:
        m = jnp.maximum(m, l)
    return (jnp.max(m) > SB_DEAD).astype(jnp.int32)


def _sb_walk(i, run, logs_of, step):
    def cond(c):
        return (c[0] >= 1) & (c[1] > 0)

    def body(c):
        r = step(c[0], c[2])
        return c[0] - 1, _sb_alive(logs_of(r)), r

    return lax.while_loop(cond, body, (i - 1, _sb_alive(logs_of(run)), run))


def _sb_logs(z):
    ls = jnp.minimum(z, 0.0) - jnp.log(1.0 + jnp.exp(-jnp.abs(z)))
    return ls, ls - z


def sb_fwd(proj, exch):
    lq = proj.shape[0]
    t = SB_T

    def body(q_ref, k_ref, v_ref, o_ref, of_ref, acc_ref):
        i = pl.program_id(1)
        lo, u, row, col = _sb_masks_and_u()
        qs = (q_ref[...].astype(F32) * SB_DH ** -0.5).astype(BF16)
        zero = jnp.zeros_like(qs)
        qh = (jnp.where(lo, qs, zero), jnp.where(lo, zero, qs))
        acc_ref[...] = jnp.zeros_like(acc_ref)

        def block(j, run, masked):
            rows = _sb_rows(j)
            ks, vs = k_ref[rows, :], v_ref[rows, :]
            if masked:
                valid = (col + j * t < row + i * t) & (col + j * t >= META0)
            ls, ln = _sb_logs(_dot_nt(q2, ks))
            if masked:
                valid2 = jnp.concatenate([valid, valid], axis=0)
                ln = jnp.where(valid2, ln, 0.0)
            a = jnp.exp(ls + _dot(ln.astype(BF16), u) + run[0])
            if masked:
                a = jnp.where(valid2, a, 0.0)
            av = _dot(a.astype(BF16), vs)
            acc_ref[0] += av[:t]
            acc_ref[1] += av[t:]
            return (run[0] + jnp.sum(ln, axis=1, keepdims=True),)

        q2 = jnp.concatenate(qh, axis=0)
        run = block(i, (jnp.zeros((2 * t, 1), F32),), True)
        _, go, run = _sb_walk(i, run, lambda r: r, lambda j, r: block(j, r, False))

        @pl.when((i > 0) & (go > 0))
        def _():
            block(0, run, True)

        o = jnp.where(lo, acc_ref[0], acc_ref[1])
        o_ref[...] = o.astype(BF16)
        of_ref[...] = o

    blk = pl.BlockSpec((t, 128), lambda p, i: (i, p))
    return _grid_call(
        body, "sb_fwd", (D // 128, lq // t), (proj, proj, proj),
        [pl.BlockSpec((t, 128), lambda p, i: (i, C_SQ // 128 + p)),
         pl.BlockSpec((lq, 128), lambda p, i: (0, C_SK // 128 + p)),
         pl.BlockSpec((lq, 128), lambda p, i: (0, C_SV // 128 + p))],
        (jax.ShapeDtypeStruct((lq, D), BF16), jax.ShapeDtypeStruct((lq, D), F32)), (blk, blk),
        [pltpu.VMEM((2, t, 128), F32)], exch)


def sb_bwd(proj, o, d_o, d_proj, exch):
    lq = proj.shape[0]
    t = SB_T
    last = lq // t - 1

    def body(q_ref, k_ref, v_ref, o_ref, do_ref, _, dq_ref, dk_out, dv_out, acc_ref, dk_ref, dv_ref):
        i = pl.program_id(1)

        @pl.when(i == 0)
        def _():
            dk_ref[...] = jnp.zeros_like(dk_ref)
            dv_ref[...] = jnp.zeros_like(dv_ref)

        lo, u, row, col = _sb_masks_and_u()
        incl = (row <= col).astype(BF16)
        qs = (q_ref[...].astype(F32) * SB_DH ** -0.5).astype(BF16)
        do = do_ref[...]
        zero = jnp.zeros_like(qs)
        qh = (jnp.where(lo, qs, zero), jnp.where(lo, zero, qs))
        doh = (jnp.where(lo, do, zero), jnp.where(lo, zero, do))
        prod = o_ref[...] * do.astype(F32)
        dsum = (jnp.sum(jnp.where(lo, prod, 0.0), axis=1, keepdims=True),
                jnp.sum(jnp.where(lo, 0.0, prod), axis=1, keepdims=True))
        acc_ref[...] = jnp.zeros_like(acc_ref)

        def block(j, run, masked):
            rows = _sb_rows(j)
            ks, vs = k_ref[rows, :], v_ref[rows, :]
            if masked:
                valid = (col + j * t < row + i * t) & (col + j * t >= META0)
            out = []
            for hh in range(2):
                run_ln, run_e = run[2 * hh], run[2 * hh + 1]
                ls, ln = _sb_logs(_dot_nt(qh[hh], ks))
                if masked:
                    ln = jnp.where(valid, ln, 0.0)
                a = jnp.exp(ls + _dot(ln.astype(BF16), u) + run_ln)
                if masked:
                    a = jnp.where(valid, a, 0.0)
                ab = a.astype(BF16)
                e = ab.astype(F32) * _dot_nt(doh[hh], vs)
                e_sum = jnp.sum(e, axis=1, keepdims=True)
                upto = (dsum[hh] - run_e - e_sum) + _dot(e.astype(BF16), incl)
                dz = e - jnp.exp(ls) * upto
                if masked:
                    dz = jnp.where(valid, dz, 0.0)
                dzb = dz.astype(BF16)
                acc_ref[hh] += _dot(dzb, ks)
                dk_ref[rows, :] += _dot_tn(dzb, qh[hh])
                dv_ref[rows, :] += _dot_tn(ab, doh[hh])
                out += [run_ln + jnp.sum(ln, axis=1, keepdims=True), run_e + e_sum]
            return tuple(out)

        zeros = jnp.zeros((t, 1), F32)
        run = block(i, (zeros,) * 4, True)
        _, go, run = _sb_walk(i, run, lambda r: (r[0], r[2]), lambda j, r: block(j, r, False))

        @pl.when((i > 0) & (go > 0))
        def _():
            block(0, run, True)

        dq_ref[...] = (jnp.where(lo, acc_ref[0], acc_ref[1]) * SB_DH ** -0.5).astype(BF16)

        @pl.when(i == last)
        def _():
            dk_out[...] = dk_ref[...].astype(BF16)
            dv_out[...] = dv_ref[...].astype(BF16)

    blk = pl.BlockSpec((t, 128), lambda p, i: (i, p))
    q_cols = pl.BlockSpec((t, 128), lambda p, i: (i, C_SQ // 128 + p))
    col_blk = pl.BlockSpec((lq, 128), lambda p, i: (0, p))
    return _grid_call(
        body, "sb_bwd", (D // 128, lq // t), (proj, proj, proj, o, d_o, d_proj),
        [q_cols, pl.BlockSpec((lq, 128), lambda p, i: (0, C_SK // 128 + p)),
         pl.BlockSpec((lq, 128), lambda p, i: (0, C_SV // 128 + p)), blk, blk, ANY_SPEC],
        (_d_proj_shape(lq), jax.ShapeDtypeStruct((lq, D), BF16), jax.ShapeDtypeStruct((lq, D), BF16)),
        (q_cols, col_blk, col_blk),
        [pltpu.VMEM((2, t, 128), F32), pltpu.VMEM((lq, 128), F32), pltpu.VMEM((lq, 128), F32)], exch, {5: 0})


def all_gather(blocks):
    nb = len(blocks)

    def body(*refs):
        x_refs, out_refs = refs[:nb], refs[nb:2 * nb]
        send_sems, recv_sems, local_sems = refs[2 * nb:]
        x, y, c = lax.axis_index("x"), lax.axis_index("y"), lax.axis_index("c")
        me, sibling = (x, y, c), (x, y, 1 - c)
        chips = [(1 - x, y), (x, 1 - y), (1 - x, 1 - y)]

        def copy(b, k, block, to, src=None):
            slot = out_refs[b].at[_device_index(*block)]
            return pltpu.make_async_remote_copy(
                src_ref=slot if src is None else src, dst_ref=slot,
                send_sem=send_sems.at[b, k], recv_sem=recv_sems.at[b, k],
                device_id=to, device_id_type=pl.DeviceIdType.MESH)

        mine = [pltpu.make_async_copy(x_refs[b], out_refs[b].at[_device_index(*me)], local_sems.at[b])
                for b in range(nb)]
        for cp in mine:
            cp.start()
        first = []
        for b in range(nb):
            first.append(copy(b, 0, me, sibling, src=x_refs[b]))
            first += [copy(b, 1 + j, me, (*chip, c), src=x_refs[b]) for j, chip in enumerate(chips)]
        for cp in first:
            cp.start()
        passed = []
        for j, chip in enumerate(chips):
            for b in range(nb):
                copy(b, 1 + j, (*chip, c), me).wait_recv()
                cp = copy(b, 4 + j, (*chip, c), sibling)
                cp.start()
                passed.append(cp)
        for b in range(nb):
            copy(b, 0, sibling, me).wait_recv()
            for j, chip in enumerate(chips):
                copy(b, 4 + j, (*chip, 1 - c), me).wait_recv()
        for cp in first + passed:
            cp.wait_send()
        for cp in mine:
            cp.wait()

    any_spec = pl.BlockSpec(memory_space=pl.ANY)
    return pl.pallas_call(
        body, name="all_gather",
        in_specs=[any_spec] * nb, out_specs=[any_spec] * nb,
        out_shape=[jax.ShapeDtypeStruct((NDEV,) + b.shape, b.dtype) for b in blocks],
        scratch_shapes=[pltpu.SemaphoreType.DMA((nb, 7)), pltpu.SemaphoreType.DMA((nb, 7)),
                        pltpu.SemaphoreType.DMA((nb,))],
    )(*blocks)


def gather_small(small):
    def body(s_ref, land_ref, send_sems, recv_sems, local_sem):
        start, wait = _direct_exchange(lambda d: s_ref, land_ref, send_sems, recv_sems, local_sem)
        start()
        wait()

    any_spec = pl.BlockSpec(memory_space=pl.ANY)
    return pl.pallas_call(
        body, name="gather_small", in_specs=[any_spec], out_specs=any_spec,
        out_shape=jax.ShapeDtypeStruct((NDEV,) + small.shape, small.dtype),
        scratch_shapes=[pltpu.SemaphoreType.DMA((NDEV - 1,)), pltpu.SemaphoreType.DMA((NDEV - 1,)),
                        pltpu.SemaphoreType.DMA(())],
    )(small)


def sum_slots(landed, name):
    _, r, c = landed.shape
    tr = _tile(r, (352, 224, 8))

    def body(l_ref, o_ref):
        acc = l_ref[0].astype(F32)
        for p in range(1, NDEV):
            acc = acc + l_ref[p].astype(F32)
        o_ref[...] = acc

    return pl.pallas_call(
        body, name=name, grid=(r // tr,),
        in_specs=[pl.BlockSpec((NDEV, tr, c), lambda i: (0, i, 0))],
        out_specs=pl.BlockSpec((tr, c), lambda i: (i, 0)),
        out_shape=jax.ShapeDtypeStruct((r, c), F32),
        compiler_params=_params("parallel"),
    )(landed)


def adamw(w, g, m, v, name):
    r, c = w.shape
    tr = _tile(r, (256, 128))

    def body(w_ref, g_ref, m_ref, v_ref, d_ref, nm_ref, nv_ref):
        g_ = g_ref[...]
        m_ = ADAM_B1 * m_ref[...] + (1.0 - ADAM_B1) * g_
        v_ = ADAM_B2 * v_ref[...] + (1.0 - ADAM_B2) * jnp.square(g_)
        m_hat = m_ / (1.0 - ADAM_B1 ** ADAM_STEP)
        v_hat = v_ / (1.0 - ADAM_B2 ** ADAM_STEP)
        d_ref[...] = -ADAM_LR * (m_hat / (jnp.sqrt(v_hat) + ADAM_EPS) + ADAM_WD * w_ref[...])
        nm_ref[...] = m_
        nv_ref[...] = v_

    spec = pl.BlockSpec((tr, c), lambda i: (i, 0))
    out = jax.ShapeDtypeStruct((r, c), F32)
    return pl.pallas_call(
        body, name=name, grid=(r // tr,), in_specs=[spec] * 4, out_specs=(spec,) * 3, out_shape=(out,) * 3,
        compiler_params=_params("parallel"),
    )(w, g, m, v)


def kernel(x, meta_tokens, w_in, w_ret_out, w_sb_out, w_out, w_ffn_in, w_ffn_out, norm_mix_pre, norm_mix_post, norm_ffn_pre, norm_ffn_post, loss_target, m_meta_tokens, m_w_in, m_w_ret_out, m_w_sb_out, m_w_out, m_w_ffn_in, m_w_ffn_out, m_norm_mix_pre, m_norm_mix_post, m_norm_ffn_pre, m_norm_ffn_post, v_meta_tokens, v_w_in, v_w_ret_out, v_w_sb_out, v_w_out, v_w_ffn_in, v_w_ffn_out, v_norm_mix_pre, v_norm_mix_post, v_norm_ffn_pre, v_norm_ffn_post):
    seq = x.shape[1]
    lq = seq + FRONT
    me = _device_index(lax.axis_index("x"), lax.axis_index("y"), lax.axis_index("c"))

    shards = {"w_in_t": w_in[0].T, "w_ffn_in_t": w_ffn_in[0].T, "w_ret_out": w_ret_out[0],
              "w_sb_out": w_sb_out[0], "w_out": w_out[0], "w_ffn_out": w_ffn_out[0]}
    rest_rows = PACK_ROWS[1:]
    pack_rest = jnp.concatenate([shards[n].astype(BF16) for n, _ in rest_rows], axis=0)
    gathered_in, meta_all = all_gather([shards["w_in_t"].astype(BF16), meta_tokens])
    full = {"w_in_t": gathered_in.reshape(PROJ, D)}
    meta_full = meta_all.transpose(1, 0, 2).reshape(N_META, D)

    pos = jnp.arange(lq, dtype=F32) - META0
    half = RDK // 2
    ang = pos[:, None] * (ROPE_BASE ** (-jnp.arange(half, dtype=F32) / half))[None, :]
    cos, sin = jnp.cos(ang), jnp.sin(ang)
    decay, coef = _retention_tables()

    hp = jnp.concatenate([jnp.zeros((META0, D), F32), meta_full, x[0]], axis=0)
    hn1 = rms_fwd(hp, norm_mix_pre, "rms_mix_pre")
    proj = mm_nt(hn1, full["w_in_t"], BF16, "proj")
    o_ret, gr, states = retention_fwd(proj, cos, sin, decay, coef)
    o_sb, o_sb_f32, gathered_rest = sb_fwd(proj, (pack_rest, "gather"))
    off = 0
    for n, r in rest_rows:
        full[n] = gathered_rest[:, off:off + r, :].reshape(NDEV * r, D)
        off += r
    y_ret = mm_nn(gr, full["w_ret_out"], BF16, "y_ret")
    y_sb = mm_nn(o_sb, full["w_sb_out"], BF16, "y_sb")
    merged = merge_fwd(proj, y_ret, y_sb)
    mix = mm_nn(merged, full["w_out"], BF16, "mix")
    h1, hn2 = post_mix_fwd(hp, mix, norm_mix_post, norm_ffn_pre)
    ffn_a, ffn_b, act = ffn_in_swiglu(hn2, full["w_ffn_in_t"])
    ff = mm_nn(act, full["w_ffn_out"], BF16, "ffn_out")
    loss_blk, d_h2, d_ff, dg_ffn_post = loss_head(h1, ff, norm_ffn_post, loss_target[0])
    loss = lax.psum(loss_blk[0, 0], ("x", "y", "c"))

    grads = {}
    d_ffn_a, d_ffn_b = d_act_swiglu(d_ff, full["w_ffn_out"], ffn_a, ffn_b)
    grads["w_ffn_out"] = mm_tn(act, d_ff, BF16, "dw_ffn_out")
    d_hn2 = mm_nn_halves(d_ffn_a, d_ffn_b, full["w_ffn_in_t"], BF16, "d_hn2")
    grads["w_ffn_in_t"] = jnp.concatenate([mm_tn(d_ffn_a, hn2, BF16, "dw_ffn_in_gate"),
                                           mm_tn(d_ffn_b, hn2, BF16, "dw_ffn_in_up")], axis=0)
    d_h1, d_mix, dg_ffn_pre, dg_mix_post = post_mix_bwd(h1, d_hn2, norm_ffn_pre, d_h2, mix, norm_mix_post)
    d_merged = mm_nt(d_mix, full["w_out"], BF16, "d_merged")
    grads["w_out"] = mm_tn(merged, d_mix, BF16, "dw_out")
    d_y_ret, d_y_sb, d_proj = merge_bwd(proj, y_ret, y_sb, d_merged)
    d_gr = mm_nt(d_y_ret, full["w_ret_out"], BF16, "d_gr")
    grads["w_ret_out"] = mm_tn(gr, d_y_ret, BF16, "dw_ret_out")
    d_o_sb = mm_nt(d_y_sb, full["w_sb_out"], BF16, "d_o_sb")
    grads["w_sb_out"] = mm_tn(o_sb, d_y_sb, BF16, "dw_sb_out")
    d_proj = retention_bwd(proj, cos, sin, o_ret, d_gr, states, decay, coef, d_proj)
    parts_rest = jnp.concatenate([grads[n].reshape(NDEV, r, D) for n, r in rest_rows], axis=1)
    d_proj, d_sk, d_sv, landed_rest = sb_bwd(proj, o_sb_f32, d_o_sb, d_proj, (parts_rest, "scatter"))
    d_proj = lax.dynamic_update_slice(d_proj, d_sk, (0, C_SK))
    d_proj = lax.dynamic_update_slice(d_proj, d_sv, (0, C_SV))
    dw_in_t = mm_tn(d_proj, hn1, BF16, "dw_in")
    d_hn1, landed_in = mm_nn(d_proj, full["w_in_t"], BF16, "d_hn1",
                             exch=(dw_in_t.reshape(NDEV, PROJ // NDEV, D), "scatter"))
    d_hp, dg_mix_pre = pre_mix_bwd(hp, d_hn1, norm_mix_pre, d_h1)
    grad_x = d_hp[FRONT:][None]
    small = jnp.concatenate([dg_mix_pre, dg_mix_post, dg_ffn_pre, dg_ffn_post, d_hp[META0:FRONT]], axis=0)
    ssum = sum_slots(gather_small(small), "sum_small")
    g = {"w_in_t": sum_slots(landed_in, "sum_grads_in")}
    gsum = sum_slots(landed_rest, "sum_grads_rest")
    off = 0
    for n, r in rest_rows:
        g[n] = gsum[off:off + r]
        off += r
    gain = lambda k: ssum[k * GAIN_ROWS:k * GAIN_ROWS + 1]
    g_w = {"meta_tokens": lax.dynamic_slice(ssum[4 * GAIN_ROWS:], (0, me * (D // NDEV)), (N_META, D // NDEV)),
           "w_in": g["w_in_t"].T[None], "w_ret_out": g["w_ret_out"][None], "w_sb_out": g["w_sb_out"][None],
           "w_out": g["w_out"][None], "w_ffn_in": g["w_ffn_in_t"].T[None], "w_ffn_out": g["w_ffn_out"][None],
           "norm_mix_pre": gain(0), "norm_mix_post": gain(1), "norm_ffn_pre": gain(2), "norm_ffn_post": gain(3)}

    names = ["meta_tokens", "w_in", "w_ret_out", "w_sb_out", "w_out", "w_ffn_in", "w_ffn_out",
             "norm_mix_pre", "norm_mix_post", "norm_ffn_pre", "norm_ffn_post"]
    w_of = dict(zip(names, (meta_tokens, w_in, w_ret_out, w_sb_out, w_out, w_ffn_in, w_ffn_out,
                            norm_mix_pre, norm_mix_post, norm_ffn_pre, norm_ffn_post)))
    m_of = dict(zip(names, (m_meta_tokens, m_w_in, m_w_ret_out, m_w_sb_out, m_w_out, m_w_ffn_in, m_w_ffn_out,
                            m_norm_mix_pre, m_norm_mix_post, m_norm_ffn_pre, m_norm_ffn_post)))
    v_of = dict(zip(names, (v_meta_tokens, v_w_in, v_w_ret_out, v_w_sb_out, v_w_out, v_w_ffn_in, v_w_ffn_out,
                            v_norm_mix_pre, v_norm_mix_post, v_norm_ffn_pre, v_norm_ffn_post)))
    delta, new_m, new_v = {}, {}, {}
    for n in names:
        shape = w_of[n].shape
        two_d = (shape[-2], shape[-1])
        d_, m_, v_ = adamw(w_of[n].reshape(two_d), g_w[n].reshape(two_d), m_of[n].reshape(two_d),
                           v_of[n].reshape(two_d), "adamw_" + n)
        delta[n], new_m[n], new_v[n] = d_.reshape(shape), m_.reshape(shape), v_.reshape(shape)

    return (loss, grad_x, *[g_w[n] for n in names], *[delta[n] for n in names],
            *[new_m[n] for n in names], *[new_v[n] for n in names])
```
